```python
import math
import jax, jax.numpy as jnp
from jax import lax
import numpy as np

D_MODEL = 1024
BATCH = 8
SEQ = 2048
DEPTH = 2
DEC_BATCH = 128
DEC_SEQ = 4
PAST_LEN = 2048
PAGE_SIZE = 128

N_A_LAYERS = DEPTH // 2
N_B_LAYERS = DEPTH - N_A_LAYERS
HEAD_DIM = 64
MIX_WIDTH = 3 * D_MODEL // 4
MEM_WIDTH = D_MODEL - MIX_WIDTH
MEM_HEADS = MEM_WIDTH // HEAD_DIM
N_MEM = 256
SSM_GROUP = 16
SSM_GROUPS = MIX_WIDTH // SSM_GROUP
SSM_STATE = 64
DIL_GROUPS = ((128, 1), (512, 4), (2048, 16))
N_DIL = len(DIL_GROUPS)
ATT_HEADS = MIX_WIDTH // HEAD_DIM
HEADS_PER_GROUP = ATT_HEADS // N_DIL
IN_SPLITS = (MIX_WIDTH, 2 * MIX_WIDTH, 2 * MIX_WIDTH + MEM_WIDTH)
IN_WIDTH = 2 * MIX_WIDTH + 2 * MEM_WIDTH
OUT_WIDTH = MIX_WIDTH + MEM_WIDTH
DEEPNORM_ALPHA = (2.0 * DEPTH) ** 0.25
DEEPNORM_BETA = (8.0 * DEPTH) ** -0.25
LN_EPS = 1e-5
SCALE = HEAD_DIM ** -0.5
NEG_INF = -1e30
DT_MIN, DT_MAX = 1e-3, 1e-1

kernel_name = "yoco_s5_dilated_attention_decode_step"


def layer_norm(x, g, b):
    xf = x.astype(jnp.float32)
    mu = jnp.mean(xf, axis=-1, keepdims=True)
    var = jnp.mean(jnp.square(xf - mu), axis=-1, keepdims=True)
    return ((xf - mu) * lax.rsqrt(var + LN_EPS) * g + b).astype(x.dtype)


def alibi_slopes():
    return 2.0 ** (-8.0 * jnp.arange(1, ATT_HEADS + 1, dtype=jnp.float32) / ATT_HEADS)


def _linear_combine(left, right):
    a_l, b_l = left
    a_r, b_r = right
    return a_l * a_r, a_r * b_l + b_r


def s5_branch(u, h0_re, h0_im, lam_re, lam_im, log_dt, b_re, b_im, c_re, c_im, d_skip, w_glu, b_glu):
    f32 = jnp.float32
    Bn, T, _ = u.shape
    lam = lax.complex(jnp.minimum(lam_re.astype(f32), -1e-4), lam_im.astype(f32))
    dt = jnp.exp(log_dt.astype(f32))[:, None]
    lam_bar = jnp.exp(lam * dt)
    b_bar = ((lam_bar - 1.0) / lam)[:, :, None] * lax.complex(b_re.astype(f32), b_im.astype(f32))
    c = lax.complex(c_re.astype(f32), c_im.astype(f32))
    h0 = lax.complex(h0_re.astype(f32), h0_im.astype(f32))
    ug = u.astype(f32).reshape(Bn, T, SSM_GROUPS, SSM_GROUP)
    bu = jnp.einsum('btgc,gpc->btgp', ug.astype(jnp.complex64), b_bar)
    bu = bu.at[:, 0].add(lam_bar * h0)
    a = jnp.broadcast_to(lam_bar, bu.shape)
    _, h = lax.associative_scan(_linear_combine, (a, bu), axis=1)
    y = jnp.real(jnp.einsum('btgp,gcp->btgc', h, c)) + d_skip.astype(f32) * ug
    y = jax.nn.gelu(y.reshape(Bn, T, MIX_WIDTH))
    y = y * jax.nn.sigmoid(y @ w_glu.astype(f32) + b_glu.astype(f32))
    h_last = h[:, -1]
    return y, (jnp.real(h_last), jnp.imag(h_last))


def mem_attention(q, mem_k, mem_v):
    Bn, T, _ = q.shape
    q = q.reshape(Bn, T, MEM_HEADS, HEAD_DIM)
    s = jnp.einsum('bthe,bmhe->bhtm', q, mem_k).astype(jnp.float32) * SCALE
    p = jax.nn.softmax(s, axis=-1)
    o = jnp.einsum('bhtm,bmhe->bthe', p.astype(mem_v.dtype), mem_v)
    return o.reshape(Bn, T, MEM_WIDTH)


def _dilated_group_full(q, k, v, slopes, window, dil):
    f32 = jnp.float32
    Bn, S, H, E = q.shape
    n = window // dil
    L = S // dil
    nblk = -(-L // n)
    Lp = nblk * n

    def strided(t, front):
        t = t.reshape(Bn, L, dil, H, E).transpose(0, 2, 1, 3, 4)
        return jnp.pad(t, ((0, 0), (0, 0), (front, Lp - L), (0, 0), (0, 0)))

    def banded(t):
        t = strided(t, n).reshape(Bn, dil, nblk + 1, n, H, E)
        return jnp.concatenate([t[:, :, :-1], t[:, :, 1:]], axis=3)

    qb = strided(q, 0).reshape(Bn, dil, nblk, n, H, E)
    kb, vb = banded(k), banded(v)
    i = jnp.arange(n)[:, None]
    u = jnp.arange(2 * n)[None, :]
    delta = i + n - u
    key_pos = (jnp.arange(nblk)[:, None, None] - 1) * n + u
    valid = (delta >= 0) & (delta <= n) & (key_pos >= 0)
    bias = -slopes.astype(f32)[:, None, None] * (delta * dil).astype(f32)
    s = jnp.einsum('brjihe,brjuhe->brjhiu', qb, kb).astype(f32) * SCALE + bias
    s = jnp.where(valid[:, None], s, NEG_INF)
    lse = jax.nn.logsumexp(s, axis=-1)
    p = jnp.exp(s - lse[..., None])
    o = jnp.einsum('brjhiu,brjuhe->brjihe', p.astype(vb.dtype), vb)
    o = o.reshape(Bn, dil, Lp, H, E)[:, :, :L].transpose(0, 2, 1, 3, 4).reshape(Bn, S, H, E)
    lse = lse.transpose(0, 1, 2, 4, 3).reshape(Bn, dil, Lp, H)[:, :, :L]
    lse = lse.transpose(0, 2, 1, 3).reshape(Bn, S, H)
    return o, lse


def _dilated_group_cached(q, k_new, v_new, buf_k, buf_v, slopes, window, dil):
    f32 = jnp.float32
    T = q.shape[1]
    Lb = buf_k.shape[1]
    n = window // dil
    kc = jnp.concatenate([buf_k, k_new], axis=1)
    vc = jnp.concatenate([buf_v, v_new], axis=1)
    dist = jnp.arange(n + 1) * dil
    idx = Lb + jnp.arange(T)[:, None] - dist[None, :]
    valid = idx >= 0
    idx = jnp.maximum(idx, 0)
    kg = kc[:, idx]
    vg = vc[:, idx]
    bias = -slopes.astype(f32)[:, None, None] * dist.astype(f32)
    s = jnp.einsum('bthe,btmhe->bhtm', q, kg).astype(f32) * SCALE + bias
    s = jnp.where(valid, s, NEG_INF)
    lse = jax.nn.logsumexp(s, axis=-1)
    p = jnp.exp(s - lse[..., None])
    o = jnp.einsum('bhtm,btmhe->bthe', p.astype(vg.dtype), vg)
    return o, lse.transpose(0, 2, 1)


def _merge_groups(outs, lses):
    w = jax.nn.softmax(jnp.stack(lses, axis=0), axis=0)
    o = jnp.concatenate([outs[g].astype(jnp.float32) * w[g][..., None] for g in range(N_DIL)], axis=2)
    Bn, T = o.shape[0], o.shape[1]
    return o.reshape(Bn, T, MIX_WIDTH)


def dilated_attention_prompt(q, kv):
    slopes = alibi_slopes()
    outs, lses = [], []
    for g, (win, dil) in enumerate(DIL_GROUPS):
        hs = slice(g * HEADS_PER_GROUP, (g + 1) * HEADS_PER_GROUP)
        o, lse = _dilated_group_full(q[:, :, hs], kv[:, :, 0, hs], kv[:, :, 1, hs], slopes[hs], win, dil)
        outs.append(o)
        lses.append(lse)
    return _merge_groups(outs, lses)


def dilated_attention_sample(q, kv, buffers):
    slopes = alibi_slopes()
    outs, lses = [], []
    for g, (win, dil) in enumerate(DIL_GROUPS):
        hs = slice(g * HEADS_PER_GROUP, (g + 1) * HEADS_PER_GROUP)
        buf = buffers[g]
        o, lse = _dilated_group_cached(q[:, :, hs], kv[:, :, 0, hs], kv[:, :, 1, hs],
                                       buf[:, :, 0], buf[:, :, 1], slopes[hs], win, dil)
        outs.append(o)
        lses.append(lse)
    return _merge_groups(outs, lses)


def mixer_layer(x, mem_k, mem_v, branch, w_in, w_out, ln_g, ln_b):
    proj = x @ w_in
    u, gate, mq, mgate = jnp.split(proj, IN_SPLITS, axis=-1)
    y, aux = branch(u)
    y = y.astype(x.dtype) * jax.nn.silu(gate)
    m = mem_attention(mq, mem_k, mem_v).astype(x.dtype) * jax.nn.silu(mgate)
    out = jnp.concatenate([y, m], axis=-1) @ w_out
    return layer_norm(DEEPNORM_ALPHA * x + out, ln_g, ln_b), aux


def run_trunk(x, mem_kv, h0_re, h0_im, attn_fn, w_in, w_out, ln_g, ln_b, ssm_lambda_re, ssm_lambda_im,
              ssm_log_dt, ssm_b_re, ssm_b_im, ssm_c_re, ssm_c_im, ssm_d, w_glu, b_glu, w_kv_shared):
    h_re, h_im = [], []
    kv = None
    for l in range(DEPTH):
        mem_k, mem_v = mem_kv[l][:, :, 0], mem_kv[l][:, :, 1]
        if l < N_A_LAYERS:
            def branch(u, l=l):
                return s5_branch(u, h0_re[l], h0_im[l], ssm_lambda_re[l], ssm_lambda_im[l], ssm_log_dt[l],
                                 ssm_b_re[l], ssm_b_im[l], ssm_c_re[l], ssm_c_im[l], ssm_d[l], w_glu[l], b_glu[l])
            x, (hr, hi) = mixer_layer(x, mem_k, mem_v, branch, w_in[l], w_out[l], ln_g[l], ln_b[l])
            h_re.append(hr)
            h_im.append(hi)
            if l == N_A_LAYERS - 1:
                Bn, T, _ = x.shape
                kv = (x @ w_kv_shared).reshape(Bn, T, 2, ATT_HEADS, HEAD_DIM)
        else:
            def branch(u, kv=kv):
                Bn, T, _ = u.shape
                return attn_fn(u.reshape(Bn, T, ATT_HEADS, HEAD_DIM), kv), None
            x, _ = mixer_layer(x, mem_k, mem_v, branch, w_in[l], w_out[l], ln_g[l], ln_b[l])
    return x, jnp.stack(h_re, axis=0), jnp.stack(h_im, axis=0), kv


def setup_inputs(seed: int = 0) -> dict:
    key = jax.random.key(seed)
    ks = jax.random.split(key, 28)
    f32 = jnp.float32

    def nrm(k, shape, scale=1.0):
        return jax.random.normal(k, shape, f32) * scale

    win_lens = [min(w, PAST_LEN) for (w, _) in DIL_GROUPS]
    n_idx = jnp.arange(SSM_STATE, dtype=f32)
    return {
        "x_prompt": nrm(ks[0], (BATCH, SEQ, D_MODEL)),
        "x_sample": nrm(ks[1], (DEC_BATCH, DEC_SEQ, D_MODEL)),
        "cache_mem_kv": nrm(ks[2], (DEPTH, DEC_BATCH, N_MEM, 2, MEM_HEADS, HEAD_DIM)),
        "state_ssm_re": nrm(ks[3], (N_A_LAYERS, DEC_BATCH, SSM_GROUPS, SSM_STATE), 0.3),
        "state_ssm_im": nrm(ks[4], (N_A_LAYERS, DEC_BATCH, SSM_GROUPS, SSM_STATE), 0.3),
        "cache_dil1_kv": nrm(ks[5], (DEC_BATCH, win_lens[0], 2, HEADS_PER_GROUP, HEAD_DIM)),
        "cache_dil4_kv": nrm(ks[6], (DEC_BATCH, win_lens[1], 2, HEADS_PER_GROUP, HEAD_DIM)),
        "cache_dil16_kv": nrm(ks[7], (DEC_BATCH, win_lens[2], 2, HEADS_PER_GROUP, HEAD_DIM)),
        "mem_prompt": nrm(ks[8], (BATCH, N_MEM, D_MODEL)),
        "w_in": nrm(ks[9], (DEPTH, D_MODEL, IN_WIDTH), D_MODEL ** -0.5),
        "w_out": nrm(ks[10], (DEPTH, OUT_WIDTH, D_MODEL), OUT_WIDTH ** -0.5 * DEEPNORM_BETA),
        "ln_g": 1.0 + nrm(ks[11], (DEPTH, D_MODEL), 0.01),
        "ln_b": nrm(ks[12], (DEPTH, D_MODEL), 0.01),
        "w_mem_kv": jnp.concatenate([nrm(ks[13], (DEPTH, D_MODEL, MEM_WIDTH), D_MODEL ** -0.5),
                                     nrm(ks[14], (DEPTH, D_MODEL, MEM_WIDTH), D_MODEL ** -0.5 * DEEPNORM_BETA)], axis=-1),
        "ssm_lambda_re": -0.5 + nrm(ks[15], (N_A_LAYERS, SSM_GROUPS, SSM_STATE), 0.01),
        "ssm_lambda_im": math.pi * n_idx + nrm(ks[16], (N_A_LAYERS, SSM_GROUPS, SSM_STATE), 0.01),
        "ssm_log_dt": jax.random.uniform(ks[17], (N_A_LAYERS, SSM_GROUPS), f32, math.log(DT_MIN), math.log(DT_MAX)),
        "ssm_b_re": nrm(ks[18], (N_A_LAYERS, SSM_GROUPS, SSM_STATE, SSM_GROUP), (2 * SSM_GROUP) ** -0.5),
        "ssm_b_im": nrm(ks[19], (N_A_LAYERS, SSM_GROUPS, SSM_STATE, SSM_GROUP), (2 * SSM_GROUP) ** -0.5),
        "ssm_c_re": nrm(ks[20], (N_A_LAYERS, SSM_GROUPS, SSM_GROUP, SSM_STATE), 0.5),
        "ssm_c_im": nrm(ks[21], (N_A_LAYERS, SSM_GROUPS, SSM_GROUP, SSM_STATE), 0.5),
        "ssm_d": nrm(ks[22], (N_A_LAYERS, SSM_GROUPS, SSM_GROUP)),
        "w_glu": nrm(ks[23], (N_A_LAYERS, MIX_WIDTH, MIX_WIDTH), MIX_WIDTH ** -0.5),
        "b_glu": nrm(ks[24], (N_A_LAYERS, MIX_WIDTH), 0.01),
        "w_kv_shared": jnp.concatenate([nrm(ks[25], (D_MODEL, MIX_WIDTH), D_MODEL ** -0.5),
                                        nrm(ks[26], (D_MODEL, MIX_WIDTH), D_MODEL ** -0.5 * DEEPNORM_BETA)], axis=-1),
    }


def reference(x_prompt, x_sample, cache_mem_kv, state_ssm_re, state_ssm_im, cache_dil1_kv, cache_dil4_kv,
              cache_dil16_kv, mem_prompt, w_in, w_out, ln_g, ln_b, w_mem_kv, ssm_lambda_re, ssm_lambda_im,
              ssm_log_dt, ssm_b_re, ssm_b_im, ssm_c_re, ssm_c_im, ssm_d, w_glu, b_glu, w_kv_shared):
    weights = (w_in, w_out, ln_g, ln_b, ssm_lambda_re, ssm_lambda_im, ssm_log_dt, ssm_b_re, ssm_b_im,
               ssm_c_re, ssm_c_im, ssm_d, w_glu, b_glu, w_kv_shared)

    Bp = x_prompt.shape[0]
    mem_kv_prompt = jnp.einsum('bmd,ldk->lbmk', mem_prompt, w_mem_kv).reshape(
        DEPTH, Bp, N_MEM, 2, MEM_HEADS, HEAD_DIM)
    zeros = jnp.zeros((N_A_LAYERS, Bp, SSM_GROUPS, SSM_STATE), jnp.float32)
    y_prompt, ssm_re_prompt, ssm_im_prompt, kv_p = run_trunk(
        x_prompt, mem_kv_prompt, zeros, zeros, dilated_attention_prompt, *weights)
    S = kv_p.shape[1]
    win_p = [kv_p[:, S - min(win, S):, :, g * HEADS_PER_GROUP:(g + 1) * HEADS_PER_GROUP]
             for g, (win, _) in enumerate(DIL_GROUPS)]
    dil1_kv_prompt, dil4_kv_prompt, dil16_kv_prompt = win_p

    buffers = (cache_dil1_kv, cache_dil4_kv, cache_dil16_kv)

    def attn_sample(q, kv):
        return dilated_attention_sample(q, kv, buffers)

    y_sample, ssm_re_sample, ssm_im_sample, kv_s = run_trunk(
        x_sample, cache_mem_kv, state_ssm_re, state_ssm_im, attn_sample, *weights)
    win_s = []
    for g, (win, _) in enumerate(DIL_GROUPS):
        full = jnp.concatenate([buffers[g], kv_s[:, :, :, g * HEADS_PER_GROUP:(g + 1) * HEADS_PER_GROUP]], axis=1)
        keep = min(win, full.shape[1])
        win_s.append(full[:, full.shape[1] - keep:])
    dil1_kv_sample, dil4_kv_sample, dil16_kv_sample = win_s

    return (y_prompt, y_sample, mem_kv_prompt, ssm_re_prompt, ssm_im_prompt, dil1_kv_prompt, dil4_kv_prompt,
            dil16_kv_prompt, ssm_re_sample, ssm_im_sample, dil1_kv_sample, dil4_kv_sample, dil16_kv_sample)
```

```python
import functools
import math

import jax
import jax.numpy as jnp
from jax import lax
from jax.experimental import pallas as pl
from jax.experimental.pallas import tpu as pltpu

F32 = jnp.float32
BF16 = jnp.bfloat16

D_MODEL = 1024
HEAD_DIM = 64
MIX_WIDTH = 768
MEM_WIDTH = 256
MEM_HEADS = 4
N_MEM = 256
SSM_GROUP = 16
SSM_GROUPS = 48
SSM_PAIRS = SSM_GROUPS // 2
SSM_STATE = 64
DIL_GROUPS = ((128, 1), (512, 4), (2048, 16))
ATT_HEADS = 12
HEADS_PER_GROUP = 4
GROUP_WIDTH = HEADS_PER_GROUP * HEAD_DIM
DEPTH = 2
DEEPNORM_ALPHA = (2.0 * DEPTH) ** 0.25
LN_EPS = 1e-5
SCALE = HEAD_DIM ** -0.5
NEG_INF = -1e30
WINDOW_KEYS = 128
LANES = 128
VMEM_LIMIT = 56 * 1024 * 1024


def _slope(head):
    return 2.0 ** (-8.0 * (head + 1) / ATT_HEADS)


def _params(n_parallel=1):
    return pltpu.CompilerParams(dimension_semantics=("parallel",) * n_parallel,
                                vmem_limit_bytes=VMEM_LIMIT)


def _silu(x):
    return x * jax.nn.sigmoid(x)


def _proj_kernel(x_ref, w_ref, *out_refs, splits):
    x = x_ref[...].astype(BF16)
    off = 0
    for o_ref, n in zip(out_refs, splits):
        o_ref[...] = jnp.dot(x, w_ref[:, off:off + n], preferred_element_type=F32)
        off += n


def _proj(x, w, splits, tm):
    m, k = x.shape
    n_total = w.shape[1]
    assert sum(splits) == n_total and m % tm == 0
    return pl.pallas_call(
        functools.partial(_proj_kernel, splits=splits),
        grid=(m // tm,),
        in_specs=[pl.BlockSpec((tm, k), lambda i: (i, 0)),
                  pl.BlockSpec((k, n_total), lambda i: (0, 0))],
        out_specs=[pl.BlockSpec((tm, n), lambda i: (i, 0)) for n in splits],
        out_shape=[jax.ShapeDtypeStruct((m, n), F32) for n in splits],
        compiler_params=_params(),
        name="proj",
    )(x, w)


def _s5_tables(chunk, lam_re, lam_im, log_dt, b_re, b_im, c_re, c_im, d_skip):
    hp = lax.Precision.HIGHEST
    g, p, c = SSM_GROUPS, SSM_STATE, SSM_GROUP
    kl = -(-(c * chunk) // LANES) * LANES
    lr = jnp.minimum(lam_re.astype(F32), -1e-4)
    li = lam_im.astype(F32)
    dt = jnp.exp(log_dt.astype(F32))[:, None]
    steps = jnp.arange(chunk + 1, dtype=F32)[None, :, None]
    mag = jnp.exp(lr[:, None, :] * dt[:, None, :] * steps)
    ang = li[:, None, :] * dt[:, None, :] * steps
    pw_re, pw_im = mag * jnp.cos(ang), mag * jnp.sin(ang)
    nr, ni = pw_re[:, 1] - 1.0, pw_im[:, 1]
    den = lr * lr + li * li
    f_re, f_im = (nr * lr + ni * li) / den, (ni * lr - nr * li) / den
    bb_re = f_re[:, :, None] * b_re - f_im[:, :, None] * b_im
    bb_im = f_re[:, :, None] * b_im + f_im[:, :, None] * b_re
    c_re, c_im = c_re.astype(F32), c_im.astype(F32)

    x_re = pw_re[:, :chunk, :, None] * bb_re[:, None] - pw_im[:, :chunk, :, None] * bb_im[:, None]
    x_im = pw_re[:, :chunk, :, None] * bb_im[:, None] + pw_im[:, :chunk, :, None] * bb_re[:, None]
    conv = (jnp.einsum('gcp,gtpd->gtcd', c_re, x_re, precision=hp)
            - jnp.einsum('gcp,gtpd->gtcd', c_im, x_im, precision=hp))
    s_idx = jnp.arange(chunk)[:, None]
    t_idx = jnp.arange(chunk)[None, :]
    tau = t_idx - s_idx
    toep = conv[:, jnp.maximum(tau, 0)]
    toep = jnp.where((tau >= 0)[None, :, :, None, None], toep, 0.0)
    toep = toep.transpose(0, 1, 4, 2, 3).reshape(g, chunk * c, chunk * c)
    toep = jnp.pad(toep, ((0, 0), (0, kl - chunk * c), (0, kl - chunk * c)))

    rev_re = pw_re[:, chunk - 1::-1][:, :chunk] if chunk > 1 else pw_re[:, :1]
    rev_im = pw_im[:, chunk - 1::-1][:, :chunk] if chunk > 1 else pw_im[:, :1]
    in_re = rev_re[:, :, None, :] * bb_re.transpose(0, 2, 1)[:, None] \
        - rev_im[:, :, None, :] * bb_im.transpose(0, 2, 1)[:, None]
    in_im = rev_re[:, :, None, :] * bb_im.transpose(0, 2, 1)[:, None] \
        + rev_im[:, :, None, :] * bb_re.transpose(0, 2, 1)[:, None]
    in_re = jnp.pad(in_re.reshape(g, chunk * c, p), ((0, 0), (0, kl - chunk * c), (0, 0)))
    in_im = jnp.pad(in_im.reshape(g, chunk * c, p), ((0, 0), (0, kl - chunk * c), (0, 0)))
    zin = jnp.zeros_like(in_re)
    w_in = jnp.stack([
        jnp.concatenate([in_re, zin, in_im, zin], axis=-1),
        jnp.concatenate([zin, in_re, zin, in_im], axis=-1),
    ], axis=1)
    w_in = jnp.stack([w_in[0::2, 0], w_in[1::2, 1]], axis=1)

    e_re = pw_re[:, 1:, None, :] * c_re[:, None] - pw_im[:, 1:, None, :] * c_im[:, None]
    e_im = pw_re[:, 1:, None, :] * c_im[:, None] + pw_im[:, 1:, None, :] * c_re[:, None]
    e_re = jnp.pad(e_re.transpose(0, 3, 1, 2).reshape(g, p, chunk * c), ((0, 0), (0, 0), (0, kl - chunk * c)))
    e_im = jnp.pad(e_im.transpose(0, 3, 1, 2).reshape(g, p, chunk * c), ((0, 0), (0, 0), (0, kl - chunk * c)))
    zout = jnp.zeros_like(e_re)
    w_out_even = jnp.concatenate([e_re, zout, -e_im, zout], axis=1)
    w_out_odd = jnp.concatenate([zout, e_re, zout, -e_im], axis=1)
    w_out = jnp.stack([w_out_even[0::2], w_out_odd[1::2]], axis=1)

    lam_l = jnp.stack([pw_re[:, chunk].reshape(SSM_PAIRS, 2 * p),
                       pw_im[:, chunk].reshape(SSM_PAIRS, 2 * p)], axis=1)
    skip = jnp.pad(jnp.tile(d_skip.astype(F32), (1, chunk)), ((0, 0), (0, kl - chunk * c)))[:, None]
    return toep.astype(BF16), w_in.astype(BF16), w_out.astype(BF16), lam_l, skip


def _s5_kernel(a_ref, toep_ref, win_ref, wout_ref, lam_ref, skip_ref, h0r_ref, h0i_ref,
               y_ref, hr_ref, hi_ref, g_scr, hs_scr, *, n_chunks, batch):
    half = 2 * SSM_STATE
    a0 = a_ref[0].astype(BF16)
    a1 = a_ref[1].astype(BF16)
    g_scr[...] = (jnp.dot(a0, win_ref[0, 0], preferred_element_type=F32)
                  + jnp.dot(a1, win_ref[0, 1], preferred_element_type=F32))
    ar = lam_ref[0, 0:1, :]
    ai = lam_ref[0, 1:2, :]

    def step(k, carry):
        hr, hi = carry
        rows = pl.ds(pl.multiple_of(k * batch, 8), batch)
        hs_scr[rows, 0:half] = hr
        hs_scr[rows, half:2 * half] = hi
        gr = g_scr[rows, 0:half]
        gi = g_scr[rows, half:2 * half]
        return ar * hr - ai * hi + gr, ar * hi + ai * hr + gi

    hr, hi = lax.fori_loop(0, n_chunks, step, (h0r_ref[0], h0i_ref[0]))
    hr_ref[0] = hr
    hi_ref[0] = hi
    hs = hs_scr[...].astype(BF16)
    for e, a in enumerate((a0, a1)):
        y = (jnp.dot(a, toep_ref[e], preferred_element_type=F32)
             + jnp.dot(hs, wout_ref[0, e], preferred_element_type=F32)
             + skip_ref[e] * a_ref[e])
        y_ref[e] = jax.nn.gelu(y)


def _s5_mixer(u, h0_re, h0_im, tables, batch, seq, chunk):
    toep, w_in, w_out, lam_l, skip = tables
    g, c, p = SSM_GROUPS, SSM_GROUP, SSM_STATE
    kl = toep.shape[-1]
    n_chunks = seq // chunk
    rows = n_chunks * batch
    a = u.reshape(batch, n_chunks, chunk, g, c).transpose(3, 1, 0, 2, 4).reshape(g, rows, chunk * c)
    a = jnp.pad(a, ((0, 0), (0, 0), (0, kl - chunk * c)))
    h0r = h0_re.reshape(batch, SSM_PAIRS, 2 * p).transpose(1, 0, 2)
    h0i = h0_im.reshape(batch, SSM_PAIRS, 2 * p).transpose(1, 0, 2)
    pair3 = lambda j: (j, 0, 0)
    pair4 = lambda j: (j, 0, 0, 0)
    y, hr, hi = pl.pallas_call(
        functools.partial(_s5_kernel, n_chunks=n_chunks, batch=batch),
        grid=(SSM_PAIRS,),
        in_specs=[pl.BlockSpec((2, rows, kl), pair3),
                  pl.BlockSpec((2, kl, kl), pair3),
                  pl.BlockSpec((1, 2, kl, 4 * p), pair4),
                  pl.BlockSpec((1, 2, 4 * p, kl), pair4),
                  pl.BlockSpec((1, 2, 2 * p), pair3),
                  pl.BlockSpec((2, 1, kl), pair3),
                  pl.BlockSpec((1, batch, 2 * p), pair3),
                  pl.BlockSpec((1, batch, 2 * p), pair3)],
        out_specs=[pl.BlockSpec((2, rows, kl), pair3),
                   pl.BlockSpec((1, batch, 2 * p), pair3),
                   pl.BlockSpec((1, batch, 2 * p), pair3)],
        out_shape=[jax.ShapeDtypeStruct((g, rows, kl), F32),
                   jax.ShapeDtypeStruct((SSM_PAIRS, batch, 2 * p), F32),
                   jax.ShapeDtypeStruct((SSM_PAIRS, batch, 2 * p), F32)],
        scratch_shapes=[pltpu.VMEM((rows, 4 * p), F32), pltpu.VMEM((rows, 4 * p), F32)],
        compiler_params=_params(),
        name="s5_chunks",
    )(a, toep, w_in, w_out, lam_l, skip, h0r, h0i)
    y = y[:, :, :chunk * c].reshape(g, n_chunks, batch, chunk, c).transpose(2, 1, 3, 0, 4)
    y = y.reshape(batch * seq, g * c)
    h_re = hr.transpose(1, 0, 2).reshape(batch, g, p)
    h_im = hi.transpose(1, 0, 2).reshape(batch, g, p)
    return y, h_re, h_im


def _glu_kernel(y_ref, gate_ref, w_ref, b_ref, o_ref):
    y = y_ref[...]
    z = jnp.dot(y.astype(BF16), w_ref[...], preferred_element_type=F32) + b_ref[...]
    o_ref[...] = y * jax.nn.sigmoid(z) * _silu(gate_ref[...])


def _glu_gate(y, gate, w_glu, b_glu, tm):
    m, n = y.shape
    tok = lambda i: (i, 0)
    fixed = lambda i: (0, 0)
    return pl.pallas_call(
        _glu_kernel,
        grid=(m // tm,),
        in_specs=[pl.BlockSpec((tm, n), tok), pl.BlockSpec((tm, n), tok),
                  pl.BlockSpec((n, n), fixed), pl.BlockSpec((1, n), fixed)],
        out_specs=pl.BlockSpec((tm, n), tok),
        out_shape=jax.ShapeDtypeStruct((m, n), F32),
        compiler_params=_params(),
        name="glu_gate",
    )(y, gate, w_glu, b_glu)


def _mem_attn_kernel(q_ref, kv_ref, mg_ref, o_ref, *, block_batch):
    for bi in range(block_batch):
        q = q_ref[bi]
        kv = kv_ref[bi]
        outs = []
        for h in range(MEM_HEADS):
            lo, hi = h * HEAD_DIM, (h + 1) * HEAD_DIM
            qh = q[:, lo:hi].astype(BF16)
            kh = kv[:, lo:hi].astype(BF16)
            vh = kv[:, MEM_WIDTH + lo:MEM_WIDTH + hi].astype(BF16)
            s = lax.dot_general(qh, kh, (((1,), (1,)), ((), ())), preferred_element_type=F32) * SCALE
            p = jnp.exp(s - jnp.max(s, axis=-1, keepdims=True))
            l = jnp.sum(p, axis=-1, keepdims=True)
            outs.append(jnp.dot(p.astype(BF16), vh, preferred_element_type=F32) / l)
        o_ref[bi] = jnp.concatenate(outs, axis=1) * _silu(mg_ref[bi])


def _mem_attn(q, kv, mgate, block_batch, tq):
    b, t, w = q.shape
    tok = lambda i, j: (i, j, 0)
    return pl.pallas_call(
        functools.partial(_mem_attn_kernel, block_batch=block_batch),
        grid=(b // block_batch, t // tq),
        in_specs=[pl.BlockSpec((block_batch, tq, w), tok),
                  pl.BlockSpec((block_batch, N_MEM, 2 * w), lambda i, j: (i, 0, 0)),
                  pl.BlockSpec((block_batch, tq, w), tok)],
        out_specs=pl.BlockSpec((block_batch, tq, w), tok),
        out_shape=jax.ShapeDtypeStruct((b, t, w), F32),
        compiler_params=_params(2),
        name="mem_attn",
    )(q, kv, mgate)


def _out_ln_kernel(y_ref, m_ref, x_ref, wy_ref, wm_ref, g_ref, b_ref, o_ref):
    out = (jnp.dot(y_ref[...].astype(BF16), wy_ref[...], preferred_element_type=F32)
           + jnp.dot(m_ref[...].astype(BF16), wm_ref[...], preferred_element_type=F32))
    z = DEEPNORM_ALPHA * x_ref[...] + out
    mu = jnp.mean(z, axis=-1, keepdims=True)
    zc = z - mu
    var = jnp.mean(zc * zc, axis=-1, keepdims=True)
    o_ref[...] = zc * lax.rsqrt(var + LN_EPS) * g_ref[...] + b_ref[...]


def _out_ln(y, m, x, w_out, ln_g, ln_b, tm):
    rows, d = x.shape
    tok = lambda i: (i, 0)
    fixed = lambda i: (0, 0)
    return pl.pallas_call(
        _out_ln_kernel,
        grid=(rows // tm,),
        in_specs=[pl.BlockSpec((tm, MIX_WIDTH), tok), pl.BlockSpec((tm, MEM_WIDTH), tok),
                  pl.BlockSpec((tm, d), tok),
                  pl.BlockSpec((MIX_WIDTH, d), fixed), pl.BlockSpec((MEM_WIDTH, d), fixed),
                  pl.BlockSpec((1, d), fixed), pl.BlockSpec((1, d), fixed)],
        out_specs=pl.BlockSpec((tm, d), tok),
        out_shape=jax.ShapeDtypeStruct((rows, d), F32),
        compiler_params=_params(),
        name="out_ln",
    )(y, m, x, w_out[:MIX_WIDTH], w_out[MIX_WIDTH:], ln_g, ln_b)


def _softmax_parts(s):
    m = jnp.max(s, axis=-1, keepdims=True)
    p = jnp.exp(s - m)
    l = jnp.sum(p, axis=-1, keepdims=True)
    return p, l, m + jnp.log(l)


def _dil_prompt_kernel(q_ref, k_ref, v_ref, o_ref, lse_ref, *, seq, dil, group):
    n = WINDOW_KEYS
    n_blocks = seq // dil // n
    heads = LANES // HEAD_DIM
    first_head = group * HEADS_PER_GROUP
    second_pair = pl.program_id(1) == 1
    slopes = [jnp.where(second_pair, _slope(first_head + heads + h), _slope(first_head + h))
              for h in range(heads)]
    qi = lax.broadcasted_iota(jnp.int32, (n, 2 * n), 0)
    ku = lax.broadcasted_iota(jnp.int32, (n, 2 * n), 1)
    delta2 = qi + n - ku
    valid2 = jnp.where(delta2 >= 0, delta2, n + 1) <= n
    delta1 = delta2[:, n:]
    valid1 = delta1 >= 0
    for r in range(dil):
        for jb in range(n_blocks):
            start = r + jb * n * dil
            cur = pl.ds(start, n, stride=dil) if dil > 1 else pl.ds(start, n)
            q = q_ref[0, cur, :]
            k = k_ref[0, cur, :]
            v = v_ref[0, cur, :]
            if jb > 0:
                prev = pl.ds(start - n * dil, n, stride=dil) if dil > 1 else pl.ds(start - n * dil, n)
                k = jnp.concatenate([k_ref[0, prev, :], k], axis=0)
                v = jnp.concatenate([v_ref[0, prev, :], v], axis=0)
                delta, valid = delta2, valid2
            else:
                delta, valid = delta1, valid1
            dist = (delta * dil).astype(F32)
            outs, lses = [], []
            for h in range(heads):
                lo, hi = h * HEAD_DIM, (h + 1) * HEAD_DIM
                s = lax.dot_general(q[:, lo:hi].astype(BF16), k[:, lo:hi].astype(BF16),
                                    (((1,), (1,)), ((), ())), preferred_element_type=F32) * SCALE
                p, l, lse = _softmax_parts(jnp.where(valid, s - slopes[h] * dist, NEG_INF))
                outs.append(jnp.dot(p.astype(BF16), v[:, lo:hi].astype(BF16),
                                    preferred_element_type=F32) / l)
                lses.append(jnp.broadcast_to(lse, (n, HEAD_DIM)))
            o_ref[0, cur, :] = jnp.concatenate(outs, axis=1)
            lse_ref[0, cur, :] = jnp.concatenate(lses, axis=1)


def _dil_prompt(q, kv, group):
    b, seq, _ = q.shape
    _, dil = DIL_GROUPS[group]
    pairs = GROUP_WIDTH // LANES
    k_col = group * pairs
    v_col = MIX_WIDTH // LANES + k_col
    blk = (1, seq, LANES)
    return pl.pallas_call(
        functools.partial(_dil_prompt_kernel, seq=seq, dil=dil, group=group),
        grid=(b, pairs),
        in_specs=[pl.BlockSpec(blk, lambda i, j: (i, 0, k_col + j)),
                  pl.BlockSpec(blk, lambda i, j: (i, 0, k_col + j)),
                  pl.BlockSpec(blk, lambda i, j: (i, 0, v_col + j))],
        out_specs=[pl.BlockSpec(blk, lambda i, j: (i, 0, j)), pl.BlockSpec(blk, lambda i, j: (i, 0, j))],
        out_shape=[jax.ShapeDtypeStruct((b, seq, GROUP_WIDTH), F32)] * 2,
        compiler_params=_params(2),
        name=f"dil_prompt_g{group}",
    )(q, kv, kv)


def _merge_kernel(o0, o1, o2, l0, l1, l2, gate_ref, out_ref):
    ls = [l0[...], l1[...], l2[...]]
    mx = jnp.maximum(jnp.maximum(ls[0], ls[1]), ls[2])
    es = [jnp.exp(l - mx) for l in ls]
    den = es[0] + es[1] + es[2]
    o = jnp.concatenate([o_ref[...] * (e / den) for o_ref, e in zip((o0, o1, o2), es)], axis=1)
    out_ref[...] = o * _silu(gate_ref[...])


def _merge_gate(outs, lses, gate, tm):
    m = gate.shape[0]
    tok = lambda i: (i, 0)
    grp = pl.BlockSpec((tm, GROUP_WIDTH), tok)
    return pl.pallas_call(
        _merge_kernel,
        grid=(m // tm,),
        in_specs=[grp] * 6 + [pl.BlockSpec((tm, MIX_WIDTH), tok)],
        out_specs=pl.BlockSpec((tm, MIX_WIDTH), tok),
        out_shape=jax.ShapeDtypeStruct((m, MIX_WIDTH), F32),
        compiler_params=_params(),
        name="merge_gate",
    )(*outs, *lses, gate)


def _dil_sample_kernel(q_ref, kvn_ref, gate_ref, c0_ref, c1_ref, c2_ref,
                       y_ref, n0_ref, n1_ref, n2_ref, *, steps):
    n = WINDOW_KEYS
    gw = GROUP_WIDTH
    seg_r = lax.broadcasted_iota(jnp.int32, (gw, LANES), 0) // HEAD_DIM
    seg_c = lax.broadcasted_iota(jnp.int32, (gw, LANES), 1)
    seg = (seg_r == seg_c).astype(BF16)
    exp_r = lax.broadcasted_iota(jnp.int32, (LANES, gw), 0)
    exp_c = lax.broadcasted_iota(jnp.int32, (LANES, gw), 1) // HEAD_DIM
    expand = (exp_r == exp_c).astype(BF16)
    head_lane = lax.broadcasted_iota(jnp.int32, (1, LANES), 1)
    slopes = [jnp.zeros((1, LANES), F32) for _ in DIL_GROUPS]
    for g in range(len(DIL_GROUPS)):
        for h in range(HEADS_PER_GROUP):
            slopes[g] = jnp.where(head_lane == h, _slope(g * HEADS_PER_GROUP + h), slopes[g])
    row_j = lax.broadcasted_iota(jnp.int32, (n, LANES), 0)
    row_t = lax.broadcasted_iota(jnp.int32, (q_ref.shape[1], LANES), 0)

    def spread(x):
        x1 = x.astype(BF16)
        r1 = x - x1.astype(F32)
        x2 = r1.astype(BF16)
        x3 = (r1 - x2.astype(F32)).astype(BF16)
        return (jnp.dot(x1, expand, preferred_element_type=F32)
                + jnp.dot(x2, expand, preferred_element_type=F32)
                + jnp.dot(x3, expand, preferred_element_type=F32))

    y_ref[...] = jnp.zeros(y_ref.shape, F32)
    q_all = q_ref[0]
    kvn = kvn_ref[0]
    group_out = [[None] * steps for _ in DIL_GROUPS]
    group_lse = [[None] * steps for _ in DIL_GROUPS]
    for g, ((win, dil), c_ref, n_ref) in enumerate(zip(DIL_GROUPS, (c0_ref, c1_ref, c2_ref),
                                                       (n0_ref, n1_ref, n2_ref))):
        k_new = kvn[:, g * gw:(g + 1) * gw]
        v_new = kvn[:, MIX_WIDTH + g * gw:MIX_WIDTH + (g + 1) * gw]
        for t in range(steps):
            q = q_all[t:t + 1, g * gw:(g + 1) * gw]
            first = t if dil > 1 else 0
            k_buf = c_ref[0, :, first * 2 * gw:first * 2 * gw + gw]
            v_buf = c_ref[0, :, first * 2 * gw + gw:(first + 1) * 2 * gw]
            dist_buf = (win + t - first) - row_j * dil
            s_buf = jnp.dot((k_buf * q).astype(BF16), seg, preferred_element_type=F32) * SCALE \
                - slopes[g] * dist_buf.astype(F32)
            s_buf = jnp.where(dist_buf <= win, s_buf, NEG_INF)
            dist_new = t - row_t
            s_new = jnp.dot((k_new * q).astype(BF16), seg, preferred_element_type=F32) * SCALE \
                - slopes[g] * dist_new.astype(F32)
            ok_new = jnp.where(dist_new >= 0, dist_new & (dil - 1), 1) == 0
            s_new = jnp.where(ok_new, s_new, NEG_INF)
            m = jnp.maximum(jnp.max(s_buf, axis=0, keepdims=True), jnp.max(s_new, axis=0, keepdims=True))
            p_buf = jnp.exp(s_buf - m)
            p_new = jnp.exp(s_new - m)
            l = jnp.sum(p_buf, axis=0, keepdims=True) + jnp.sum(p_new, axis=0, keepdims=True)
            o = (jnp.sum(spread(p_buf) * v_buf, axis=0, keepdims=True)
                 + jnp.sum(spread(p_new) * v_new, axis=0, keepdims=True))
            stats = spread(jnp.concatenate([l, m + jnp.log(l)] + [jnp.zeros_like(l)] * 6, axis=0))
            group_out[g][t] = o / stats[0:1]
            group_lse[g][t] = stats[1:2]
        kv_rows = jnp.concatenate([k_new, v_new], axis=1)
        for r in range(dil):
            src, carry = (r + steps) % dil, (r + steps) // dil
            dst_cols = slice(r * 2 * gw, (r + 1) * 2 * gw)
            n_ref[0, 0:n - carry, dst_cols] = c_ref[0, carry:n, src * 2 * gw:(src + 1) * 2 * gw]
            for j in range(n - carry, n):
                tok = r + steps - (n - j) * dil
                n_ref[0, j:j + 1, dst_cols] = kv_rows[tok:tok + 1]
    for t in range(steps):
        ls = [group_lse[g][t] for g in range(len(DIL_GROUPS))]
        mx = jnp.maximum(jnp.maximum(ls[0], ls[1]), ls[2])
        es = [jnp.exp(l - mx) for l in ls]
        den = es[0] + es[1] + es[2]
        o = jnp.concatenate([group_out[g][t] * (es[g] / den) for g in range(len(DIL_GROUPS))], axis=1)
        y_ref[0, t:t + 1, :] = o * _silu(gate_ref[0, t:t + 1, :])


def _dil_sample(q, kv_new, gate, caches):
    b, steps, _ = q.shape
    rows = -(-steps // 8) * 8
    pad = ((0, 0), (0, rows - steps), (0, 0))
    q, kv_new, gate = jnp.pad(q, pad), jnp.pad(kv_new, pad), jnp.pad(gate, pad)
    tok = lambda i: (i, 0, 0)
    cache_specs = [pl.BlockSpec((1,) + c.shape[1:], tok) for c in caches]
    res = pl.pallas_call(
        functools.partial(_dil_sample_kernel, steps=steps),
        grid=(b,),
        in_specs=[pl.BlockSpec((1, rows, MIX_WIDTH), tok), pl.BlockSpec((1, rows, 2 * MIX_WIDTH), tok),
                  pl.BlockSpec((1, rows, MIX_WIDTH), tok)] + cache_specs,
        out_specs=[pl.BlockSpec((1, rows, MIX_WIDTH), tok)] + cache_specs,
        out_shape=[jax.ShapeDtypeStruct((b, rows, MIX_WIDTH), F32)]
        + [jax.ShapeDtypeStruct(c.shape, F32) for c in caches],
        compiler_params=_params(),
        name="dil_sample",
    )(q, kv_new, gate, *caches)
    return [res[0][:, :steps]] + list(res[1:])


def _trunk(x, batch, seq, mem_kv, h0_re, h0_im, chunk, weights, tm, mem_block, mem_tq, dil_attn):
    (w_in, w_out, ln_g, ln_b, tables, w_glu, b_glu, w_kv) = weights
    splits = (MIX_WIDTH, MIX_WIDTH, MEM_WIDTH, MEM_WIDTH)
    seq_mem = -(-seq // 8) * 8

    def mem_branch(mq, mgate, layer):
        mq = mq.reshape(batch, seq, MEM_WIDTH)
        mgate = mgate.reshape(batch, seq, MEM_WIDTH)
        if seq_mem != seq:
            pad = ((0, 0), (0, seq_mem - seq), (0, 0))
            mq, mgate = jnp.pad(mq, pad), jnp.pad(mgate, pad)
        m = _mem_attn(mq, mem_kv[layer], mgate, mem_block, min(mem_tq, seq_mem))
        return m[:, :seq].reshape(batch * seq, MEM_WIDTH)

    u, gate, mq, mgate = _proj(x, w_in[0], splits, tm)
    y, h_re, h_im = _s5_mixer(u, h0_re, h0_im, tables, batch, seq, chunk)
    y = _glu_gate(y, gate, w_glu, b_glu, tm)
    m = mem_branch(mq, mgate, 0)
    x = _out_ln(y, m, x, w_out[0], ln_g[0:1], ln_b[0:1], tm)
    (kv,) = _proj(x, w_kv, (2 * MIX_WIDTH,), tm)

    u, gate, mq, mgate = _proj(x, w_in[1], splits, tm)
    y, extras = dil_attn(u, kv, gate)
    m = mem_branch(mq, mgate, 1)
    x = _out_ln(y, m, x, w_out[1], ln_g[1:2], ln_b[1:2], tm)
    return x, h_re, h_im, kv, extras


def kernel(x_prompt, x_sample, cache_mem_kv, state_ssm_re, state_ssm_im, cache_dil1_kv, cache_dil4_kv,
           cache_dil16_kv, mem_prompt, w_in, w_out, ln_g, ln_b, w_mem_kv, ssm_lambda_re, ssm_lambda_im,
           ssm_log_dt, ssm_b_re, ssm_b_im, ssm_c_re, ssm_c_im, ssm_d, w_glu, b_glu, w_kv_shared):
    bp, seq, d = x_prompt.shape
    bs, steps, _ = x_sample.shape
    kv_w = 2 * MEM_WIDTH

    w_in_b = w_in.astype(BF16)
    w_out_b = w_out.astype(BF16)
    w_glu_b = w_glu[0].astype(BF16)
    w_kv_b = w_kv_shared.astype(BF16)
    ssm = (ssm_lambda_re[0], ssm_lambda_im[0], ssm_log_dt[0], ssm_b_re[0], ssm_b_im[0],
           ssm_c_re[0], ssm_c_im[0], ssm_d[0])

    def weights(chunk):
        return (w_in_b, w_out_b, ln_g, ln_b, _s5_tables(chunk, *ssm), w_glu_b, b_glu, w_kv_b)

    w_mem = jnp.concatenate([w_mem_kv[l] for l in range(DEPTH)], axis=1).astype(BF16)
    mem_kv_layers = _proj(mem_prompt.reshape(bp * N_MEM, d), w_mem, (kv_w,) * DEPTH, 512)
    mem_kv_p = [m.reshape(bp, N_MEM, kv_w) for m in mem_kv_layers]
    zeros = jnp.zeros((bp, SSM_GROUPS, SSM_STATE), F32)

    def attn_prompt(u, kv, gate):
        q3 = u.reshape(bp, seq, MIX_WIDTH)
        kv3 = kv.reshape(bp, seq, 2 * MIX_WIDTH)
        res = [_dil_prompt(q3, kv3, g) for g in range(len(DIL_GROUPS))]
        outs = [r[0].reshape(bp * seq, GROUP_WIDTH) for r in res]
        lses = [r[1].reshape(bp * seq, GROUP_WIDTH) for r in res]
        return _merge_gate(outs, lses, gate, 512), None

    y_p, hre_p, him_p, kv_p, _ = _trunk(
        x_prompt.reshape(bp * seq, d), bp, seq, mem_kv_p, zeros, zeros, 16, weights(16),
        512, 1, 512, attn_prompt)
    kv_p5 = kv_p.reshape(bp, seq, 2, ATT_HEADS, HEAD_DIM)
    win_p = [kv_p5[:, seq - min(win, seq):, :, g * HEADS_PER_GROUP:(g + 1) * HEADS_PER_GROUP]
             for g, (win, _) in enumerate(DIL_GROUPS)]

    caches = [c.reshape(bs, c.shape[1] // dil, dil * 2 * GROUP_WIDTH)
              for c, (_, dil) in zip((cache_dil1_kv, cache_dil4_kv, cache_dil16_kv), DIL_GROUPS)]
    mem_kv_s = [cache_mem_kv[l].reshape(bs, N_MEM, kv_w) for l in range(DEPTH)]

    def attn_sample(u, kv, gate):
        res = _dil_sample(u.reshape(bs, steps, MIX_WIDTH), kv.reshape(bs, steps, 2 * MIX_WIDTH),
                          gate.reshape(bs, steps, MIX_WIDTH), caches)
        return res[0].reshape(bs * steps, MIX_WIDTH), res[1:]

    y_s, hre_s, him_s, _, new_caches = _trunk(
        x_sample.reshape(bs * steps, d), bs, steps, mem_kv_s, state_ssm_re[0], state_ssm_im[0],
        steps, weights(steps), 256, 8, 8, attn_sample)
    win_s = [c.reshape(bs, -1, 2, HEADS_PER_GROUP, HEAD_DIM) for c in new_caches]

    mem_kv_out = jnp.stack(mem_kv_layers, axis=0).reshape(DEPTH, bp, N_MEM, 2, MEM_HEADS, HEAD_DIM)
    return (y_p.reshape(bp, seq, d), y_s.reshape(bs, steps, d), mem_kv_out,
            hre_p[None], him_p[None], win_p[0], win_p[1], win_p[2],
            hre_s[None], him_s[None], win_s[0], win_s[1], win_s[2])
```

```python
import functools

import jax
import jax.numpy as jnp
from jax import lax
from jax.experimental import pallas as pl
from jax.experimental.pallas import tpu as pltpu

F32 = jnp.float32
BF16 = jnp.bfloat16

D_MODEL = 1024
HEAD_DIM = 64
MIX_WIDTH = 768
MEM_WIDTH = 256
MEM_HEADS = 4
N_MEM = 256
SSM_GROUP = 16
SSM_GROUPS = 48
SSM_STATE = 64
DIL_GROUPS = ((128, 1), (512, 4), (2048, 16))
ATT_HEADS = 12
HEADS_PER_GROUP = 4
GROUP_WIDTH = HEADS_PER_GROUP * HEAD_DIM
DEPTH = 2
DEEPNORM_ALPHA = (2.0 * DEPTH) ** 0.25
LN_EPS = 1e-5
SCALE = HEAD_DIM ** -0.5
NEG_INF = -1e30
WINDOW_KEYS = 128
LANES = 128
SUBLANES = 8
SLAB_GROUPS = LANES // SSM_GROUP
SSM_SLABS = SSM_GROUPS // SLAB_GROUPS
SLAB_STATE = SLAB_GROUPS * SSM_STATE
VMEM_LIMIT = 56 * 1024 * 1024
NT_DIMS = (((1,), (1,)), ((), ()))


def _slope(head):
    return 2.0 ** (-8.0 * (head + 1) / ATT_HEADS)


def _params(*semantics):
    return pltpu.CompilerParams(dimension_semantics=semantics or ("parallel",),
                                vmem_limit_bytes=VMEM_LIMIT)


def _silu(x):
    return x * jax.nn.sigmoid(x)


def _round_up(n, m):
    return -(-n // m) * m


def _proj_kernel(x_ref, w_ref, *out_refs, splits):
    x = x_ref[...].astype(BF16)
    off = 0
    for o_ref, n in zip(out_refs, splits):
        o_ref[...] = jnp.dot(x, w_ref[:, off:off + n], preferred_element_type=F32)
        off += n


def _proj(x, w, splits, tm, first_out_block=None):
    m, k = x.shape
    n_total = w.shape[1]
    assert sum(splits) == n_total and m % tm == 0
    tok = lambda i: (i, 0)
    first = tok if first_out_block is None else (lambda i: (first_out_block(i), 0))
    out_maps = [first] + [tok] * (len(splits) - 1)
    return pl.pallas_call(
        functools.partial(_proj_kernel, splits=splits),
        grid=(m // tm,),
        in_specs=[pl.BlockSpec((tm, k), tok), pl.BlockSpec((k, n_total), lambda i: (0, 0))],
        out_specs=[pl.BlockSpec((tm, n), om) for n, om in zip(splits, out_maps)],
        out_shape=[jax.ShapeDtypeStruct((m, n), F32) for n in splits],
        compiler_params=_params(),
        name="proj",
    )(x, w)


def _mem_kv_t_kernel(w_ref, mem_ref, o_ref):
    o_ref[0, 0] = lax.dot_general(w_ref[0], mem_ref[0].astype(BF16), NT_DIMS, preferred_element_type=F32)


def _mem_kv_t(w_t, mem):
    depth, n, d = w_t.shape
    b = mem.shape[0]
    return pl.pallas_call(
        _mem_kv_t_kernel,
        grid=(depth, b),
        in_specs=[pl.BlockSpec((1, n, d), lambda l, i: (l, 0, 0)),
                  pl.BlockSpec((1, N_MEM, d), lambda l, i: (i, 0, 0))],
        out_specs=pl.BlockSpec((1, 1, n, N_MEM), lambda l, i: (l, i, 0, 0)),
        out_shape=jax.ShapeDtypeStruct((depth, b, n, N_MEM), F32),
        compiler_params=_params("parallel", "parallel"),
        name="mem_kv_t",
    )(w_t, mem)


def _s5_tables(chunk, lam_re, lam_im, log_dt, b_re, b_im, c_re, c_im, d_skip):
    hp = lax.Precision.HIGHEST
    p, c = SSM_STATE, SSM_GROUP
    ns, gs = SSM_SLABS, SLAB_GROUPS
    kl = chunk * LANES
    lr = jnp.minimum(lam_re.astype(F32), -1e-4)
    li = lam_im.astype(F32)
    dt = jnp.exp(log_dt.astype(F32))[:, None]
    def lam_bar_pow(n):
        n = n.astype(F32)[None, :, None]
        mag = jnp.exp(lr[:, None, :] * dt[:, None, :] * n)
        ang = li[:, None, :] * dt[:, None, :] * n
        return mag * jnp.cos(ang), mag * jnp.sin(ang)

    pw_re, pw_im = lam_bar_pow(jnp.arange(chunk + 1))
    nr, ni = pw_re[:, 1] - 1.0, pw_im[:, 1]
    den = lr * lr + li * li
    f_re, f_im = (nr * lr + ni * li) / den, (ni * lr - nr * li) / den
    bb_re = f_re[:, :, None] * b_re - f_im[:, :, None] * b_im
    bb_im = f_re[:, :, None] * b_im + f_im[:, :, None] * b_re
    c_re, c_im = c_re.astype(F32), c_im.astype(F32)
    eye = jnp.eye(gs, dtype=F32)

    x_re = pw_re[:, :chunk, :, None] * bb_re[:, None] - pw_im[:, :chunk, :, None] * bb_im[:, None]
    x_im = pw_re[:, :chunk, :, None] * bb_im[:, None] + pw_im[:, :chunk, :, None] * bb_re[:, None]
    conv = (jnp.einsum('gcp,gtpd->gtcd', c_re, x_re, precision=hp)
            - jnp.einsum('gcp,gtpd->gtcd', c_im, x_im, precision=hp))
    tau = jnp.arange(chunk)[None, :] - jnp.arange(chunk)[:, None]
    toep = conv[:, jnp.maximum(tau, 0)]
    toep = jnp.where((tau >= 0)[None, :, :, None, None], toep, 0.0)
    toep = toep.transpose(0, 1, 4, 2, 3).reshape(ns, gs, chunk, c, chunk, c)
    toep = jnp.einsum('jgsctd,gh->jsgcthd', toep, eye).reshape(ns, kl, kl)

    rev_re, rev_im = lam_bar_pow(chunk - 1 - jnp.arange(chunk))
    bt_re, bt_im = bb_re.transpose(0, 2, 1)[:, None], bb_im.transpose(0, 2, 1)[:, None]
    in_re = rev_re[:, :, None, :] * bt_re - rev_im[:, :, None, :] * bt_im
    in_im = rev_re[:, :, None, :] * bt_im + rev_im[:, :, None, :] * bt_re
    w_in = jnp.stack([in_re, in_im], axis=3).reshape(ns, gs, chunk, c, 2, p)
    w_in = jnp.einsum('jgscrp,gh->jsgcrhp', w_in, eye).reshape(ns, kl, 2 * SLAB_STATE)

    e_re = pw_re[:, 1:, None, :] * c_re[:, None] - pw_im[:, 1:, None, :] * c_im[:, None]
    e_im = pw_re[:, 1:, None, :] * c_im[:, None] + pw_im[:, 1:, None, :] * c_re[:, None]
    w_out = jnp.stack([e_re, -e_im], axis=1).transpose(0, 1, 4, 2, 3)
    w_out = w_out.reshape(ns, gs, 2, p, chunk, c)
    w_out = jnp.einsum('jgrptc,gh->jrgpthc', w_out, eye).reshape(ns, 2 * SLAB_STATE, kl)

    lam_l = jnp.stack([pw_re[:, chunk].reshape(ns, SLAB_STATE),
                       pw_im[:, chunk].reshape(ns, SLAB_STATE)], axis=1)
    skip = jnp.tile(d_skip.astype(F32).reshape(ns, LANES), (1, chunk))[:, None]
    return toep.astype(BF16), w_in.astype(BF16), w_out.astype(BF16), lam_l, skip


def _s5_kernel(u_ref, toep_ref, win_ref, wout_ref, lam_ref, skip_ref, h0_ref, y_ref, hout_ref,
               a_scr, g_scr, hs_scr, y_scr, h_scr, *, batch, tk, chunk):
    n_chunks = tk // chunk
    ns = SLAB_STATE

    @pl.when(pl.program_id(1) == 0)
    def _():
        h_scr[...] = h0_ref[0]

    def gather(kk, carry):
        rows = pl.ds(pl.multiple_of(kk * batch, SUBLANES), batch)
        for t in range(chunk):
            a_scr[rows, t * LANES:(t + 1) * LANES] = u_ref[pl.ds(kk * chunk + t, batch, stride=tk), :]
        return carry

    lax.fori_loop(0, n_chunks, gather, 0)
    a = a_scr[...].astype(BF16)
    g_scr[...] = jnp.dot(a, win_ref[0], preferred_element_type=F32)
    ar = lam_ref[0, 0:1, :]
    ai = lam_ref[0, 1:2, :]

    def step(kk, carry):
        hr, hi = carry
        rows = pl.ds(pl.multiple_of(kk * batch, SUBLANES), batch)
        hs_scr[rows, 0:ns] = hr
        hs_scr[rows, ns:2 * ns] = hi
        return (ar * hr - ai * hi + g_scr[rows, 0:ns], ar * hi + ai * hr + g_scr[rows, ns:2 * ns])

    hr, hi = lax.fori_loop(0, n_chunks, step, (h_scr[:, 0:ns], h_scr[:, ns:2 * ns]))
    h_scr[:, 0:ns] = hr
    h_scr[:, ns:2 * ns] = hi
    hout_ref[0, :, 0:ns] = hr
    hout_ref[0, :, ns:2 * ns] = hi
    y = (jnp.dot(a, toep_ref[0], preferred_element_type=F32)
         + jnp.dot(hs_scr[...].astype(BF16), wout_ref[0], preferred_element_type=F32)
         + skip_ref[0] * a_scr[...])
    y_scr[...] = jax.nn.gelu(y)

    def scatter(kk, carry):
        rows = pl.ds(pl.multiple_of(kk * batch, SUBLANES), batch)
        for t in range(chunk):
            y_ref[pl.ds(kk * chunk + t, batch, stride=tk), :] = y_scr[rows, t * LANES:(t + 1) * LANES]
        return carry

    lax.fori_loop(0, n_chunks, scatter, 0)


def _s5_mixer(u, h0_re, h0_im, tables, batch, seq, tk, chunk):
    toep, w_in, w_out, lam_l, skip = tables
    kl = chunk * LANES
    n_blocks = seq // tk
    rows = (tk // chunk) * batch
    to_slabs = lambda h: h.reshape(batch, SSM_SLABS, SLAB_STATE).transpose(1, 0, 2)
    h0 = jnp.concatenate([to_slabs(h0_re), to_slabs(h0_im)], axis=-1)
    slab = lambda j, k: (j, 0, 0)
    y, h = pl.pallas_call(
        functools.partial(_s5_kernel, batch=batch, tk=tk, chunk=chunk),
        grid=(SSM_SLABS, n_blocks),
        in_specs=[pl.BlockSpec((batch * tk, LANES), lambda j, k: (k, j)),
                  pl.BlockSpec((1, kl, kl), slab),
                  pl.BlockSpec((1, kl, 2 * SLAB_STATE), slab),
                  pl.BlockSpec((1, 2 * SLAB_STATE, kl), slab),
                  pl.BlockSpec((1, 2, SLAB_STATE), slab),
                  pl.BlockSpec((1, 1, kl), slab),
                  pl.BlockSpec((1, batch, 2 * SLAB_STATE), slab)],
        out_specs=[pl.BlockSpec((batch * tk, LANES), lambda j, k: (k, j)),
                   pl.BlockSpec((1, batch, 2 * SLAB_STATE), slab)],
        out_shape=[jax.ShapeDtypeStruct(u.shape, F32),
                   jax.ShapeDtypeStruct((SSM_SLABS, batch, 2 * SLAB_STATE), F32)],
        scratch_shapes=[pltpu.VMEM((rows, kl), F32), pltpu.VMEM((rows, 2 * SLAB_STATE), F32),
                        pltpu.VMEM((rows, 2 * SLAB_STATE), F32), pltpu.VMEM((rows, kl), F32),
                        pltpu.VMEM((batch, 2 * SLAB_STATE), F32)],
        compiler_params=_params("parallel", "arbitrary"),
        name="s5_chunks",
    )(u, toep, w_in, w_out, lam_l, skip, h0)
    from_slabs = lambda x: x.transpose(1, 0, 2).reshape(batch, SSM_GROUPS, SSM_STATE)
    return y, from_slabs(h[:, :, :SLAB_STATE]), from_slabs(h[:, :, SLAB_STATE:])


def _glu_kernel(y_ref, gate_ref, w_ref, b_ref, o_ref):
    y = y_ref[...]
    z = jnp.dot(y.astype(BF16), w_ref[...], preferred_element_type=F32) + b_ref[...]
    o_ref[...] = y * jax.nn.sigmoid(z) * _silu(gate_ref[...])


def _glu_gate(y, gate, w_glu, b_glu, tm, y_block=None):
    m, n = y.shape
    tok = lambda i: (i, 0)
    fixed = lambda i: (0, 0)
    y_map = tok if y_block is None else (lambda i: (y_block(i), 0))
    return pl.pallas_call(
        _glu_kernel,
        grid=(m // tm,),
        in_specs=[pl.BlockSpec((tm, n), y_map), pl.BlockSpec((tm, n), tok),
                  pl.BlockSpec((n, n), fixed), pl.BlockSpec((1, n), fixed)],
        out_specs=pl.BlockSpec((tm, n), tok),
        out_shape=jax.ShapeDtypeStruct((m, n), F32),
        compiler_params=_params(),
        name="glu_gate",
    )(y, gate, w_glu, b_glu)


def _mem_attn_kernel(q_ref, kv_ref, mg_ref, o_ref, *, block_batch):
    for bi in range(block_batch):
        q = q_ref[bi]
        outs = []
        for h in range(MEM_HEADS):
            qh = q[:, h * HEAD_DIM:(h + 1) * HEAD_DIM].astype(BF16)
            kt = kv_ref[0, bi, 0, h].astype(BF16)
            vt = kv_ref[0, bi, 1, h].astype(BF16)
            s = jnp.dot(qh, kt, preferred_element_type=F32) * SCALE
            p = jnp.exp(s - jnp.max(s, axis=-1, keepdims=True))
            l = jnp.sum(p, axis=-1, keepdims=True)
            outs.append(lax.dot_general(p.astype(BF16), vt, NT_DIMS, preferred_element_type=F32) / l)
        o_ref[bi] = jnp.concatenate(outs, axis=1) * _silu(mg_ref[bi])


def _mem_attn(q, kv_t, layer, mgate, block_batch, tq):
    b, t, w = q.shape
    tok = lambda i, j: (i, j, 0)
    return pl.pallas_call(
        functools.partial(_mem_attn_kernel, block_batch=block_batch),
        grid=(b // block_batch, t // tq),
        in_specs=[pl.BlockSpec((block_batch, tq, w), tok),
                  pl.BlockSpec((1, block_batch) + kv_t.shape[2:], lambda i, j: (layer, i, 0, 0, 0, 0)),
                  pl.BlockSpec((block_batch, tq, w), tok)],
        out_specs=pl.BlockSpec((block_batch, tq, w), tok),
        out_shape=jax.ShapeDtypeStruct((b, t, w), F32),
        compiler_params=_params("parallel", "parallel"),
        name="mem_attn",
    )(q, kv_t, mgate)


def _out_ln_kernel(y_ref, m_ref, x_ref, wy_ref, wm_ref, g_ref, b_ref, o_ref):
    out = (jnp.dot(y_ref[...].astype(BF16), wy_ref[...], preferred_element_type=F32)
           + jnp.dot(m_ref[...].astype(BF16), wm_ref[...], preferred_element_type=F32))
    z = DEEPNORM_ALPHA * x_ref[...] + out
    mu = jnp.mean(z, axis=-1, keepdims=True)
    zc = z - mu
    var = jnp.mean(zc * zc, axis=-1, keepdims=True)
    o_ref[...] = zc * lax.rsqrt(var + LN_EPS) * g_ref[...] + b_ref[...]


def _out_ln(y, m, x, w_out, ln_g, ln_b, tm):
    rows, d = x.shape
    tok = lambda i: (i, 0)
    fixed = lambda i: (0, 0)
    return pl.pallas_call(
        _out_ln_kernel,
        grid=(rows // tm,),
        in_specs=[pl.BlockSpec((tm, MIX_WIDTH), tok), pl.BlockSpec((tm, MEM_WIDTH), tok),
                  pl.BlockSpec((tm, d), tok),
                  pl.BlockSpec((MIX_WIDTH, d), fixed), pl.BlockSpec((MEM_WIDTH, d), fixed),
                  pl.BlockSpec((1, d), fixed), pl.BlockSpec((1, d), fixed)],
        out_specs=pl.BlockSpec((tm, d), tok),
        out_shape=jax.ShapeDtypeStruct((rows, d), F32),
        compiler_params=_params(),
        name="out_ln",
    )(y, m, x, w_out[:MIX_WIDTH], w_out[MIX_WIDTH:], ln_g, ln_b)


def _softmax_parts(s):
    m = jnp.max(s, axis=-1, keepdims=True)
    p = jnp.exp(s - m)
    l = jnp.sum(p, axis=-1, keepdims=True)
    return p, l, m + jnp.log(l)


def _dil_prompt_kernel(q_ref, k_ref, v_ref, o_ref, lse_ref, *, seq, dil, group):
    n = WINDOW_KEYS
    n_blocks = seq // dil // n
    heads = LANES // HEAD_DIM
    first_head = group * HEADS_PER_GROUP
    second_pair = pl.program_id(1) == 1
    slopes = [jnp.where(second_pair, _slope(first_head + heads + h), _slope(first_head + h))
              for h in range(heads)]
    qi = lax.broadcasted_iota(jnp.int32, (n, 2 * n), 0)
    ku = lax.broadcasted_iota(jnp.int32, (n, 2 * n), 1)
    delta2 = qi + n - ku
    valid2 = jnp.where(delta2 >= 0, delta2, n + 1) <= n
    delta1 = delta2[:, n:]
    valid1 = delta1 >= 0
    for r in range(dil):
        for jb in range(n_blocks):
            start = r + jb * n * dil
            cur = pl.ds(start, n, stride=dil) if dil > 1 else pl.ds(start, n)
            q = q_ref[0, cur, :]
            k = k_ref[0, cur, :]
            v = v_ref[0, cur, :]
            if jb > 0:
                prev = pl.ds(start - n * dil, n, stride=dil) if dil > 1 else pl.ds(start - n * dil, n)
                k = jnp.concatenate([k_ref[0, prev, :], k], axis=0)
                v = jnp.concatenate([v_ref[0, prev, :], v], axis=0)
                delta, valid = delta2, valid2
            else:
                delta, valid = delta1, valid1
            dist = (delta * dil).astype(F32)
            outs, lses = [], []
            for h in range(heads):
                lo, hi = h * HEAD_DIM, (h + 1) * HEAD_DIM
                s = lax.dot_general(q[:, lo:hi].astype(BF16), k[:, lo:hi].astype(BF16),
                                    NT_DIMS, preferred_element_type=F32) * SCALE
                p, l, lse = _softmax_parts(jnp.where(valid, s - slopes[h] * dist, NEG_INF))
                outs.append(jnp.dot(p.astype(BF16), v[:, lo:hi].astype(BF16),
                                    preferred_element_type=F32) / l)
                lses.append(jnp.broadcast_to(lse, (n, HEAD_DIM)))
            o_ref[0, cur, :] = jnp.concatenate(outs, axis=1)
            lse_ref[0, cur, :] = jnp.concatenate(lses, axis=1)


def _dil_prompt(q, kv, group):
    b, seq, _ = q.shape
    _, dil = DIL_GROUPS[group]
    pairs = GROUP_WIDTH // LANES
    k_col = group * pairs
    v_col = MIX_WIDTH // LANES + k_col
    blk = (1, seq, LANES)
    return pl.pallas_call(
        functools.partial(_dil_prompt_kernel, seq=seq, dil=dil, group=group),
        grid=(b, pairs),
        in_specs=[pl.BlockSpec(blk, lambda i, j: (i, 0, k_col + j)),
                  pl.BlockSpec(blk, lambda i, j: (i, 0, k_col + j)),
                  pl.BlockSpec(blk, lambda i, j: (i, 0, v_col + j))],
        out_specs=[pl.BlockSpec(blk, lambda i, j: (i, 0, j)), pl.BlockSpec(blk, lambda i, j: (i, 0, j))],
        out_shape=[jax.ShapeDtypeStruct((b, seq, GROUP_WIDTH), F32)] * 2,
        compiler_params=_params("parallel", "parallel"),
        name=f"dil_prompt_g{group}",
    )(q, kv, kv)


def _merge_kernel(o0, o1, o2, l0, l1, l2, gate_ref, out_ref):
    ls = [l0[...], l1[...], l2[...]]
    mx = jnp.maximum(jnp.maximum(ls[0], ls[1]), ls[2])
    es = [jnp.exp(l - mx) for l in ls]
    den = es[0] + es[1] + es[2]
    o = jnp.concatenate([o_ref[...] * (e / den) for o_ref, e in zip((o0, o1, o2), es)], axis=1)
    out_ref[...] = o * _silu(gate_ref[...])


def _merge_gate(outs, lses, gate, tm):
    m = gate.shape[0]
    tok = lambda i: (i, 0)
    grp = pl.BlockSpec((tm, GROUP_WIDTH), tok)
    return pl.pallas_call(
        _merge_kernel,
        grid=(m // tm,),
        in_specs=[grp] * 6 + [pl.BlockSpec((tm, MIX_WIDTH), tok)],
        out_specs=pl.BlockSpec((tm, MIX_WIDTH), tok),
        out_shape=jax.ShapeDtypeStruct((m, MIX_WIDTH), F32),
        compiler_params=_params(),
        name="merge_gate",
    )(*outs, *lses, gate)


def _dil_sample_kernel(q_ref, kvn_ref, gate_ref, c0_ref, c1_ref, c2_ref, y_ref, *, steps):
    rows = q_ref.shape[1]
    q_all = q_ref[0]
    kvn = kvn_ref[0]
    tok = lax.broadcasted_iota(jnp.int32, (rows, 1), 0)
    outs = [[None] * HEADS_PER_GROUP for _ in DIL_GROUPS]
    lses = [[None] * HEADS_PER_GROUP for _ in DIL_GROUPS]
    for g, ((win, dil), c_ref) in enumerate(zip(DIL_GROUPS, (c0_ref, c1_ref, c2_ref))):
        pos = lax.broadcasted_iota(jnp.int32, (rows, win), 1)
        dist_buf = (win + tok) - pos
        valid_buf = jnp.where(pos >= tok, (pos - tok) & (dil - 1), 1) == 0
        dist_buf_f = dist_buf.astype(F32)
        for h in range(HEADS_PER_GROUP):
            slope = _slope(g * HEADS_PER_GROUP + h)
            lo = g * GROUP_WIDTH + h * HEAD_DIM
            qh = q_all[:, lo:lo + HEAD_DIM]
            k_new = kvn[:, lo:lo + HEAD_DIM]
            v_new = kvn[:, MIX_WIDTH + lo:MIX_WIDTH + lo + HEAD_DIM]
            kt = c_ref[0, 0, h].astype(BF16)
            vt = c_ref[0, 1, h].astype(BF16)
            s_buf = jnp.dot(qh.astype(BF16), kt, preferred_element_type=F32) * SCALE - slope * dist_buf_f
            s_buf = jnp.where(valid_buf, s_buf, NEG_INF)
            s_new = []
            for t2 in range(steps):
                col = jnp.sum(qh * k_new[t2:t2 + 1, :], axis=-1, keepdims=True) * SCALE \
                    - slope * (tok - t2).astype(F32)
                ok = jnp.where(tok >= t2, (tok - t2) & (dil - 1), 1) == 0
                s_new.append(jnp.where(ok, col, NEG_INF))
            m = jnp.max(s_buf, axis=-1, keepdims=True)
            for col in s_new:
                m = jnp.maximum(m, col)
            p_buf = jnp.exp(s_buf - m)
            l = jnp.sum(p_buf, axis=-1, keepdims=True)
            o = lax.dot_general(p_buf.astype(BF16), vt, NT_DIMS, preferred_element_type=F32)
            for t2, col in enumerate(s_new):
                p_col = jnp.exp(col - m)
                l = l + p_col
                o = o + p_col * v_new[t2:t2 + 1, :]
            outs[g][h] = o / l
            lses[g][h] = m + jnp.log(l)
    pieces = [[None] * HEADS_PER_GROUP for _ in DIL_GROUPS]
    for h in range(HEADS_PER_GROUP):
        ls = [lses[g][h] for g in range(len(DIL_GROUPS))]
        mx = jnp.maximum(jnp.maximum(ls[0], ls[1]), ls[2])
        es = [jnp.exp(l - mx) for l in ls]
        den = es[0] + es[1] + es[2]
        for g in range(len(DIL_GROUPS)):
            pieces[g][h] = outs[g][h] * (es[g] / den)
    o = jnp.concatenate([pieces[g][h] for g in range(len(DIL_GROUPS)) for h in range(HEADS_PER_GROUP)], axis=1)
    y_ref[0] = o * _silu(gate_ref[0])


def _dil_sample(q, kv_new, gate, caches_t):
    b, steps, _ = q.shape
    rows = _round_up(steps, SUBLANES)
    pad = ((0, 0), (0, rows - steps), (0, 0))
    q, kv_new, gate = jnp.pad(q, pad), jnp.pad(kv_new, pad), jnp.pad(gate, pad)
    tok = lambda i: (i, 0, 0)
    y = pl.pallas_call(
        functools.partial(_dil_sample_kernel, steps=steps),
        grid=(b,),
        in_specs=[pl.BlockSpec((1, rows, MIX_WIDTH), tok), pl.BlockSpec((1, rows, 2 * MIX_WIDTH), tok),
                  pl.BlockSpec((1, rows, MIX_WIDTH), tok)]
        + [pl.BlockSpec((1,) + c.shape[1:], lambda i: (i, 0, 0, 0, 0)) for c in caches_t],
        out_specs=pl.BlockSpec((1, rows, MIX_WIDTH), tok),
        out_shape=jax.ShapeDtypeStruct((b, rows, MIX_WIDTH), F32),
        compiler_params=_params(),
        name="dil_sample",
    )(q, kv_new, gate, *caches_t)
    return y[:, :steps]


def _trunk(x, batch, seq, mem_kv_t, h0_re, h0_im, weights, tm, s5_tk, s5_chunk, mem_block, mem_tq, dil_attn):
    (w_in, w_out, ln_g, ln_b, tables, w_glu, b_glu, w_kv) = weights
    splits = (MIX_WIDTH, MIX_WIDTH, MEM_WIDTH, MEM_WIDTH)
    seq_mem = _round_up(seq, SUBLANES)

    def mem_branch(mq, mgate, layer):
        mq = mq.reshape(batch, seq, MEM_WIDTH)
        mgate = mgate.reshape(batch, seq, MEM_WIDTH)
        if seq_mem != seq:
            pad = ((0, 0), (0, seq_mem - seq), (0, 0))
            mq, mgate = jnp.pad(mq, pad), jnp.pad(mgate, pad)
        m = _mem_attn(mq, mem_kv_t, layer, mgate, mem_block, min(mem_tq, seq_mem))
        return m[:, :seq].reshape(batch * seq, MEM_WIDTH)

    if s5_tk == seq:
        block_of = None
    else:
        assert tm == s5_tk
        n_blocks = seq // s5_tk
        block_of = lambda i: (i % n_blocks) * batch + i // n_blocks
    u, gate, mq, mgate = _proj(x, w_in[0], splits, tm, first_out_block=block_of)
    y, h_re, h_im = _s5_mixer(u, h0_re, h0_im, tables, batch, seq, s5_tk, s5_chunk)
    y = _glu_gate(y, gate, w_glu, b_glu, tm, y_block=block_of)
    m = mem_branch(mq, mgate, 0)
    x = _out_ln(y, m, x, w_out[0], ln_g[0:1], ln_b[0:1], tm)
    (kv,) = _proj(x, w_kv, (2 * MIX_WIDTH,), tm)

    u, gate, mq, mgate = _proj(x, w_in[1], splits, tm)
    y = dil_attn(u, kv, gate)
    m = mem_branch(mq, mgate, 1)
    x = _out_ln(y, m, x, w_out[1], ln_g[1:2], ln_b[1:2], tm)
    return x, h_re, h_im, kv


def kernel(x_prompt, x_sample, cache_mem_kv, state_ssm_re, state_ssm_im, cache_dil1_kv, cache_dil4_kv,
           cache_dil16_kv, mem_prompt, w_in, w_out, ln_g, ln_b, w_mem_kv, ssm_lambda_re, ssm_lambda_im,
           ssm_log_dt, ssm_b_re, ssm_b_im, ssm_c_re, ssm_c_im, ssm_d, w_glu, b_glu, w_kv_shared):
    bp, seq, d = x_prompt.shape
    bs, steps, _ = x_sample.shape
    kv_w = 2 * MEM_WIDTH
    prompt_chunk, prompt_tk = 8, 512

    w_in_b = w_in.astype(BF16)
    w_out_b = w_out.astype(BF16)
    w_glu_b = w_glu[0].astype(BF16)
    w_kv_b = w_kv_shared.astype(BF16)
    ssm = (ssm_lambda_re[0], ssm_lambda_im[0], ssm_log_dt[0], ssm_b_re[0], ssm_b_im[0],
           ssm_c_re[0], ssm_c_im[0], ssm_d[0])

    def weights(chunk):
        return (w_in_b, w_out_b, ln_g, ln_b, _s5_tables(chunk, *ssm), w_glu_b, b_glu, w_kv_b)

    mem_kv_p = _mem_kv_t(w_mem_kv.transpose(0, 2, 1).astype(BF16), mem_prompt)
    mem_kv_p = mem_kv_p.reshape(DEPTH, bp, 2, MEM_HEADS, HEAD_DIM, N_MEM)
    zeros = jnp.zeros((bp, SSM_GROUPS, SSM_STATE), F32)

    def attn_prompt(u, kv, gate):
        q3 = u.reshape(bp, seq, MIX_WIDTH)
        kv3 = kv.reshape(bp, seq, 2 * MIX_WIDTH)
        res = [_dil_prompt(q3, kv3, g) for g in range(len(DIL_GROUPS))]
        outs = [r[0].reshape(bp * seq, GROUP_WIDTH) for r in res]
        lses = [r[1].reshape(bp * seq, GROUP_WIDTH) for r in res]
        return _merge_gate(outs, lses, gate, 512)

    y_p, hre_p, him_p, kv_p = _trunk(
        x_prompt.reshape(bp * seq, d), bp, seq, mem_kv_p, zeros, zeros, weights(prompt_chunk),
        prompt_tk, prompt_tk, prompt_chunk, 1, 512, attn_prompt)
    kv_p5 = kv_p.reshape(bp, seq, 2, ATT_HEADS, HEAD_DIM)
    win_p = [kv_p5[:, seq - min(win, seq):, :, g * HEADS_PER_GROUP:(g + 1) * HEADS_PER_GROUP]
             for g, (win, _) in enumerate(DIL_GROUPS)]

    caches = (cache_dil1_kv, cache_dil4_kv, cache_dil16_kv)
    caches_t = [c.transpose(0, 2, 3, 4, 1) for c in caches]
    mem_kv_s = cache_mem_kv.transpose(0, 1, 3, 4, 5, 2)

    def attn_sample(u, kv, gate):
        y = _dil_sample(u.reshape(bs, steps, MIX_WIDTH), kv.reshape(bs, steps, 2 * MIX_WIDTH),
                        gate.reshape(bs, steps, MIX_WIDTH), caches_t)
        return y.reshape(bs * steps, MIX_WIDTH)

    y_s, hre_s, him_s, kv_s = _trunk(
        x_sample.reshape(bs * steps, d), bs, steps, mem_kv_s, state_ssm_re[0], state_ssm_im[0],
        weights(steps), 256, steps, steps, 8, 8, attn_sample)
    kv_s5 = kv_s.reshape(bs, steps, 2, ATT_HEADS, HEAD_DIM)
    win_s = []
    for g, c in enumerate(caches):
        new = kv_s5[:, :, :, g * HEADS_PER_GROUP:(g + 1) * HEADS_PER_GROUP]
        win_s.append(jnp.concatenate([c, new], axis=1)[:, -c.shape[1]:])

    mem_kv_out = mem_kv_p.transpose(0, 1, 5, 2, 3, 4)
    return (y_p.reshape(bp, seq, d), y_s.reshape(bs, steps, d), mem_kv_out,
            hre_p[None], him_p[None], win_p[0], win_p[1], win_p[2],
            hre_s[None], him_s[None], win_s[0], win_s[1], win_s[2])
```

```python
import functools

import jax
import jax.numpy as jnp
from jax import lax
from jax.experimental import pallas as pl
from jax.experimental.pallas import tpu as pltpu

F32 = jnp.float32
BF16 = jnp.bfloat16

D_MODEL = 1024
HEAD_DIM = 64
MIX_WIDTH = 768
MEM_WIDTH = 256
MEM_HEADS = 4
N_MEM = 256
SSM_GROUP = 16
SSM_GROUPS = 48
SSM_STATE = 64
DIL_GROUPS = ((128, 1), (512, 4), (2048, 16))
ATT_HEADS = 12
HEADS_PER_GROUP = 4
GROUP_WIDTH = HEADS_PER_GROUP * HEAD_DIM
DEPTH = 2
DEEPNORM_ALPHA = (2.0 * DEPTH) ** 0.25
LN_EPS = 1e-5
SCALE = HEAD_DIM ** -0.5
NEG_INF = -1e30
WINDOW_KEYS = 128
LANES = 128
SUBLANES = 8
SLAB_GROUPS = LANES // SSM_GROUP
SSM_SLABS = SSM_GROUPS // SLAB_GROUPS
SLAB_STATE = SLAB_GROUPS * SSM_STATE
VMEM_LIMIT = 56 * 1024 * 1024
NT_DIMS = (((1,), (1,)), ((), ()))


def _slope(head):
    return 2.0 ** (-8.0 * (head + 1) / ATT_HEADS)


def _params(*semantics):
    return pltpu.CompilerParams(dimension_semantics=semantics or ("parallel",),
                                vmem_limit_bytes=VMEM_LIMIT)


def _silu(x):
    return x * jax.nn.sigmoid(x)


def _round_up(n, m):
    return -(-n // m) * m


def _proj_kernel(x_ref, w_ref, *out_refs, splits):
    x = x_ref[...].astype(BF16)
    off = 0
    for o_ref, n in zip(out_refs, splits):
        o_ref[...] = jnp.dot(x, w_ref[:, off:off + n], preferred_element_type=F32)
        off += n


def _proj(x, w, splits, tm, first_out_block=None):
    m, k = x.shape
    n_total = w.shape[1]
    assert sum(splits) == n_total and m % tm == 0
    tok = lambda i: (i, 0)
    first = tok if first_out_block is None else (lambda i: (first_out_block(i), 0))
    out_maps = [first] + [tok] * (len(splits) - 1)
    return pl.pallas_call(
        functools.partial(_proj_kernel, splits=splits),
        grid=(m // tm,),
        in_specs=[pl.BlockSpec((tm, k), tok), pl.BlockSpec((k, n_total), lambda i: (0, 0))],
        out_specs=[pl.BlockSpec((tm, n), om) for n, om in zip(splits, out_maps)],
        out_shape=[jax.ShapeDtypeStruct((m, n), F32) for n in splits],
        compiler_params=_params(),
        name="proj",
    )(x, w)


def _kv_proj_kernel(x_ref, w_ref, wt_all_ref, wt_tail_ref, kv_ref, kvt_all_ref, kvt_tail_ref, *, tiles):
    x = x_ref[...].astype(BF16)
    kv_ref[...] = jnp.dot(x, w_ref[...], preferred_element_type=F32)
    kvt_all_ref[0] = lax.dot_general(wt_all_ref[...], x, NT_DIMS, preferred_element_type=F32)

    @pl.when(pl.program_id(0) % tiles == tiles - 1)
    def _():
        kvt_tail_ref[0] = lax.dot_general(wt_tail_ref[...], x, NT_DIMS, preferred_element_type=F32)


def _kv_proj(x, w, wt_all, wt_tail, batch, seq, tm):
    m, k = x.shape
    n_total, n_all, n_tail = w.shape[1], wt_all.shape[0], wt_tail.shape[0]
    tiles = seq // tm
    fixed = lambda i: (0, 0)
    return pl.pallas_call(
        functools.partial(_kv_proj_kernel, tiles=tiles),
        grid=(m // tm,),
        in_specs=[pl.BlockSpec((tm, k), lambda i: (i, 0)),
                  pl.BlockSpec((k, n_total), fixed),
                  pl.BlockSpec((n_all, k), fixed),
                  pl.BlockSpec((n_tail, k), fixed)],
        out_specs=[pl.BlockSpec((tm, n_total), lambda i: (i, 0)),
                   pl.BlockSpec((1, n_all, tm), lambda i: (i // tiles, 0, i % tiles)),
                   pl.BlockSpec((1, n_tail, tm), lambda i: (i // tiles, 0, 0))],
        out_shape=[jax.ShapeDtypeStruct((m, n_total), F32),
                   jax.ShapeDtypeStruct((batch, n_all, seq), F32),
                   jax.ShapeDtypeStruct((batch, n_tail, tm), F32)],
        compiler_params=_params("arbitrary"),
        name="kv_proj",
    )(x, w, wt_all, wt_tail)


def _mem_kv_t_kernel(w_ref, mem_ref, o_ref):
    o_ref[0, 0] = lax.dot_general(w_ref[0], mem_ref[0].astype(BF16), NT_DIMS, preferred_element_type=F32)


def _mem_kv_t(w_t, mem):
    depth, n, d = w_t.shape
    b = mem.shape[0]
    return pl.pallas_call(
        _mem_kv_t_kernel,
        grid=(depth, b),
        in_specs=[pl.BlockSpec((1, n, d), lambda l, i: (l, 0, 0)),
                  pl.BlockSpec((1, N_MEM, d), lambda l, i: (i, 0, 0))],
        out_specs=pl.BlockSpec((1, 1, n, N_MEM), lambda l, i: (l, i, 0, 0)),
        out_shape=jax.ShapeDtypeStruct((depth, b, n, N_MEM), F32),
        compiler_params=_params("parallel", "parallel"),
        name="mem_kv_t",
    )(w_t, mem)


def _s5_tables(chunk, lam_re, lam_im, log_dt, b_re, b_im, c_re, c_im, d_skip):
    hp = lax.Precision.HIGHEST
    p, c = SSM_STATE, SSM_GROUP
    ns, gs = SSM_SLABS, SLAB_GROUPS
    kl = chunk * LANES
    lr = jnp.minimum(lam_re.astype(F32), -1e-4)
    li = lam_im.astype(F32)
    dt = jnp.exp(log_dt.astype(F32))[:, None]
    def lam_bar_pow(n):
        n = n.astype(F32)[None, :, None]
        mag = jnp.exp(lr[:, None, :] * dt[:, None, :] * n)
        ang = li[:, None, :] * dt[:, None, :] * n
        return mag * jnp.cos(ang), mag * jnp.sin(ang)

    pw_re, pw_im = lam_bar_pow(jnp.arange(chunk + 1))
    nr, ni = pw_re[:, 1] - 1.0, pw_im[:, 1]
    den = lr * lr + li * li
    f_re, f_im = (nr * lr + ni * li) / den, (ni * lr - nr * li) / den
    bb_re = f_re[:, :, None] * b_re - f_im[:, :, None] * b_im
    bb_im = f_re[:, :, None] * b_im + f_im[:, :, None] * b_re
    c_re, c_im = c_re.astype(F32), c_im.astype(F32)
    eye = jnp.eye(gs, dtype=F32)

    x_re = pw_re[:, :chunk, :, None] * bb_re[:, None] - pw_im[:, :chunk, :, None] * bb_im[:, None]
    x_im = pw_re[:, :chunk, :, None] * bb_im[:, None] + pw_im[:, :chunk, :, None] * bb_re[:, None]
    conv = (jnp.einsum('gcp,gtpd->gtcd', c_re, x_re, precision=hp)
            - jnp.einsum('gcp,gtpd->gtcd', c_im, x_im, precision=hp))
    conv_d = jnp.einsum('jgtcd,gh->jtgchd', conv.transpose(0, 1, 3, 2).reshape(ns, gs, chunk, c, c), eye)
    conv_d = conv_d.reshape(ns, chunk, LANES, LANES)

    def b_diag(x):
        x = x.transpose(0, 2, 1).reshape(ns, gs, c, p)
        return jnp.einsum('jgcp,gh->jgchp', x, eye).reshape(ns, 1, LANES, SLAB_STATE)

    def c_diag(x):
        x = x.transpose(0, 2, 1).reshape(ns, gs, p, c)
        return jnp.einsum('jgpc,gh->jgphc', x, eye).reshape(ns, SLAB_STATE, 1, LANES)

    def slab_rows(x):
        n = x.shape[1]
        return x.reshape(ns, gs, n, p).transpose(0, 2, 1, 3).reshape(ns, n, SLAB_STATE)

    strip = conv_d.transpose(0, 2, 1, 3).reshape(ns, LANES, kl)
    toep = jnp.stack([jnp.pad(strip[:, :, :kl - s * LANES], ((0, 0), (0, 0), (s * LANES, 0)))
                      for s in range(chunk)], axis=1).reshape(ns, kl, kl)

    rev_re, rev_im = lam_bar_pow(chunk - 1 - jnp.arange(chunk))
    rev_re, rev_im = slab_rows(rev_re)[:, :, None, :], slab_rows(rev_im)[:, :, None, :]
    bd_re, bd_im = b_diag(bb_re), b_diag(bb_im)
    w_in = jnp.concatenate([rev_re * bd_re - rev_im * bd_im,
                            rev_re * bd_im + rev_im * bd_re], axis=-1)
    w_in = w_in.reshape(ns, kl, 2 * SLAB_STATE)

    col_re = slab_rows(pw_re[:, 1:]).transpose(0, 2, 1)[:, :, :, None]
    col_im = slab_rows(pw_im[:, 1:]).transpose(0, 2, 1)[:, :, :, None]
    cd_re, cd_im = c_diag(c_re), c_diag(c_im)
    w_out = jnp.concatenate([col_re * cd_re - col_im * cd_im,
                             -(col_re * cd_im + col_im * cd_re)], axis=1)
    w_out = w_out.reshape(ns, 2 * SLAB_STATE, kl)

    lam_l = jnp.stack([pw_re[:, chunk].reshape(ns, SLAB_STATE),
                       pw_im[:, chunk].reshape(ns, SLAB_STATE)], axis=1)
    skip = jnp.tile(d_skip.astype(F32).reshape(ns, LANES), (1, chunk))[:, None]
    return toep.astype(BF16), w_in.astype(BF16), w_out.astype(BF16), lam_l, skip


def _s5_kernel(u_ref, toep_ref, win_ref, wout_ref, lam_ref, skip_ref, h0_ref, y_ref, hout_ref,
               a_scr, g_scr, hs_scr, y_scr, h_scr, *, batch, tk, chunk):
    n_chunks = tk // chunk
    ns = SLAB_STATE

    @pl.when(pl.program_id(1) == 0)
    def _():
        h_scr[...] = h0_ref[0]

    def gather(kk, carry):
        rows = pl.ds(pl.multiple_of(kk * batch, SUBLANES), batch)
        for t in range(chunk):
            a_scr[rows, t * LANES:(t + 1) * LANES] = u_ref[pl.ds(kk * chunk + t, batch, stride=tk), :]
        return carry

    lax.fori_loop(0, n_chunks, gather, 0)
    a = a_scr[...].astype(BF16)
    g_scr[...] = jnp.dot(a, win_ref[0], preferred_element_type=F32)
    ar = lam_ref[0, 0:1, :]
    ai = lam_ref[0, 1:2, :]

    def step(kk, carry):
        hr, hi = carry
        rows = pl.ds(pl.multiple_of(kk * batch, SUBLANES), batch)
        hs_scr[rows, 0:ns] = hr
        hs_scr[rows, ns:2 * ns] = hi
        return (ar * hr - ai * hi + g_scr[rows, 0:ns], ar * hi + ai * hr + g_scr[rows, ns:2 * ns])

    hr, hi = lax.fori_loop(0, n_chunks, step, (h_scr[:, 0:ns], h_scr[:, ns:2 * ns]))
    h_scr[:, 0:ns] = hr
    h_scr[:, ns:2 * ns] = hi
    hout_ref[0, :, 0:ns] = hr
    hout_ref[0, :, ns:2 * ns] = hi
    y = (jnp.dot(a, toep_ref[0], preferred_element_type=F32)
         + jnp.dot(hs_scr[...].astype(BF16), wout_ref[0], preferred_element_type=F32)
         + skip_ref[0] * a_scr[...])
    y_scr[...] = jax.nn.gelu(y)

    def scatter(kk, carry):
        rows = pl.ds(pl.multiple_of(kk * batch, SUBLANES), batch)
        for t in range(chunk):
            y_ref[pl.ds(kk * chunk + t, batch, stride=tk), :] = y_scr[rows, t * LANES:(t + 1) * LANES]
        return carry

    lax.fori_loop(0, n_chunks, scatter, 0)


def _s5_mixer(u, h0_re, h0_im, tables, batch, seq, tk, chunk):
    toep, w_in, w_out, lam_l, skip = tables
    kl = chunk * LANES
    n_blocks = seq // tk
    rows = (tk // chunk) * batch
    to_slabs = lambda h: h.reshape(batch, SSM_SLABS, SLAB_STATE).transpose(1, 0, 2)
    h0 = jnp.concatenate([to_slabs(h0_re), to_slabs(h0_im)], axis=-1)
    slab = lambda j, k: (j, 0, 0)
    y, h = pl.pallas_call(
        functools.partial(_s5_kernel, batch=batch, tk=tk, chunk=chunk),
        grid=(SSM_SLABS, n_blocks),
        in_specs=[pl.BlockSpec((batch * tk, LANES), lambda j, k: (k, j)),
                  pl.BlockSpec((1, kl, kl), slab),
                  pl.BlockSpec((1, kl, 2 * SLAB_STATE), slab),
                  pl.BlockSpec((1, 2 * SLAB_STATE, kl), slab),
                  pl.BlockSpec((1, 2, SLAB_STATE), slab),
                  pl.BlockSpec((1, 1, kl), slab),
                  pl.BlockSpec((1, batch, 2 * SLAB_STATE), slab)],
        out_specs=[pl.BlockSpec((batch * tk, LANES), lambda j, k: (k, j)),
                   pl.BlockSpec((1, batch, 2 * SLAB_STATE), slab)],
        out_shape=[jax.ShapeDtypeStruct(u.shape, F32),
                   jax.ShapeDtypeStruct((SSM_SLABS, batch, 2 * SLAB_STATE), F32)],
        scratch_shapes=[pltpu.VMEM((rows, kl), F32), pltpu.VMEM((rows, 2 * SLAB_STATE), F32),
                        pltpu.VMEM((rows, 2 * SLAB_STATE), F32), pltpu.VMEM((rows, kl), F32),
                        pltpu.VMEM((batch, 2 * SLAB_STATE), F32)],
        compiler_params=_params("parallel", "arbitrary"),
        name="s5_chunks",
    )(u, toep, w_in, w_out, lam_l, skip, h0)
    from_slabs = lambda x: x.transpose(1, 0, 2).reshape(batch, SSM_GROUPS, SSM_STATE)
    return y, from_slabs(h[:, :, :SLAB_STATE]), from_slabs(h[:, :, SLAB_STATE:])


def _glu_kernel(y_ref, gate_ref, w_ref, b_ref, o_ref):
    y = y_ref[...]
    z = jnp.dot(y.astype(BF16), w_ref[...], preferred_element_type=F32) + b_ref[...]
    o_ref[...] = y * jax.nn.sigmoid(z) * _silu(gate_ref[...])


def _glu_gate(y, gate, w_glu, b_glu, tm, y_block=None):
    m, n = y.shape
    tok = lambda i: (i, 0)
    fixed = lambda i: (0, 0)
    y_map = tok if y_block is None else (lambda i: (y_block(i), 0))
    return pl.pallas_call(
        _glu_kernel,
        grid=(m // tm,),
        in_specs=[pl.BlockSpec((tm, n), y_map), pl.BlockSpec((tm, n), tok),
                  pl.BlockSpec((n, n), fixed), pl.BlockSpec((1, n), fixed)],
        out_specs=pl.BlockSpec((tm, n), tok),
        out_shape=jax.ShapeDtypeStruct((m, n), F32),
        compiler_params=_params(),
        name="glu_gate",
    )(y, gate, w_glu, b_glu)


def _mem_attn_kernel(q_ref, kv_ref, mg_ref, o_ref, *, block_batch):
    for bi in range(block_batch):
        q = q_ref[bi]
        outs = []
        for h in range(MEM_HEADS):
            qh = q[:, h * HEAD_DIM:(h + 1) * HEAD_DIM].astype(BF16)
            kt = kv_ref[0, bi, 0, h].astype(BF16)
            vt = kv_ref[0, bi, 1, h].astype(BF16)
            s = jnp.dot(qh, kt, preferred_element_type=F32) * SCALE
            p = jnp.exp(s - jnp.max(s, axis=-1, keepdims=True))
            l = jnp.sum(p, axis=-1, keepdims=True)
            outs.append(lax.dot_general(p.astype(BF16), vt, NT_DIMS, preferred_element_type=F32) / l)
        o_ref[bi] = jnp.concatenate(outs, axis=1) * _silu(mg_ref[bi])


def _mem_attn(q, kv_t, layer, mgate, block_batch, tq):
    b, t, w = q.shape
    tok = lambda i, j: (i, j, 0)
    return pl.pallas_call(
        functools.partial(_mem_attn_kernel, block_batch=block_batch),
        grid=(b // block_batch, t // tq),
        in_specs=[pl.BlockSpec((block_batch, tq, w), tok),
                  pl.BlockSpec((1, block_batch) + kv_t.shape[2:], lambda i, j: (layer, i, 0, 0, 0, 0)),
                  pl.BlockSpec((block_batch, tq, w), tok)],
        out_specs=pl.BlockSpec((block_batch, tq, w), tok),
        out_shape=jax.ShapeDtypeStruct((b, t, w), F32),
        compiler_params=_params("parallel", "parallel"),
        name="mem_attn",
    )(q, kv_t, mgate)


def _out_ln_kernel(y_ref, m_ref, x_ref, wy_ref, wm_ref, g_ref, b_ref, o_ref):
    out = (jnp.dot(y_ref[...].astype(BF16), wy_ref[...], preferred_element_type=F32)
           + jnp.dot(m_ref[...].astype(BF16), wm_ref[...], preferred_element_type=F32))
    z = DEEPNORM_ALPHA * x_ref[...] + out
    mu = jnp.mean(z, axis=-1, keepdims=True)
    zc = z - mu
    var = jnp.mean(zc * zc, axis=-1, keepdims=True)
    o_ref[...] = zc * lax.rsqrt(var + LN_EPS) * g_ref[...] + b_ref[...]


def _out_ln(y, m, x, w_out, ln_g, ln_b, tm):
    rows, d = x.shape
    tok = lambda i: (i, 0)
    fixed = lambda i: (0, 0)
    return pl.pallas_call(
        _out_ln_kernel,
        grid=(rows // tm,),
        in_specs=[pl.BlockSpec((tm, MIX_WIDTH), tok), pl.BlockSpec((tm, MEM_WIDTH), tok),
                  pl.BlockSpec((tm, d), tok),
                  pl.BlockSpec((MIX_WIDTH, d), fixed), pl.BlockSpec((MEM_WIDTH, d), fixed),
                  pl.BlockSpec((1, d), fixed), pl.BlockSpec((1, d), fixed)],
        out_specs=pl.BlockSpec((tm, d), tok),
        out_shape=jax.ShapeDtypeStruct((rows, d), F32),
        compiler_params=_params(),
        name="out_ln",
    )(y, m, x, w_out[:MIX_WIDTH], w_out[MIX_WIDTH:], ln_g, ln_b)


def _softmax_parts(s):
    m = jnp.max(s, axis=-1, keepdims=True)
    p = jnp.exp(s - m)
    l = jnp.sum(p, axis=-1, keepdims=True)
    return p, l, m + jnp.log(l)


def _dil_prompt_kernel(q_ref, k_ref, v_ref, o_ref, lse_ref, *, seq, dil, group):
    n = WINDOW_KEYS
    n_blocks = seq // dil // n
    heads = LANES // HEAD_DIM
    first_head = group * HEADS_PER_GROUP
    second_pair = pl.program_id(1) == 1
    slopes = [jnp.where(second_pair, _slope(first_head + heads + h), _slope(first_head + h))
              for h in range(heads)]
    qi = lax.broadcasted_iota(jnp.int32, (n, 2 * n), 0)
    ku = lax.broadcasted_iota(jnp.int32, (n, 2 * n), 1)
    delta2 = qi + n - ku
    valid2 = jnp.where(delta2 >= 0, delta2, n + 1) <= n
    delta1 = delta2[:, n:]
    valid1 = delta1 >= 0
    for r in range(dil):
        for jb in range(n_blocks):
            start = r + jb * n * dil
            cur = pl.ds(start, n, stride=dil) if dil > 1 else pl.ds(start, n)
            q = q_ref[0, cur, :]
            k = k_ref[0, cur, :]
            v = v_ref[0, cur, :]
            if jb > 0:
                prev = pl.ds(start - n * dil, n, stride=dil) if dil > 1 else pl.ds(start - n * dil, n)
                k = jnp.concatenate([k_ref[0, prev, :], k], axis=0)
                v = jnp.concatenate([v_ref[0, prev, :], v], axis=0)
                delta, valid = delta2, valid2
            else:
                delta, valid = delta1, valid1
            dist = (delta * dil).astype(F32)
            outs, lses = [], []
            for h in range(heads):
                lo, hi = h * HEAD_DIM, (h + 1) * HEAD_DIM
                s = lax.dot_general(q[:, lo:hi].astype(BF16), k[:, lo:hi].astype(BF16),
                                    NT_DIMS, preferred_element_type=F32) * SCALE
                p, l, lse = _softmax_parts(jnp.where(valid, s - slopes[h] * dist, NEG_INF))
                outs.append(jnp.dot(p.astype(BF16), v[:, lo:hi].astype(BF16),
                                    preferred_element_type=F32) / l)
                lses.append(jnp.broadcast_to(lse, (n, HEAD_DIM)))
            o_ref[0, cur, :] = jnp.concatenate(outs, axis=1)
            lse_ref[0, cur, :] = jnp.concatenate(lses, axis=1)


def _dil_prompt(q, kv, group):
    b, seq, _ = q.shape
    _, dil = DIL_GROUPS[group]
    pairs = GROUP_WIDTH // LANES
    k_col = group * pairs
    v_col = MIX_WIDTH // LANES + k_col
    blk = (1, seq, LANES)
    return pl.pallas_call(
        functools.partial(_dil_prompt_kernel, seq=seq, dil=dil, group=group),
        grid=(b, pairs),
        in_specs=[pl.BlockSpec(blk, lambda i, j: (i, 0, k_col + j)),
                  pl.BlockSpec(blk, lambda i, j: (i, 0, k_col + j)),
                  pl.BlockSpec(blk, lambda i, j: (i, 0, v_col + j))],
        out_specs=[pl.BlockSpec(blk, lambda i, j: (i, 0, j)), pl.BlockSpec(blk, lambda i, j: (i, 0, j))],
        out_shape=[jax.ShapeDtypeStruct((b, seq, GROUP_WIDTH), F32)] * 2,
        compiler_params=_params("parallel", "parallel"),
        name=f"dil_prompt_g{group}",
    )(q, kv, kv)


def _merge_kernel(o0, o1, o2, l0, l1, l2, gate_ref, out_ref):
    ls = [l0[...], l1[...], l2[...]]
    mx = jnp.maximum(jnp.maximum(ls[0], ls[1]), ls[2])
    es = [jnp.exp(l - mx) for l in ls]
    den = es[0] + es[1] + es[2]
    o = jnp.concatenate([o_ref[...] * (e / den) for o_ref, e in zip((o0, o1, o2), es)], axis=1)
    out_ref[...] = o * _silu(gate_ref[...])


def _merge_gate(outs, lses, gate, tm):
    m = gate.shape[0]
    tok = lambda i: (i, 0)
    grp = pl.BlockSpec((tm, GROUP_WIDTH), tok)
    return pl.pallas_call(
        _merge_kernel,
        grid=(m // tm,),
        in_specs=[grp] * 6 + [pl.BlockSpec((tm, MIX_WIDTH), tok)],
        out_specs=pl.BlockSpec((tm, MIX_WIDTH), tok),
        out_shape=jax.ShapeDtypeStruct((m, MIX_WIDTH), F32),
        compiler_params=_params(),
        name="merge_gate",
    )(*outs, *lses, gate)


def _dil_sample_kernel(q_ref, kvn_ref, gate_ref, c0_ref, c1_ref, c2_ref,
                       y_ref, n0_ref, n1_ref, n2_ref, *, steps):
    rows = q_ref.shape[1]
    lead = rows - steps
    q_all = q_ref[0]
    kvn = kvn_ref[0]
    tok = lax.broadcasted_iota(jnp.int32, (rows, 1), 0) - lead
    kvn_t = jnp.concatenate([jnp.zeros((LANES - rows, 2 * MIX_WIDTH), F32), kvn], axis=0).T
    tail_lane = lax.broadcasted_iota(jnp.int32, (HEAD_DIM, LANES), 1) >= LANES - steps
    outs = [[None] * HEADS_PER_GROUP for _ in DIL_GROUPS]
    lses = [[None] * HEADS_PER_GROUP for _ in DIL_GROUPS]
    for g, ((win, dil), c_ref, n_ref) in enumerate(zip(DIL_GROUPS, (c0_ref, c1_ref, c2_ref),
                                                       (n0_ref, n1_ref, n2_ref))):
        pos = lax.broadcasted_iota(jnp.int32, (rows, win), 1)
        dist_buf = (win + tok) - pos
        valid_buf = jnp.where(pos >= tok, (pos - tok) & (dil - 1), 1) == 0
        dist_buf_f = dist_buf.astype(F32)
        for h in range(HEADS_PER_GROUP):
            slope = _slope(g * HEADS_PER_GROUP + h)
            lo = g * GROUP_WIDTH + h * HEAD_DIM
            qh = q_all[:, lo:lo + HEAD_DIM]
            k_new = kvn[:, lo:lo + HEAD_DIM]
            v_new = kvn[:, MIX_WIDTH + lo:MIX_WIDTH + lo + HEAD_DIM]
            kt_f = c_ref[0, 0, h]
            vt_f = c_ref[0, 1, h]
            for kv_i, src in ((0, kt_f), (1, vt_f)):
                shifted = pltpu.roll(src, win - steps, axis=1)
                new_t = kvn_t[kv_i * MIX_WIDTH + lo:kv_i * MIX_WIDTH + lo + HEAD_DIM, :]
                if win > LANES:
                    n_ref[0, kv_i, h, :, 0:win - LANES] = shifted[:, 0:win - LANES]
                n_ref[0, kv_i, h, :, win - LANES:win] = jnp.where(tail_lane, new_t, shifted[:, win - LANES:win])
            kt = kt_f.astype(BF16)
            vt = vt_f.astype(BF16)
            s_buf = jnp.dot(qh.astype(BF16), kt, preferred_element_type=F32) * SCALE - slope * dist_buf_f
            s_buf = jnp.where(valid_buf, s_buf, NEG_INF)
            s_new = []
            for t2 in range(steps):
                col =jnp.sum(qh * k_new[lead + t2:lead + t2 + 1, :], axis=-1, keepdims=True) * SCALE \
                    - slope * (tok - t2).astype(F32)
                ok = jnp.where(tok >= t2, (tok - t2) & (dil - 1), 1) == 0
                s_new.append(jnp.where(ok, col, NEG_INF))
            m = jnp.max(s_buf, axis=-1, keepdims=True)
            for col in s_new:
                m = jnp.maximum(m, col)
            p_buf = jnp.exp(s_buf - m)
            l = jnp.sum(p_buf, axis=-1, keepdims=True)
            o = lax.dot_general(p_buf.astype(BF16), vt, NT_DIMS, preferred_element_type=F32)
            for t2, col in enumerate(s_new):
                p_col = jnp.exp(col - m)
                l = l + p_col
                o = o + p_col * v_new[lead + t2:lead + t2 + 1, :]
            outs[g][h] = o / l
            lses[g][h] = m + jnp.log(l)
    pieces = [[None] * HEADS_PER_GROUP for _ in DIL_GROUPS]
    for h in range(HEADS_PER_GROUP):
        ls = [lses[g][h] for g in range(len(DIL_GROUPS))]
        mx = jnp.maximum(jnp.maximum(ls[0], ls[1]), ls[2])
        es = [jnp.exp(l - mx) for l in ls]
        den = es[0] + es[1] + es[2]
        for g in range(len(DIL_GROUPS)):
            pieces[g][h] = outs[g][h] * (es[g] / den)
    o = jnp.concatenate([pieces[g][h] for g in range(len(DIL_GROUPS)) for h in range(HEADS_PER_GROUP)], axis=1)
    y_ref[0] = o * _silu(gate_ref[0])


def _dil_sample(q, kv_new, gate, caches_t):
    b, steps, _ = q.shape
    rows = _round_up(steps, SUBLANES)
    pad = ((0, 0), (rows - steps, 0), (0, 0))
    q, kv_new, gate = jnp.pad(q, pad), jnp.pad(kv_new, pad), jnp.pad(gate, pad)
    tok = lambda i: (i, 0, 0)
    cache_specs = [pl.BlockSpec((1,) + c.shape[1:], lambda i: (i, 0, 0, 0, 0)) for c in caches_t]
    res = pl.pallas_call(
        functools.partial(_dil_sample_kernel, steps=steps),
        grid=(b,),
        in_specs=[pl.BlockSpec((1, rows, MIX_WIDTH), tok), pl.BlockSpec((1, rows, 2 * MIX_WIDTH), tok),
                  pl.BlockSpec((1, rows, MIX_WIDTH), tok)] + cache_specs,
        out_specs=[pl.BlockSpec((1, rows, MIX_WIDTH), tok)] + cache_specs,
        out_shape=[jax.ShapeDtypeStruct((b, rows, MIX_WIDTH), F32)]
        + [jax.ShapeDtypeStruct(c.shape, F32) for c in caches_t],
        compiler_params=_params(),
        name="dil_sample",
    )(q, kv_new, gate, *caches_t)
    return [res[0][:, rows - steps:]] + list(res[1:])


def _trunk(x, batch, seq, mem_kv_t, h0_re, h0_im, weights, tm, s5_tk, s5_chunk, mem_block, mem_tq,
           kv_proj, dil_attn):
    (w_in, w_out, ln_g, ln_b, tables, w_glu, b_glu) = weights
    splits = (MIX_WIDTH, MIX_WIDTH, MEM_WIDTH, MEM_WIDTH)
    seq_mem = _round_up(seq, SUBLANES)

    def mem_branch(mq, mgate, layer):
        mq = mq.reshape(batch, seq, MEM_WIDTH)
        mgate = mgate.reshape(batch, seq, MEM_WIDTH)
        if seq_mem != seq:
            pad = ((0, 0), (0, seq_mem - seq), (0, 0))
            mq, mgate = jnp.pad(mq, pad), jnp.pad(mgate, pad)
        m = _mem_attn(mq, mem_kv_t, layer, mgate, mem_block, min(mem_tq, seq_mem))
        return m[:, :seq].reshape(batch * seq, MEM_WIDTH)

    if s5_tk == seq:
        block_of = None
    else:
        assert tm == s5_tk
        n_blocks = seq // s5_tk
        block_of = lambda i: (i % n_blocks) * batch + i // n_blocks
    u, gate, mq, mgate = _proj(x, w_in[0], splits, tm, first_out_block=block_of)
    y, h_re, h_im = _s5_mixer(u, h0_re, h0_im, tables, batch, seq, s5_tk, s5_chunk)
    y = _glu_gate(y, gate, w_glu, b_glu, tm, y_block=block_of)
    m = mem_branch(mq, mgate, 0)
    x = _out_ln(y, m, x, w_out[0], ln_g[0:1], ln_b[0:1], tm)
    kv, kv_extras = kv_proj(x)

    u, gate, mq, mgate = _proj(x, w_in[1], splits, tm)
    y, attn_extras = dil_attn(u, kv, gate)
    m = mem_branch(mq, mgate, 1)
    x = _out_ln(y, m, x, w_out[1], ln_g[1:2], ln_b[1:2], tm)
    return x, h_re, h_im, kv, kv_extras, attn_extras


def kernel(x_prompt, x_sample, cache_mem_kv, state_ssm_re, state_ssm_im, cache_dil1_kv, cache_dil4_kv,
           cache_dil16_kv, mem_prompt, w_in, w_out, ln_g, ln_b, w_mem_kv, ssm_lambda_re, ssm_lambda_im,
           ssm_log_dt, ssm_b_re, ssm_b_im, ssm_c_re, ssm_c_im, ssm_d, w_glu, b_glu, w_kv_shared):
    bp, seq, d = x_prompt.shape
    bs, steps, _ = x_sample.shape
    prompt_chunk, prompt_tk = 8, 512
    n_groups = len(DIL_GROUPS)

    w_in_b = w_in.astype(BF16)
    w_out_b = w_out.astype(BF16)
    w_glu_b = w_glu[0].astype(BF16)
    w_kv_b = w_kv_shared.astype(BF16)
    ssm = (ssm_lambda_re[0], ssm_lambda_im[0], ssm_log_dt[0], ssm_b_re[0], ssm_b_im[0],
           ssm_c_re[0], ssm_c_im[0], ssm_d[0])

    def weights(chunk):
        return (w_in_b, w_out_b, ln_g, ln_b, _s5_tables(chunk, *ssm), w_glu_b, b_glu)

    mem_kv_p = _mem_kv_t(w_mem_kv.transpose(0, 2, 1).astype(BF16), mem_prompt)
    mem_kv_p = mem_kv_p.reshape(DEPTH, bp, 2, MEM_HEADS, HEAD_DIM, N_MEM)
    zeros = jnp.zeros((bp, SSM_GROUPS, SSM_STATE), F32)
    wide = n_groups - 1
    assert DIL_GROUPS[wide][0] >= seq and all(w <= prompt_tk for w, _ in DIL_GROUPS[:wide])

    def group_cols_t(g):
        return jnp.concatenate([w_kv_b[:, kv_i * MIX_WIDTH + g * GROUP_WIDTH:
                                       kv_i * MIX_WIDTH + (g + 1) * GROUP_WIDTH] for kv_i in range(2)], axis=1).T

    wt_all = group_cols_t(wide)
    wt_tail = jnp.concatenate([group_cols_t(g) for g in range(wide)], axis=0)

    def kv_prompt(x):
        kv, kvt_all, kvt_tail = _kv_proj(x, w_kv_b, wt_all, wt_tail, bp, seq, prompt_tk)
        return kv, (kvt_all, kvt_tail)

    def attn_prompt(u, kv, gate):
        q3 = u.reshape(bp, seq, MIX_WIDTH)
        kv3 = kv.reshape(bp, seq, 2 * MIX_WIDTH)
        res = [_dil_prompt(q3, kv3, g) for g in range(n_groups)]
        outs = [r[0].reshape(bp * seq, GROUP_WIDTH) for r in res]
        lses = [r[1].reshape(bp * seq, GROUP_WIDTH) for r in res]
        return _merge_gate(outs, lses, gate, 512), None

    y_p, hre_p, him_p, _, (kvt_all, kvt_tail), _ = _trunk(
        x_prompt.reshape(bp * seq, d), bp, seq, mem_kv_p, zeros, zeros, weights(prompt_chunk),
        prompt_tk, prompt_tk, prompt_chunk, 1, 512, kv_prompt, attn_prompt)
    kvt_tail = kvt_tail.reshape(bp, wide, 2, HEADS_PER_GROUP, HEAD_DIM, prompt_tk)
    win_p = [kvt_tail[:, g, :, :, :, prompt_tk - win:].transpose(0, 4, 1, 2, 3)
             for g, (win, _) in enumerate(DIL_GROUPS[:wide])]
    win_p.append(kvt_all.reshape(bp, 2, HEADS_PER_GROUP, HEAD_DIM, seq).transpose(0, 4, 1, 2, 3))

    caches = (cache_dil1_kv, cache_dil4_kv, cache_dil16_kv)
    caches_t = [c.transpose(0, 2, 3, 4, 1) for c in caches]
    mem_kv_s = cache_mem_kv.transpose(0, 1, 3, 4, 5, 2)

    def kv_sample(x):
        return _proj(x, w_kv_b, (2 * MIX_WIDTH,), 256)[0], None

    def attn_sample(u, kv, gate):
        res = _dil_sample(u.reshape(bs, steps, MIX_WIDTH), kv.reshape(bs, steps, 2 * MIX_WIDTH),
                          gate.reshape(bs, steps, MIX_WIDTH), caches_t)
        return res[0].reshape(bs * steps, MIX_WIDTH), res[1:]

    y_s, hre_s, him_s, _, _, rolled = _trunk(
        x_sample.reshape(bs * steps, d), bs, steps, mem_kv_s, state_ssm_re[0], state_ssm_im[0],
        weights(steps), 256, steps, steps, 8, 8, kv_sample, attn_sample)
    win_s = [c.transpose(0, 4, 1, 2, 3) for c in rolled]

    mem_kv_out = mem_kv_p.transpose(0, 1, 5, 2, 3, 4)
    return (y_p.reshape(bp, seq, d), y_s.reshape(bs, steps, d), mem_kv_out,
            hre_p[None], him_p[None], win_p[0], win_p[1], win_p[2],
            hre_s[None], him_s[None], win_s[0], win_s[1], win_s[2])
```

```python
import functools

import jax
import jax.numpy as jnp
from jax import lax
from jax.experimental import pallas as pl
from jax.experimental.pallas import tpu as pltpu

F32 = jnp.float32
BF16 = jnp.bfloat16

D_MODEL = 1024
HEAD_DIM = 64
MIX_WIDTH = 768
MEM_WIDTH = 256
MEM_HEADS = 4
N_MEM = 256
SSM_GROUP = 16
SSM_GROUPS = 48
SSM_STATE = 64
DIL_GROUPS = ((128, 1), (512, 4), (2048, 16))
ATT_HEADS = 12
HEADS_PER_GROUP = 4
GROUP_WIDTH = HEADS_PER_GROUP * HEAD_DIM
DEPTH = 2
DEEPNORM_ALPHA = (2.0 * DEPTH) ** 0.25
LN_EPS = 1e-5
SCALE = HEAD_DIM ** -0.5
NEG_INF = -1e30
WINDOW_KEYS = 128
LANES = 128
SUBLANES = 8
SLAB_GROUPS = LANES // SSM_GROUP
SSM_SLABS = SSM_GROUPS // SLAB_GROUPS
SLAB_STATE = SLAB_GROUPS * SSM_STATE
VMEM_LIMIT = 56 * 1024 * 1024
NT_DIMS = (((1,), (1,)), ((), ()))


def _slope(head):
    return 2.0 ** (-8.0 * (head + 1) / ATT_HEADS)


def _params(*semantics):
    return pltpu.CompilerParams(dimension_semantics=semantics or ("parallel",),
                                vmem_limit_bytes=VMEM_LIMIT)


def _silu(x):
    return x * jax.nn.sigmoid(x)


def _round_up(n, m):
    return -(-n // m) * m


def _proj_kernel(x_ref, w_ref, *out_refs, splits):
    x = x_ref[...].astype(BF16)
    off = 0
    for o_ref, n in zip(out_refs, splits):
        o_ref[...] = jnp.dot(x, w_ref[:, off:off + n], preferred_element_type=F32)
        off += n


def _proj(x, w, splits, tm, first_out_block=None):
    m, k = x.shape
    n_total = w.shape[1]
    assert sum(splits) == n_total and m % tm == 0
    tok = lambda i: (i, 0)
    first = tok if first_out_block is None else (lambda i: (first_out_block(i), 0))
    out_maps = [first] + [tok] * (len(splits) - 1)
    return pl.pallas_call(
        functools.partial(_proj_kernel, splits=splits),
        grid=(m // tm,),
        in_specs=[pl.BlockSpec((tm, k), tok), pl.BlockSpec((k, n_total), lambda i: (0, 0))],
        out_specs=[pl.BlockSpec((tm, n), om) for n, om in zip(splits, out_maps)],
        out_shape=[jax.ShapeDtypeStruct((m, n), F32) for n in splits],
        compiler_params=_params(),
        name="proj",
    )(x, w)


def _kv_proj_kernel(x_ref, w_ref, wt_all_ref, wt_tail_ref, kv_ref, kvt_all_ref, kvt_tail_ref, *, tiles):
    x = x_ref[...].astype(BF16)
    kv_ref[...] = jnp.dot(x, w_ref[...], preferred_element_type=F32)
    kvt_all_ref[0] = lax.dot_general(wt_all_ref[...], x, NT_DIMS, preferred_element_type=F32)

    @pl.when(pl.program_id(0) % tiles == tiles - 1)
    def _():
        kvt_tail_ref[0] = lax.dot_general(wt_tail_ref[...], x, NT_DIMS, preferred_element_type=F32)


def _kv_proj(x, w, wt_all, wt_tail, batch, seq, tm):
    m, k = x.shape
    n_total, n_all, n_tail = w.shape[1], wt_all.shape[0], wt_tail.shape[0]
    tiles = seq // tm
    fixed = lambda i: (0, 0)
    return pl.pallas_call(
        functools.partial(_kv_proj_kernel, tiles=tiles),
        grid=(m // tm,),
        in_specs=[pl.BlockSpec((tm, k), lambda i: (i, 0)),
                  pl.BlockSpec((k, n_total), fixed),
                  pl.BlockSpec((n_all, k), fixed),
                  pl.BlockSpec((n_tail, k), fixed)],
        out_specs=[pl.BlockSpec((tm, n_total), lambda i: (i, 0)),
                   pl.BlockSpec((1, n_all, tm), lambda i: (i // tiles, 0, i % tiles)),
                   pl.BlockSpec((1, n_tail, tm), lambda i: (i // tiles, 0, 0))],
        out_shape=[jax.ShapeDtypeStruct((m, n_total), F32),
                   jax.ShapeDtypeStruct((batch, n_all, seq), F32),
                   jax.ShapeDtypeStruct((batch, n_tail, tm), F32)],
        compiler_params=_params("arbitrary"),
        name="kv_proj",
    )(x, w, wt_all, wt_tail)


def _mem_kv_t_kernel(w_ref, mem_ref, o_ref):
    o_ref[0, 0] = lax.dot_general(w_ref[0], mem_ref[0].astype(BF16), NT_DIMS, preferred_element_type=F32)


def _mem_kv_t(w_t, mem):
    depth, n, d = w_t.shape
    b = mem.shape[0]
    return pl.pallas_call(
        _mem_kv_t_kernel,
        grid=(depth, b),
        in_specs=[pl.BlockSpec((1, n, d), lambda l, i: (l, 0, 0)),
                  pl.BlockSpec((1, N_MEM, d), lambda l, i: (i, 0, 0))],
        out_specs=pl.BlockSpec((1, 1, n, N_MEM), lambda l, i: (l, i, 0, 0)),
        out_shape=jax.ShapeDtypeStruct((depth, b, n, N_MEM), F32),
        compiler_params=_params("parallel", "parallel"),
        name="mem_kv_t",
    )(w_t, mem)


def _s5_tables(chunk, lam_re, lam_im, log_dt, b_re, b_im, c_re, c_im, d_skip):
    hp = lax.Precision.HIGHEST
    p, c = SSM_STATE, SSM_GROUP
    ns, gs = SSM_SLABS, SLAB_GROUPS
    kl = chunk * LANES
    lr = jnp.minimum(lam_re.astype(F32), -1e-4)
    li = lam_im.astype(F32)
    dt = jnp.exp(log_dt.astype(F32))[:, None]
    def lam_bar_pow(n):
        n = n.astype(F32)[None, :, None]
        mag = jnp.exp(lr[:, None, :] * dt[:, None, :] * n)
        ang = li[:, None, :] * dt[:, None, :] * n
        return mag * jnp.cos(ang), mag * jnp.sin(ang)

    pw_re, pw_im = lam_bar_pow(jnp.arange(chunk + 1))
    nr, ni = pw_re[:, 1] - 1.0, pw_im[:, 1]
    den = lr * lr + li * li
    f_re, f_im = (nr * lr + ni * li) / den, (ni * lr - nr * li) / den
    bb_re = f_re[:, :, None] * b_re - f_im[:, :, None] * b_im
    bb_im = f_re[:, :, None] * b_im + f_im[:, :, None] * b_re
    c_re, c_im = c_re.astype(F32), c_im.astype(F32)
    eye = jnp.eye(gs, dtype=F32)

    x_re = pw_re[:, :chunk, :, None] * bb_re[:, None] - pw_im[:, :chunk, :, None] * bb_im[:, None]
    x_im = pw_re[:, :chunk, :, None] * bb_im[:, None] + pw_im[:, :chunk, :, None] * bb_re[:, None]
    conv = (jnp.einsum('gcp,gtpd->gtcd', c_re, x_re, precision=hp)
            - jnp.einsum('gcp,gtpd->gtcd', c_im, x_im, precision=hp))
    conv_d = jnp.einsum('jgtcd,gh->jtgchd', conv.transpose(0, 1, 3, 2).reshape(ns, gs, chunk, c, c), eye)
    conv_d = conv_d.reshape(ns, chunk, LANES, LANES)

    def b_diag(x):
        x = x.transpose(0, 2, 1).reshape(ns, gs, c, p)
        return jnp.einsum('jgcp,gh->jgchp', x, eye).reshape(ns, 1, LANES, SLAB_STATE)

    def c_diag(x):
        x = x.transpose(0, 2, 1).reshape(ns, gs, p, c)
        return jnp.einsum('jgpc,gh->jgphc', x, eye).reshape(ns, SLAB_STATE, 1, LANES)

    def slab_rows(x):
        n = x.shape[1]
        return x.reshape(ns, gs, n, p).transpose(0, 2, 1, 3).reshape(ns, n, SLAB_STATE)

    strip = conv_d.transpose(0, 2, 1, 3).reshape(ns, LANES, kl)
    toep = jnp.stack([jnp.pad(strip[:, :, :kl - s * LANES], ((0, 0), (0, 0), (s * LANES, 0)))
                      for s in range(chunk)], axis=1).reshape(ns, kl, kl)

    rev_re, rev_im = lam_bar_pow(chunk - 1 - jnp.arange(chunk))
    rev_re, rev_im = slab_rows(rev_re)[:, :, None, :], slab_rows(rev_im)[:, :, None, :]
    bd_re, bd_im = b_diag(bb_re), b_diag(bb_im)
    w_in = jnp.concatenate([rev_re * bd_re - rev_im * bd_im,
                            rev_re * bd_im + rev_im * bd_re], axis=-1)
    w_in = w_in.reshape(ns, kl, 2 * SLAB_STATE)

    col_re = slab_rows(pw_re[:, 1:]).transpose(0, 2, 1)[:, :, :, None]
    col_im = slab_rows(pw_im[:, 1:]).transpose(0, 2, 1)[:, :, :, None]
    cd_re, cd_im = c_diag(c_re), c_diag(c_im)
    w_out = jnp.concatenate([col_re * cd_re - col_im * cd_im,
                             -(col_re * cd_im + col_im * cd_re)], axis=1)
    w_out = w_out.reshape(ns, 2 * SLAB_STATE, kl)

    lam_l = jnp.stack([pw_re[:, chunk].reshape(ns, SLAB_STATE),
                       pw_im[:, chunk].reshape(ns, SLAB_STATE)], axis=1)
    skip = jnp.tile(d_skip.astype(F32).reshape(ns, LANES), (1, chunk))[:, None]
    return toep.astype(BF16), w_in.astype(BF16), w_out.astype(BF16), lam_l, skip


def _s5_kernel(u_ref, toep_ref, win_ref, wout_ref, lam_ref, skip_ref, h0_ref, y_ref, hout_ref,
               a_scr, g_scr, hs_scr, y_scr, h_scr, *, batch, tk, chunk):
    n_chunks = tk // chunk
    ns = SLAB_STATE

    @pl.when(pl.program_id(1) == 0)
    def _():
        h_scr[...] = h0_ref[0]

    def gather(kk, carry):
        rows = pl.ds(pl.multiple_of(kk * batch, SUBLANES), batch)
        for t in range(chunk):
            a_scr[rows, t * LANES:(t + 1) * LANES] = u_ref[pl.ds(kk * chunk + t, batch, stride=tk), :]
        return carry

    lax.fori_loop(0, n_chunks, gather, 0)
    a = a_scr[...].astype(BF16)
    g_scr[...] = jnp.dot(a, win_ref[0], preferred_element_type=F32)
    ar = lam_ref[0, 0:1, :]
    ai = lam_ref[0, 1:2, :]

    def step(kk, carry):
        hr, hi = carry
        rows = pl.ds(pl.multiple_of(kk * batch, SUBLANES), batch)
        hs_scr[rows, 0:ns] = hr
        hs_scr[rows, ns:2 * ns] = hi
        return (ar * hr - ai * hi + g_scr[rows, 0:ns], ar * hi + ai * hr + g_scr[rows, ns:2 * ns])

    hr, hi = lax.fori_loop(0, n_chunks, step, (h_scr[:, 0:ns], h_scr[:, ns:2 * ns]))
    h_scr[:, 0:ns] = hr
    h_scr[:, ns:2 * ns] = hi
    hout_ref[0, :, 0:ns] = hr
    hout_ref[0, :, ns:2 * ns] = hi
    y = (jnp.dot(a, toep_ref[0], preferred_element_type=F32)
         + jnp.dot(hs_scr[...].astype(BF16), wout_ref[0], preferred_element_type=F32)
         + skip_ref[0] * a_scr[...])
    y_scr[...] = jax.nn.gelu(y)

    def scatter(kk, carry):
        rows = pl.ds(pl.multiple_of(kk * batch, SUBLANES), batch)
        for t in range(chunk):
            y_ref[pl.ds(kk * chunk + t, batch, stride=tk), :] = y_scr[rows, t * LANES:(t + 1) * LANES]
        return carry

    lax.fori_loop(0, n_chunks, scatter, 0)


def _s5_mixer(u, h0_re, h0_im, tables, batch, seq, tk, chunk):
    toep, w_in, w_out, lam_l, skip = tables
    kl = chunk * LANES
    n_blocks = seq // tk
    rows = (tk // chunk) * batch
    to_slabs = lambda h: h.reshape(batch, SSM_SLABS, SLAB_STATE).transpose(1, 0, 2)
    h0 = jnp.concatenate([to_slabs(h0_re), to_slabs(h0_im)], axis=-1)
    slab = lambda j, k: (j, 0, 0)
    y, h = pl.pallas_call(
        functools.partial(_s5_kernel, batch=batch, tk=tk, chunk=chunk),
        grid=(SSM_SLABS, n_blocks),
        in_specs=[pl.BlockSpec((batch * tk, LANES), lambda j, k: (k, j)),
                  pl.BlockSpec((1, kl, kl), slab),
                  pl.BlockSpec((1, kl, 2 * SLAB_STATE), slab),
                  pl.BlockSpec((1, 2 * SLAB_STATE, kl), slab),
                  pl.BlockSpec((1, 2, SLAB_STATE), slab),
                  pl.BlockSpec((1, 1, kl), slab),
                  pl.BlockSpec((1, batch, 2 * SLAB_STATE), slab)],
        out_specs=[pl.BlockSpec((batch * tk, LANES), lambda j, k: (k, j)),
                   pl.BlockSpec((1, batch, 2 * SLAB_STATE), slab)],
        out_shape=[jax.ShapeDtypeStruct(u.shape, F32),
                   jax.ShapeDtypeStruct((SSM_SLABS, batch, 2 * SLAB_STATE), F32)],
        scratch_shapes=[pltpu.VMEM((rows, kl), F32), pltpu.VMEM((rows, 2 * SLAB_STATE), F32),
                        pltpu.VMEM((rows, 2 * SLAB_STATE), F32), pltpu.VMEM((rows, kl), F32),
                        pltpu.VMEM((batch, 2 * SLAB_STATE), F32)],
        compiler_params=_params("parallel", "arbitrary"),
        name="s5_chunks",
    )(u, toep, w_in, w_out, lam_l, skip, h0)
    from_slabs = lambda x: x.transpose(1, 0, 2).reshape(batch, SSM_GROUPS, SSM_STATE)
    return y, from_slabs(h[:, :, :SLAB_STATE]), from_slabs(h[:, :, SLAB_STATE:])


def _head_rows(x, heads):
    t, w = x.shape
    row_head = lax.broadcasted_iota(jnp.int32, (heads * t, w), 0) // t
    lane_head = lax.broadcasted_iota(jnp.int32, (heads * t, w), 1) // HEAD_DIM
    mask = row_head == lane_head
    return jnp.where(mask, jnp.concatenate([x] * heads, axis=0), 0.0), mask


def _head_cols(x, mask, heads):
    t = x.shape[0] // heads
    x = jnp.where(mask, x, 0.0)
    out = x[0:t]
    for h in range(1, heads):
        out = out + x[h * t:(h + 1) * t]
    return out


def _mem_attn_kernel(q_ref, kv_ref, mg_ref, o_ref, *, block_batch, stack_heads):
    for bi in range(block_batch):
        q = q_ref[bi]
        if stack_heads:
            q_bd, mask = _head_rows(q, MEM_HEADS)
            kt = jnp.concatenate([kv_ref[0, bi, 0, h] for h in range(MEM_HEADS)], axis=0).astype(BF16)
            vt = jnp.concatenate([kv_ref[0, bi, 1, h] for h in range(MEM_HEADS)], axis=0).astype(BF16)
            s = jnp.dot(q_bd.astype(BF16), kt, preferred_element_type=F32) * SCALE
            p = jnp.exp(s - jnp.max(s, axis=-1, keepdims=True))
            l = jnp.sum(p, axis=-1, keepdims=True)
            o = lax.dot_general(p.astype(BF16), vt, NT_DIMS, preferred_element_type=F32) / l
            o_ref[bi] = _head_cols(o, mask, MEM_HEADS) * _silu(mg_ref[bi])
            continue
        outs = []
        for h in range(MEM_HEADS):
            qh = q[:, h * HEAD_DIM:(h + 1) * HEAD_DIM].astype(BF16)
            kt = kv_ref[0, bi, 0, h].astype(BF16)
            vt = kv_ref[0, bi, 1, h].astype(BF16)
            s = jnp.dot(qh, kt, preferred_element_type=F32) * SCALE
            p = jnp.exp(s - jnp.max(s, axis=-1, keepdims=True))
            l = jnp.sum(p, axis=-1, keepdims=True)
            outs.append(lax.dot_general(p.astype(BF16), vt, NT_DIMS, preferred_element_type=F32) / l)
        o_ref[bi] = jnp.concatenate(outs, axis=1) * _silu(mg_ref[bi])


def _mem_attn(q, kv_t, layer, mgate, block_batch, tq):
    b, t, w = q.shape
    tok = lambda i, j: (i, j, 0)
    return pl.pallas_call(
        functools.partial(_mem_attn_kernel, block_batch=block_batch, stack_heads=tq * MEM_HEADS <= LANES),
        grid=(b // block_batch, t // tq),
        in_specs=[pl.BlockSpec((block_batch, tq, w), tok),
                  pl.BlockSpec((1, block_batch) + kv_t.shape[2:], lambda i, j: (layer, i, 0, 0, 0, 0)),
                  pl.BlockSpec((block_batch, tq, w), tok)],
        out_specs=pl.BlockSpec((block_batch, tq, w), tok),
        out_shape=jax.ShapeDtypeStruct((b, t, w), F32),
        compiler_params=_params("parallel", "parallel"),
        name="mem_attn",
    )(q, kv_t, mgate)


def _mix_plain(tok_refs, fixed_refs):
    (y_ref,) = tok_refs
    return y_ref[...]


def _mix_glu(tok_refs, fixed_refs):
    y_ref, gate_ref = tok_refs
    w_ref, b_ref = fixed_refs
    y = y_ref[...]
    z = jnp.dot(y.astype(BF16), w_ref[...], preferred_element_type=F32) + b_ref[...]
    return y * jax.nn.sigmoid(z) * _silu(gate_ref[...])


def _mix_merge(tok_refs, fixed_refs):
    o0, o1, o2, l0, l1, l2, gate_ref = tok_refs
    ls = [l0[...], l1[...], l2[...]]
    mx = jnp.maximum(jnp.maximum(ls[0], ls[1]), ls[2])
    es = [jnp.exp(l - mx) for l in ls]
    den = es[0] + es[1] + es[2]
    o = jnp.concatenate([o_ref[...] * (e / den) for o_ref, e in zip((o0, o1, o2), es)], axis=1)
    return o * _silu(gate_ref[...])


def _layer_tail_kernel(*refs, n_tok, n_fixed, mix_fn):
    mix = mix_fn(refs[:n_tok], refs[n_tok:n_tok + n_fixed])
    m_ref, x_ref, wy_ref, wm_ref, g_ref, b_ref, o_ref = refs[n_tok + n_fixed:]
    out = (jnp.dot(mix.astype(BF16), wy_ref[...], preferred_element_type=F32)
           + jnp.dot(m_ref[...].astype(BF16), wm_ref[...], preferred_element_type=F32))
    z = DEEPNORM_ALPHA * x_ref[...] + out
    mu = jnp.mean(z, axis=-1, keepdims=True)
    zc = z - mu
    var = jnp.mean(zc * zc, axis=-1, keepdims=True)
    o_ref[...] = zc * lax.rsqrt(var + LN_EPS) * g_ref[...] + b_ref[...]


def _layer_tail(mix_fn, tok_arrays, fixed_arrays, m, x, w_out, ln_g, ln_b, tm, first_block=None):
    rows, d = x.shape
    tok = lambda i: (i, 0)
    fixed = lambda i: (0, 0)
    first = tok if first_block is None else (lambda i: (first_block(i), 0))
    tok_maps = [first] + [tok] * (len(tok_arrays) - 1)
    return pl.pallas_call(
        functools.partial(_layer_tail_kernel, n_tok=len(tok_arrays), n_fixed=len(fixed_arrays), mix_fn=mix_fn),
        grid=(rows // tm,),
        in_specs=[pl.BlockSpec((tm, a.shape[1]), tm_map) for a, tm_map in zip(tok_arrays, tok_maps)]
        + [pl.BlockSpec(a.shape, fixed) for a in fixed_arrays]
        + [pl.BlockSpec((tm, MEM_WIDTH), tok), pl.BlockSpec((tm, d), tok),
           pl.BlockSpec((MIX_WIDTH, d), fixed), pl.BlockSpec((MEM_WIDTH, d), fixed),
           pl.BlockSpec((1, d), fixed), pl.BlockSpec((1, d), fixed)],
        out_specs=pl.BlockSpec((tm, d), tok),
        out_shape=jax.ShapeDtypeStruct((rows, d), F32),
        compiler_params=_params(),
        name="layer_tail",
    )(*tok_arrays, *fixed_arrays, m, x, w_out[:MIX_WIDTH], w_out[MIX_WIDTH:], ln_g, ln_b)


def _softmax_parts(s):
    m = jnp.max(s, axis=-1, keepdims=True)
    p = jnp.exp(s - m)
    l = jnp.sum(p, axis=-1, keepdims=True)
    return p, l, m + jnp.log(l)


def _dil_prompt_kernel(q_ref, k_ref, v_ref, o_ref, lse_ref, *, seq, dil, group):
    n = WINDOW_KEYS
    n_blocks = seq // dil // n
    heads = LANES // HEAD_DIM
    first_head = group * HEADS_PER_GROUP
    second_pair = pl.program_id(1) == 1
    slopes = [jnp.where(second_pair, _slope(first_head + heads + h), _slope(first_head + h))
              for h in range(heads)]
    qi = lax.broadcasted_iota(jnp.int32, (n, 2 * n), 0)
    ku = lax.broadcasted_iota(jnp.int32, (n, 2 * n), 1)
    delta2 = qi + n - ku
    valid2 = jnp.where(delta2 >= 0, delta2, n + 1) <= n
    delta1 = delta2[:, n:]
    valid1 = delta1 >= 0
    for r in range(dil):
        for jb in range(n_blocks):
            start = r + jb * n * dil
            cur = pl.ds(start, n, stride=dil) if dil > 1 else pl.ds(start, n)
            q = q_ref[0, cur, :]
            k = k_ref[0, cur, :]
            v = v_ref[0, cur, :]
            if jb > 0:
                prev = pl.ds(start - n * dil, n, stride=dil) if dil > 1 else pl.ds(start - n * dil, n)
                k = jnp.concatenate([k_ref[0, prev, :], k], axis=0)
                v = jnp.concatenate([v_ref[0, prev, :], v], axis=0)
                delta, valid = delta2, valid2
            else:
                delta, valid = delta1, valid1
            dist = (delta * dil).astype(F32)
            outs, lses = [], []
            for h in range(heads):
                lo, hi = h * HEAD_DIM, (h + 1) * HEAD_DIM
                s = lax.dot_general(q[:, lo:hi].astype(BF16), k[:, lo:hi].astype(BF16),
                                    NT_DIMS, preferred_element_type=F32) * SCALE
                p, l, lse = _softmax_parts(jnp.where(valid, s - slopes[h] * dist, NEG_INF))
                outs.append(jnp.dot(p.astype(BF16), v[:, lo:hi].astype(BF16),
                                    preferred_element_type=F32) / l)
                lses.append(jnp.broadcast_to(lse, (n, HEAD_DIM)))
            o_ref[0, cur, :] = jnp.concatenate(outs, axis=1)
            lse_ref[0, cur, :] = jnp.concatenate(lses, axis=1)


def _dil_prompt(q, kv, group):
    b, seq, _ = q.shape
    _, dil = DIL_GROUPS[group]
    pairs = GROUP_WIDTH // LANES
    k_col = group * pairs
    v_col = MIX_WIDTH // LANES + k_col
    blk = (1, seq, LANES)
    return pl.pallas_call(
        functools.partial(_dil_prompt_kernel, seq=seq, dil=dil, group=group),
        grid=(b, pairs),
        in_specs=[pl.BlockSpec(blk, lambda i, j: (i, 0, k_col + j)),
                  pl.BlockSpec(blk, lambda i, j: (i, 0, k_col + j)),
                  pl.BlockSpec(blk, lambda i, j: (i, 0, v_col + j))],
        out_specs=[pl.BlockSpec(blk, lambda i, j: (i, 0, j)), pl.BlockSpec(blk, lambda i, j: (i, 0, j))],
        out_shape=[jax.ShapeDtypeStruct((b, seq, GROUP_WIDTH), F32)] * 2,
        compiler_params=_params("parallel", "parallel"),
        name=f"dil_prompt_g{group}",
    )(q, kv, kv)


def _dil_sample_kernel(q_ref, kvn_ref, gate_ref, c0_ref, c1_ref, c2_ref,
                       y_ref, n0_ref, n1_ref, n2_ref, *, steps):
    rows = q_ref.shape[1]
    lead = rows - steps
    q_all = q_ref[0]
    kvn = kvn_ref[0]
    hg = HEADS_PER_GROUP
    stacked = lax.broadcasted_iota(jnp.int32, (hg * rows, 1), 0)
    row_head = stacked // rows
    tok = stacked - row_head * rows - lead
    kvn_t = jnp.concatenate([jnp.zeros((LANES - rows, 2 * MIX_WIDTH), F32), kvn], axis=0).T
    tail_lane = lax.broadcasted_iota(jnp.int32, (HEAD_DIM, LANES), 1) >= LANES - steps
    outs, lses = [], []
    for g, ((win, dil), c_ref, n_ref) in enumerate(zip(DIL_GROUPS, (c0_ref, c1_ref, c2_ref),
                                                       (n0_ref, n1_ref, n2_ref))):
        cols = slice(g * GROUP_WIDTH, (g + 1) * GROUP_WIDTH)
        k_new = kvn[:, cols]
        v_new = kvn[:, MIX_WIDTH + g * GROUP_WIDTH:MIX_WIDTH + (g + 1) * GROUP_WIDTH]
        kv_heads = [[c_ref[0, kv_i, h] for h in range(hg)] for kv_i in range(2)]
        for kv_i in range(2):
            for h in range(hg):
                shifted = pltpu.roll(kv_heads[kv_i][h], win - steps, axis=1)
                lo = kv_i * MIX_WIDTH + g * GROUP_WIDTH + h * HEAD_DIM
                new_t = kvn_t[lo:lo + HEAD_DIM, :]
                if win > LANES:
                    n_ref[0, kv_i, h, :, 0:win - LANES] = shifted[:, 0:win - LANES]
                n_ref[0, kv_i, h, :, win - LANES:win] = jnp.where(tail_lane, new_t, shifted[:, win - LANES:win])
        kt = jnp.concatenate(kv_heads[0], axis=0).astype(BF16)
        vt = jnp.concatenate(kv_heads[1], axis=0).astype(BF16)
        q_bd, head_mask = _head_rows(q_all[:, cols], hg)
        slope = jnp.zeros((hg * rows, 1), F32)
        for h in range(hg):
            slope = jnp.where(row_head == h, _slope(g * hg + h), slope)
        pos = lax.broadcasted_iota(jnp.int32, (hg * rows, win), 1)
        dist_buf = ((win + tok) - pos).astype(F32)
        valid_buf = jnp.where(pos >= tok, (pos - tok) & (dil - 1), 1) == 0
        s_buf = jnp.dot(q_bd.astype(BF16), kt, preferred_element_type=F32) * SCALE - slope * dist_buf
        s_buf = jnp.where(valid_buf, s_buf, NEG_INF)
        s_new = []
        for t2 in range(steps):
            col = jnp.sum(q_bd * k_new[lead + t2:lead + t2 + 1, :], axis=-1, keepdims=True) * SCALE \
                - slope * (tok - t2).astype(F32)
            ok = jnp.where(tok >= t2, (tok - t2) & (dil - 1), 1) == 0
            s_new.append(jnp.where(ok, col, NEG_INF))
        m = jnp.max(s_buf, axis=-1, keepdims=True)
        for col in s_new:
            m = jnp.maximum(m, col)
        p_buf = jnp.exp(s_buf - m)
        l = jnp.sum(p_buf, axis=-1, keepdims=True)
        o = lax.dot_general(p_buf.astype(BF16), vt, NT_DIMS, preferred_element_type=F32)
        for t2, col in enumerate(s_new):
            p_col = jnp.exp(col - m)
            l = l + p_col
            o = o + p_col * v_new[lead + t2:lead + t2 + 1, :]
        outs.append(o / l)
        lses.append(m + jnp.log(l))
    mx = jnp.maximum(jnp.maximum(lses[0], lses[1]), lses[2])
    es = [jnp.exp(l - mx) for l in lses]
    den = es[0] + es[1] + es[2]
    mixed = [_head_cols(outs[g] * (es[g] / den), head_mask, hg) for g in range(len(DIL_GROUPS))]
    y_ref[0] = jnp.concatenate(mixed, axis=1) * _silu(gate_ref[0])


def _dil_sample(q, kv_new, gate, caches_t):
    b, steps, _ = q.shape
    rows = _round_up(steps, SUBLANES)
    pad = ((0, 0), (rows - steps, 0), (0, 0))
    q, kv_new, gate = jnp.pad(q, pad), jnp.pad(kv_new, pad), jnp.pad(gate, pad)
    tok = lambda i: (i, 0, 0)
    cache_specs = [pl.BlockSpec((1,) + c.shape[1:], lambda i: (i, 0, 0, 0, 0)) for c in caches_t]
    res = pl.pallas_call(
        functools.partial(_dil_sample_kernel, steps=steps),
        grid=(b,),
        in_specs=[pl.BlockSpec((1, rows, MIX_WIDTH), tok), pl.BlockSpec((1, rows, 2 * MIX_WIDTH), tok),
                  pl.BlockSpec((1, rows, MIX_WIDTH), tok)] + cache_specs,
        out_specs=[pl.BlockSpec((1, rows, MIX_WIDTH), tok)] + cache_specs,
        out_shape=[jax.ShapeDtypeStruct((b, rows, MIX_WIDTH), F32)]
        + [jax.ShapeDtypeStruct(c.shape, F32) for c in caches_t],
        compiler_params=_params(),
        name="dil_sample",
    )(q, kv_new, gate, *caches_t)
    return [res[0][:, rows - steps:]] + list(res[1:])


def _trunk(x, batch, seq, mem_kv_t, h0_re, h0_im, weights, tm, s5_tk, s5_chunk, mem_block, mem_tq,
           kv_proj, dil_attn):
    (w_in, w_out, ln_g, ln_b, tables, w_glu, b_glu) = weights
    splits = (MIX_WIDTH, MIX_WIDTH, MEM_WIDTH, MEM_WIDTH)
    seq_mem = _round_up(seq, SUBLANES)

    def mem_branch(mq, mgate, layer):
        mq = mq.reshape(batch, seq, MEM_WIDTH)
        mgate = mgate.reshape(batch, seq, MEM_WIDTH)
        if seq_mem != seq:
            pad = ((0, 0), (0, seq_mem - seq), (0, 0))
            mq, mgate = jnp.pad(mq, pad), jnp.pad(mgate, pad)
        m = _mem_attn(mq, mem_kv_t, layer, mgate, mem_block, min(mem_tq, seq_mem))
        return m[:, :seq].reshape(batch * seq, MEM_WIDTH)

    if s5_tk == seq:
        block_of = None
    else:
        assert tm == s5_tk
        n_blocks = seq // s5_tk
        block_of = lambda i: (i % n_blocks) * batch + i // n_blocks
    u, gate, mq, mgate = _proj(x, w_in[0], splits, tm, first_out_block=block_of)
    y, h_re, h_im = _s5_mixer(u, h0_re, h0_im, tables, batch, seq, s5_tk, s5_chunk)
    m = mem_branch(mq, mgate, 0)
    x = _layer_tail(_mix_glu, (y, gate), (w_glu, b_glu), m, x, w_out[0], ln_g[0:1], ln_b[0:1], tm,
                    first_block=block_of)
    kv, kv_extras = kv_proj(x)

    u, gate, mq, mgate = _proj(x, w_in[1], splits, tm)
    mix_fn, mix_inputs, attn_extras = dil_attn(u, kv, gate)
    m = mem_branch(mq, mgate, 1)
    x = _layer_tail(mix_fn, mix_inputs, (), m, x, w_out[1], ln_g[1:2], ln_b[1:2], tm)
    return x, h_re, h_im, kv, kv_extras, attn_extras


def kernel(x_prompt, x_sample, cache_mem_kv, state_ssm_re, state_ssm_im, cache_dil1_kv, cache_dil4_kv,
           cache_dil16_kv, mem_prompt, w_in, w_out, ln_g, ln_b, w_mem_kv, ssm_lambda_re, ssm_lambda_im,
           ssm_log_dt, ssm_b_re, ssm_b_im, ssm_c_re, ssm_c_im, ssm_d, w_glu, b_glu, w_kv_shared):
    bp, seq, d = x_prompt.shape
    bs, steps, _ = x_sample.shape
    prompt_chunk, prompt_tk = 8, 512
    n_groups = len(DIL_GROUPS)

    w_in_b = w_in.astype(BF16)
    w_out_b = w_out.astype(BF16)
    w_glu_b = w_glu[0].astype(BF16)
    w_kv_b = w_kv_shared.astype(BF16)
    ssm = (ssm_lambda_re[0], ssm_lambda_im[0], ssm_log_dt[0], ssm_b_re[0], ssm_b_im[0],
           ssm_c_re[0], ssm_c_im[0], ssm_d[0])

    def weights(chunk):
        return (w_in_b, w_out_b, ln_g, ln_b, _s5_tables(chunk, *ssm), w_glu_b, b_glu)

    mem_kv_p = _mem_kv_t(w_mem_kv.transpose(0, 2, 1).astype(BF16), mem_prompt)
    mem_kv_p = mem_kv_p.reshape(DEPTH, bp, 2, MEM_HEADS, HEAD_DIM, N_MEM)
    zeros = jnp.zeros((bp, SSM_GROUPS, SSM_STATE), F32)
    wide = n_groups - 1
    assert DIL_GROUPS[wide][0] >= seq and all(w <= prompt_tk for w, _ in DIL_GROUPS[:wide])

    def group_cols_t(g):
        return jnp.concatenate([w_kv_b[:, kv_i * MIX_WIDTH + g * GROUP_WIDTH:
                                       kv_i * MIX_WIDTH + (g + 1) * GROUP_WIDTH] for kv_i in range(2)], axis=1).T

    wt_all = group_cols_t(wide)
    wt_tail = jnp.concatenate([group_cols_t(g) for g in range(wide)], axis=0)

    def kv_prompt(x):
        kv, kvt_all, kvt_tail = _kv_proj(x, w_kv_b, wt_all, wt_tail, bp, seq, prompt_tk)
        return kv, (kvt_all, kvt_tail)

    def attn_prompt(u, kv, gate):
        q3 = u.reshape(bp, seq, MIX_WIDTH)
        kv3 = kv.reshape(bp, seq, 2 * MIX_WIDTH)
        res = [_dil_prompt(q3, kv3, g) for g in range(n_groups)]
        outs = [r[0].reshape(bp * seq, GROUP_WIDTH) for r in res]
        lses = [r[1].reshape(bp * seq, GROUP_WIDTH) for r in res]
        return _mix_merge, (*outs, *lses, gate), None

    y_p, hre_p, him_p, _, (kvt_all, kvt_tail), _ = _trunk(
        x_prompt.reshape(bp * seq, d), bp, seq, mem_kv_p, zeros, zeros, weights(prompt_chunk),
        prompt_tk, prompt_tk, prompt_chunk, 1, 512, kv_prompt, attn_prompt)
    kvt_tail = kvt_tail.reshape(bp, wide, 2, HEADS_PER_GROUP, HEAD_DIM, prompt_tk)
    win_p = [kvt_tail[:, g, :, :, :, prompt_tk - win:].transpose(0, 4, 1, 2, 3)
             for g, (win, _) in enumerate(DIL_GROUPS[:wide])]
    win_p.append(kvt_all.reshape(bp, 2, HEADS_PER_GROUP, HEAD_DIM, seq).transpose(0, 4, 1, 2, 3))

    caches = (cache_dil1_kv, cache_dil4_kv, cache_dil16_kv)
    caches_t = [c.transpose(0, 2, 3, 4, 1) for c in caches]
    mem_kv_s = cache_mem_kv.transpose(0, 1, 3, 4, 5, 2)

    def kv_sample(x):
        return _proj(x, w_kv_b, (2 * MIX_WIDTH,), 256)[0], None

    def attn_sample(u, kv, gate):
        res = _dil_sample(u.reshape(bs, steps, MIX_WIDTH), kv.reshape(bs, steps, 2 * MIX_WIDTH),
                          gate.reshape(bs, steps, MIX_WIDTH), caches_t)
        return _mix_plain, (res[0].reshape(bs * steps, MIX_WIDTH),), res[1:]

    y_s, hre_s, him_s, _, _, rolled = _trunk(
        x_sample.reshape(bs * steps, d), bs, steps, mem_kv_s, state_ssm_re[0], state_ssm_im[0],
        weights(steps), 256, steps, steps, 8, 8, kv_sample, attn_sample)
    win_s = [c.transpose(0, 4, 1, 2, 3) for c in rolled]

    mem_kv_out = mem_kv_p.transpose(0, 1, 5, 2, 3, 4)
    return (y_p.reshape(bp, seq, d), y_s.reshape(bs, steps, d), mem_kv_out,
            hre_p[None], him_p[None], win_p[0], win_p[1], win_p[2],
            hre_s[None], him_s[None], win_s[0], win_s[1], win_s[2])
```

```python
import functools

import jax
import jax.numpy as jnp
from jax import lax
from jax.experimental import pallas as pl
from jax.experimental.pallas import tpu as pltpu

F32 = jnp.float32
BF16 = jnp.bfloat16

D_MODEL = 1024
HEAD_DIM = 64
MIX_WIDTH = 768
MEM_WIDTH = 256
MEM_HEADS = 4
N_MEM = 256
SSM_GROUP = 16
SSM_GROUPS = 48
SSM_STATE = 64
DIL_GROUPS = ((128, 1), (512, 4), (2048, 16))
ATT_HEADS = 12
HEADS_PER_GROUP = 4
GROUP_WIDTH = HEADS_PER_GROUP * HEAD_DIM
DEPTH = 2
DEEPNORM_ALPHA = (2.0 * DEPTH) ** 0.25
LN_EPS = 1e-5
SCALE = HEAD_DIM ** -0.5
NEG_INF = -1e30
WINDOW_KEYS = 128
LANES = 128
SUBLANES = 8
SLAB_GROUPS = LANES // SSM_GROUP
SSM_SLABS = SSM_GROUPS // SLAB_GROUPS
SLAB_STATE = SLAB_GROUPS * SSM_STATE
VMEM_LIMIT = 56 * 1024 * 1024
NT_DIMS = (((1,), (1,)), ((), ()))


def _slope(head):
    return 2.0 ** (-8.0 * (head + 1) / ATT_HEADS)


def _params(*semantics):
    return pltpu.CompilerParams(dimension_semantics=semantics or ("parallel",),
                                vmem_limit_bytes=VMEM_LIMIT)


def _silu(x):
    return x * jax.nn.sigmoid(x)


def _round_up(n, m):
    return -(-n // m) * m


def _proj_kernel(x_ref, w_ref, *out_refs, splits):
    x = x_ref[...].astype(BF16)
    off = 0
    for o_ref, n in zip(out_refs, splits):
        o_ref[...] = jnp.dot(x, w_ref[:, off:off + n], preferred_element_type=F32).astype(o_ref.dtype)
        off += n


def _proj(x, w, splits, tm, first_out_block=None, rest_dtype=F32):
    m, k = x.shape
    n_total = w.shape[1]
    assert sum(splits) == n_total and m % tm == 0
    tok = lambda i: (i, 0)
    first = tok if first_out_block is None else (lambda i: (first_out_block(i), 0))
    out_maps = [first] + [tok] * (len(splits) - 1)
    return pl.pallas_call(
        functools.partial(_proj_kernel, splits=splits),
        grid=(m // tm,),
        in_specs=[pl.BlockSpec((tm, k), tok), pl.BlockSpec((k, n_total), lambda i: (0, 0))],
        out_specs=[pl.BlockSpec((tm, n), om) for n, om in zip(splits, out_maps)],
        out_shape=[jax.ShapeDtypeStruct((m, n), F32 if i == 0 else rest_dtype) for i, n in enumerate(splits)],
        compiler_params=_params(),
        name="proj",
    )(x, w)


def _kv_proj_kernel(x_ref, w_ref, wt_all_ref, wt_tail_ref, kv_ref, kvt_all_ref, kvt_tail_ref, *, tiles):
    x = x_ref[...].astype(BF16)
    kv_ref[...] = jnp.dot(x, w_ref[...], preferred_element_type=F32)
    kvt_all_ref[0] = lax.dot_general(wt_all_ref[...], x, NT_DIMS, preferred_element_type=F32)

    @pl.when(pl.program_id(0) % tiles == tiles - 1)
    def _():
        kvt_tail_ref[0] = lax.dot_general(wt_tail_ref[...], x, NT_DIMS, preferred_element_type=F32)


def _kv_proj(x, w, wt_all, wt_tail, batch, seq, tm):
    m, k = x.shape
    n_total, n_all, n_tail = w.shape[1], wt_all.shape[0], wt_tail.shape[0]
    tiles = seq // tm
    fixed = lambda i: (0, 0)
    return pl.pallas_call(
        functools.partial(_kv_proj_kernel, tiles=tiles),
        grid=(m // tm,),
        in_specs=[pl.BlockSpec((tm, k), lambda i: (i, 0)),
                  pl.BlockSpec((k, n_total), fixed),
                  pl.BlockSpec((n_all, k), fixed),
                  pl.BlockSpec((n_tail, k), fixed)],
        out_specs=[pl.BlockSpec((tm, n_total), lambda i: (i, 0)),
                   pl.BlockSpec((1, n_all, tm), lambda i: (i // tiles, 0, i % tiles)),
                   pl.BlockSpec((1, n_tail, tm), lambda i: (i // tiles, 0, 0))],
        out_shape=[jax.ShapeDtypeStruct((m, n_total), F32),
                   jax.ShapeDtypeStruct((batch, n_all, seq), F32),
                   jax.ShapeDtypeStruct((batch, n_tail, tm), F32)],
        compiler_params=_params("arbitrary"),
        name="kv_proj",
    )(x, w, wt_all, wt_tail)


def _mem_kv_t_kernel(w_ref, mem_ref, o_ref):
    o_ref[0, 0] = lax.dot_general(w_ref[0], mem_ref[0].astype(BF16), NT_DIMS, preferred_element_type=F32)


def _mem_kv_t(w_t, mem):
    depth, n, d = w_t.shape
    b = mem.shape[0]
    return pl.pallas_call(
        _mem_kv_t_kernel,
        grid=(depth, b),
        in_specs=[pl.BlockSpec((1, n, d), lambda l, i: (l, 0, 0)),
                  pl.BlockSpec((1, N_MEM, d), lambda l, i: (i, 0, 0))],
        out_specs=pl.BlockSpec((1, 1, n, N_MEM), lambda l, i: (l, i, 0, 0)),
        out_shape=jax.ShapeDtypeStruct((depth, b, n, N_MEM), F32),
        compiler_params=_params("parallel", "parallel"),
        name="mem_kv_t",
    )(w_t, mem)


def _s5_tables(chunk, lam_re, lam_im, log_dt, b_re, b_im, c_re, c_im, d_skip):
    hp = lax.Precision.HIGHEST
    p, c = SSM_STATE, SSM_GROUP
    ns, gs = SSM_SLABS, SLAB_GROUPS
    kl = chunk * LANES
    lr = jnp.minimum(lam_re.astype(F32), -1e-4)
    li = lam_im.astype(F32)
    dt = jnp.exp(log_dt.astype(F32))[:, None]
    def lam_bar_pow(n):
        n = n.astype(F32)[None, :, None]
        mag = jnp.exp(lr[:, None, :] * dt[:, None, :] * n)
        ang = li[:, None, :] * dt[:, None, :] * n
        return mag * jnp.cos(ang), mag * jnp.sin(ang)

    pw_re, pw_im = lam_bar_pow(jnp.arange(chunk + 1))
    nr, ni = pw_re[:, 1] - 1.0, pw_im[:, 1]
    den = lr * lr + li * li
    f_re, f_im = (nr * lr + ni * li) / den, (ni * lr - nr * li) / den
    bb_re = f_re[:, :, None] * b_re - f_im[:, :, None] * b_im
    bb_im = f_re[:, :, None] * b_im + f_im[:, :, None] * b_re
    c_re, c_im = c_re.astype(F32), c_im.astype(F32)

    x_re = pw_re[:, :chunk, :, None] * bb_re[:, None] - pw_im[:, :chunk, :, None] * bb_im[:, None]
    x_im = pw_re[:, :chunk, :, None] * bb_im[:, None] + pw_im[:, :chunk, :, None] * bb_re[:, None]
    conv = (jnp.einsum('gcp,gtpd->gtcd', c_re, x_re, precision=hp)
            - jnp.einsum('gcp,gtpd->gtcd', c_im, x_im, precision=hp))
    def slab_diag(x):
        r, w = x.shape[-2:]
        x = jnp.tile(x, (1,) * (x.ndim - 1) + (gs,))
        own = (jnp.arange(gs * w) // w)[None, :] == jnp.arange(gs)[:, None]
        x = jnp.where(own.reshape((1, gs) + (1,) * (x.ndim - 3) + (gs * w,)), x, 0.0)
        x = jnp.moveaxis(x, 1, -3)
        return x.reshape(x.shape[:-3] + (gs * r, gs * w))

    conv_d = slab_diag(conv.transpose(0, 1, 3, 2).reshape(ns, gs, chunk, c, c))
    b_diag = lambda x: slab_diag(x.transpose(0, 2, 1).reshape(ns, gs, c, p))
    c_diag = lambda x: slab_diag(x.transpose(0, 2, 1).reshape(ns, gs, p, c))

    def slab_rows(x):
        n = x.shape[1]
        return x.reshape(ns, gs, n, p).transpose(0, 2, 1, 3).reshape(ns, n, SLAB_STATE)

    strip = conv_d.transpose(0, 2, 1, 3).reshape(ns, LANES, kl)
    toep = jnp.stack([jnp.pad(strip[:, :, :kl - s * LANES], ((0, 0), (0, 0), (s * LANES, 0)))
                      for s in range(chunk)], axis=1).reshape(ns, kl, kl)

    rev_re, rev_im = lam_bar_pow(chunk - 1 - jnp.arange(chunk))
    rev_re, rev_im = slab_rows(rev_re), slab_rows(rev_im)
    bd_re, bd_im = b_diag(bb_re), b_diag(bb_im)
    rev_a = jnp.concatenate([rev_re, rev_re], axis=-1)[:, :, None, :]
    rev_b = jnp.concatenate([-rev_im, rev_im], axis=-1)[:, :, None, :]
    bd_a = jnp.concatenate([bd_re, bd_im], axis=-1)[:, None]
    bd_b = jnp.concatenate([bd_im, bd_re], axis=-1)[:, None]
    w_in = (rev_a * bd_a + rev_b * bd_b).reshape(ns, kl, 2 * SLAB_STATE)

    col_re = slab_rows(pw_re[:, 1:]).transpose(0, 2, 1)
    col_im = slab_rows(pw_im[:, 1:]).transpose(0, 2, 1)
    cd_re, cd_im = c_diag(c_re), c_diag(c_im)
    col_a = jnp.repeat(jnp.concatenate([col_re, -col_re], axis=1), LANES, axis=-1)
    col_b = jnp.repeat(jnp.concatenate([-col_im, -col_im], axis=1), LANES, axis=-1)
    cd_a = jnp.tile(jnp.concatenate([cd_re, cd_im], axis=1), (1, 1, chunk))
    cd_b = jnp.tile(jnp.concatenate([cd_im, cd_re], axis=1), (1, 1, chunk))
    w_out = col_a * cd_a + col_b * cd_b

    lam_l = jnp.stack([pw_re[:, chunk].reshape(ns, SLAB_STATE),
                       pw_im[:, chunk].reshape(ns, SLAB_STATE)], axis=1)
    skip = jnp.tile(d_skip.astype(F32).reshape(ns, LANES), (1, chunk))[:, None]
    return toep.astype(BF16), w_in.astype(BF16), w_out.astype(BF16), lam_l, skip


def _s5_kernel(u_ref, toep_ref, win_ref, wout_ref, lam_ref, skip_ref, h0_ref, y_ref, hout_ref,
               a_scr, g_scr, hs_scr, y_scr, h_scr, *, batch, tk, chunk):
    n_chunks = tk // chunk
    ns = SLAB_STATE

    @pl.when(pl.program_id(1) == 0)
    def _():
        h_scr[...] = h0_ref[0]

    def gather(kk, carry):
        rows = pl.ds(pl.multiple_of(kk * batch, SUBLANES), batch)
        for t in range(chunk):
            a_scr[rows, t * LANES:(t + 1) * LANES] = u_ref[pl.ds(kk * chunk + t, batch, stride=tk), :]
        return carry

    lax.fori_loop(0, n_chunks, gather, 0)
    a = a_scr[...].astype(BF16)
    g_scr[...] = jnp.dot(a, win_ref[0], preferred_element_type=F32)
    ar = lam_ref[0, 0:1, :]
    ai = lam_ref[0, 1:2, :]

    def step(kk, carry):
        hr, hi = carry
        rows = pl.ds(pl.multiple_of(kk * batch, SUBLANES), batch)
        hs_scr[rows, 0:ns] = hr
        hs_scr[rows, ns:2 * ns] = hi
        return (ar * hr - ai * hi + g_scr[rows, 0:ns], ar * hi + ai * hr + g_scr[rows, ns:2 * ns])

    hr, hi = lax.fori_loop(0, n_chunks, step, (h_scr[:, 0:ns], h_scr[:, ns:2 * ns]))
    h_scr[:, 0:ns] = hr
    h_scr[:, ns:2 * ns] = hi
    hout_ref[0, :, 0:ns] = hr
    hout_ref[0, :, ns:2 * ns] = hi
    y = (jnp.dot(a, toep_ref[0], preferred_element_type=F32)
         + jnp.dot(hs_scr[...].astype(BF16), wout_ref[0], preferred_element_type=F32)
         + skip_ref[0] * a_scr[...])
    y_scr[...] = jax.nn.gelu(y)

    def scatter(kk, carry):
        rows = pl.ds(pl.multiple_of(kk * batch, SUBLANES), batch)
        for t in range(chunk):
            y_ref[pl.ds(kk * chunk + t, batch, stride=tk), :] = y_scr[rows, t * LANES:(t + 1) * LANES]
        return carry

    lax.fori_loop(0, n_chunks, scatter, 0)


def _s5_mixer(u, h0_re, h0_im, tables, batch, seq, tk, chunk):
    toep, w_in, w_out, lam_l, skip = tables
    kl = chunk * LANES
    n_blocks = seq // tk
    rows = (tk // chunk) * batch
    to_slabs = lambda h: h.reshape(batch, SSM_SLABS, SLAB_STATE).transpose(1, 0, 2)
    h0 = jnp.concatenate([to_slabs(h0_re), to_slabs(h0_im)], axis=-1)
    slab = lambda j, k: (j, 0, 0)
    y, h = pl.pallas_call(
        functools.partial(_s5_kernel, batch=batch, tk=tk, chunk=chunk),
        grid=(SSM_SLABS, n_blocks),
        in_specs=[pl.BlockSpec((batch * tk, LANES), lambda j, k: (k, j)),
                  pl.BlockSpec((1, kl, kl), slab),
                  pl.BlockSpec((1, kl, 2 * SLAB_STATE), slab),
                  pl.BlockSpec((1, 2 * SLAB_STATE, kl), slab),
                  pl.BlockSpec((1, 2, SLAB_STATE), slab),
                  pl.BlockSpec((1, 1, kl), slab),
                  pl.BlockSpec((1, batch, 2 * SLAB_STATE), slab)],
        out_specs=[pl.BlockSpec((batch * tk, LANES), lambda j, k: (k, j)),
                   pl.BlockSpec((1, batch, 2 * SLAB_STATE), slab)],
        out_shape=[jax.ShapeDtypeStruct(u.shape, F32),
                   jax.ShapeDtypeStruct((SSM_SLABS, batch, 2 * SLAB_STATE), F32)],
        scratch_shapes=[pltpu.VMEM((rows, kl), F32), pltpu.VMEM((rows, 2 * SLAB_STATE), F32),
                        pltpu.VMEM((rows, 2 * SLAB_STATE), F32), pltpu.VMEM((rows, kl), F32),
                        pltpu.VMEM((batch, 2 * SLAB_STATE), F32)],
        compiler_params=_params("parallel", "arbitrary"),
        name="s5_chunks",
    )(u, toep, w_in, w_out, lam_l, skip, h0)
    from_slabs = lambda x: x.transpose(1, 0, 2).reshape(batch, SSM_GROUPS, SSM_STATE)
    return y, from_slabs(h[:, :, :SLAB_STATE]), from_slabs(h[:, :, SLAB_STATE:])


def _head_rows(x, heads):
    t, w = x.shape
    row_head = lax.broadcasted_iota(jnp.int32, (heads * t, w), 0) // t
    lane_head = lax.broadcasted_iota(jnp.int32, (heads * t, w), 1) // HEAD_DIM
    mask = row_head == lane_head
    return jnp.where(mask, jnp.concatenate([x] * heads, axis=0), 0.0), mask


def _head_cols(x, mask, heads):
    t = x.shape[0] // heads
    x = jnp.where(mask, x, 0.0)
    out = x[0:t]
    for h in range(1, heads):
        out = out + x[h * t:(h + 1) * t]
    return out


def _mem_attn_kernel(q_ref, kv_ref, mg_ref, o_ref, *, block_batch):
    for bi in range(block_batch):
        q_bd, mask = _head_rows(q_ref[bi].astype(F32), MEM_HEADS)
        kt = jnp.concatenate([kv_ref[0, bi, 0, h] for h in range(MEM_HEADS)], axis=0).astype(BF16)
        vt = jnp.concatenate([kv_ref[0, bi, 1, h] for h in range(MEM_HEADS)], axis=0).astype(BF16)
        s = jnp.dot(q_bd.astype(BF16), kt, preferred_element_type=F32) * SCALE
        p = jnp.exp(s - jnp.max(s, axis=-1, keepdims=True))
        l = jnp.sum(p, axis=-1, keepdims=True)
        o = lax.dot_general(p.astype(BF16), vt, NT_DIMS, preferred_element_type=F32) / l
        o_ref[bi] = (_head_cols(o, mask, MEM_HEADS) * _silu(mg_ref[bi].astype(F32))).astype(o_ref.dtype)


def _mem_attn(q, kv_t, layer, mgate, block_batch, tq):
    b, t, w = q.shape
    tok = lambda i, j: (i, j, 0)
    return pl.pallas_call(
        functools.partial(_mem_attn_kernel, block_batch=block_batch),
        grid=(b // block_batch, t // tq),
        in_specs=[pl.BlockSpec((block_batch, tq, w), tok),
                  pl.BlockSpec((1, block_batch) + kv_t.shape[2:], lambda i, j: (layer, i, 0, 0, 0, 0)),
                  pl.BlockSpec((block_batch, tq, w), tok)],
        out_specs=pl.BlockSpec((block_batch, tq, w), tok),
        out_shape=jax.ShapeDtypeStruct((b, t, w), q.dtype),
        compiler_params=_params("parallel", "parallel"),
        name="mem_attn",
    )(q, kv_t, mgate)


def _mix_plain(tok_refs, fixed_refs):
    (y_ref,) = tok_refs
    return y_ref[...]


def _mix_glu(tok_refs, fixed_refs):
    y_ref, gate_ref = tok_refs
    w_ref, b_ref = fixed_refs
    y = y_ref[...]
    z = jnp.dot(y.astype(BF16), w_ref[...], preferred_element_type=F32) + b_ref[...]
    return y * jax.nn.sigmoid(z) * _silu(gate_ref[...].astype(F32))


def _mix_merge(tok_refs, fixed_refs):
    o0, o1, o2, l0, l1, l2, gate_ref = tok_refs
    ls = [l0[...], l1[...], l2[...]]
    mx = jnp.maximum(jnp.maximum(ls[0], ls[1]), ls[2])
    es = [jnp.exp(l - mx) for l in ls]
    den = es[0] + es[1] + es[2]
    o = jnp.concatenate([o_ref[...] * (e / den) for o_ref, e in zip((o0, o1, o2), es)], axis=1)
    return o * _silu(gate_ref[...].astype(F32))


def _layer_tail_kernel(*refs, n_tok, n_fixed, mix_fn):
    mix = mix_fn(refs[:n_tok], refs[n_tok:n_tok + n_fixed])
    m_ref, x_ref, wy_ref, wm_ref, g_ref, b_ref, o_ref = refs[n_tok + n_fixed:]
    out = (jnp.dot(mix.astype(BF16), wy_ref[...], preferred_element_type=F32)
           + jnp.dot(m_ref[...].astype(BF16), wm_ref[...], preferred_element_type=F32))
    z = DEEPNORM_ALPHA * x_ref[...] + out
    mu = jnp.mean(z, axis=-1, keepdims=True)
    zc = z - mu
    var = jnp.mean(zc * zc, axis=-1, keepdims=True)
    o_ref[...] = zc * lax.rsqrt(var + LN_EPS) * g_ref[...] + b_ref[...]


def _layer_tail(mix_fn, tok_arrays, fixed_arrays, m, x, w_out, ln_g, ln_b, tm, first_block=None):
    rows, d = x.shape
    tok = lambda i: (i, 0)
    fixed = lambda i: (0, 0)
    first = tok if first_block is None else (lambda i: (first_block(i), 0))
    tok_maps = [first] + [tok] * (len(tok_arrays) - 1)
    return pl.pallas_call(
        functools.partial(_layer_tail_kernel, n_tok=len(tok_arrays), n_fixed=len(fixed_arrays), mix_fn=mix_fn),
        grid=(rows // tm,),
        in_specs=[pl.BlockSpec((tm, a.shape[1]), tm_map) for a, tm_map in zip(tok_arrays, tok_maps)]
        + [pl.BlockSpec(a.shape, fixed) for a in fixed_arrays]
        + [pl.BlockSpec((tm, MEM_WIDTH), tok), pl.BlockSpec((tm, d), tok),
           pl.BlockSpec((MIX_WIDTH, d), fixed), pl.BlockSpec((MEM_WIDTH, d), fixed),
           pl.BlockSpec((1, d), fixed), pl.BlockSpec((1, d), fixed)],
        out_specs=pl.BlockSpec((tm, d), tok),
        out_shape=jax.ShapeDtypeStruct((rows, d), F32),
        compiler_params=_params(),
        name="layer_tail",
    )(*tok_arrays, *fixed_arrays, m, x, w_out[:MIX_WIDTH], w_out[MIX_WIDTH:], ln_g, ln_b)


def _softmax_parts(s):
    m = jnp.max(s, axis=-1, keepdims=True)
    p = jnp.exp(s - m)
    l = jnp.sum(p, axis=-1, keepdims=True)
    return p, l, m + jnp.log(l)


def _dil_prompt_kernel(q_ref, k_ref, v_ref, o_ref, lse_ref, *, seq, dil, group):
    n = WINDOW_KEYS
    n_blocks = seq // dil // n
    heads = LANES // HEAD_DIM
    first_head = group * HEADS_PER_GROUP
    second_pair = pl.program_id(1) == 1
    slopes = [jnp.where(second_pair, _slope(first_head + heads + h), _slope(first_head + h))
              for h in range(heads)]
    qi = lax.broadcasted_iota(jnp.int32, (n, 2 * n), 0)
    ku = lax.broadcasted_iota(jnp.int32, (n, 2 * n), 1)
    delta2 = qi + n - ku
    valid2 = jnp.where(delta2 >= 0, delta2, n + 1) <= n
    delta1 = delta2[:, n:]
    valid1 = delta1 >= 0
    def stream_rows(start):
        return pl.ds(start, n, stride=dil) if dil > 1 else pl.ds(start, n)

    def block(start, has_prev):
        cur = stream_rows(start)
        q = q_ref[0, cur, :]
        k = k_ref[0, cur, :]
        v = v_ref[0, cur, :]
        if has_prev:
            prev = stream_rows(start - n * dil)
            k = jnp.concatenate([k_ref[0, prev, :], k], axis=0)
            v = jnp.concatenate([v_ref[0, prev, :], v], axis=0)
            delta, valid = delta2, valid2
        else:
            delta, valid = delta1, valid1
        dist = (delta * dil).astype(F32)
        outs, lses = [], []
        for h in range(heads):
            lo, hi = h * HEAD_DIM, (h + 1) * HEAD_DIM
            s = lax.dot_general(q[:, lo:hi].astype(BF16), k[:, lo:hi].astype(BF16),
                                NT_DIMS, preferred_element_type=F32) * SCALE
            p, l, lse = _softmax_parts(jnp.where(valid, s - slopes[h] * dist, NEG_INF))
            outs.append(jnp.dot(p.astype(BF16), v[:, lo:hi].astype(BF16),
                                preferred_element_type=F32) / l)
            lses.append(jnp.broadcast_to(lse, (n, HEAD_DIM)))
        o_ref[0, cur, :] = jnp.concatenate(outs, axis=1)
        lse_ref[0, cur, :] = jnp.concatenate(lses, axis=1)

    for r in range(dil):
        for jb in range(n_blocks):
            block(r + jb * n * dil, jb > 0)


def _dil_prompt(q, kv, group):
    b, seq, _ = q.shape
    _, dil = DIL_GROUPS[group]
    pairs = GROUP_WIDTH // LANES
    k_col = group * pairs
    v_col = MIX_WIDTH // LANES + k_col
    blk = (1, seq, LANES)
    return pl.pallas_call(
        functools.partial(_dil_prompt_kernel, seq=seq, dil=dil, group=group),
        grid=(b, pairs),
        in_specs=[pl.BlockSpec(blk, lambda i, j: (i, 0, k_col + j)),
                  pl.BlockSpec(blk, lambda i, j: (i, 0, k_col + j)),
                  pl.BlockSpec(blk, lambda i, j: (i, 0, v_col + j))],
        out_specs=[pl.BlockSpec(blk, lambda i, j: (i, 0, j)), pl.BlockSpec(blk, lambda i, j: (i, 0, j))],
        out_shape=[jax.ShapeDtypeStruct((b, seq, GROUP_WIDTH), F32)] * 2,
        compiler_params=_params("parallel", "parallel"),
        name=f"dil_prompt_g{group}",
    )(q, kv, kv)


def _dil_sample_kernel(q_ref, kvn_ref, gate_ref, c0_ref, c1_ref, c2_ref,
                       y_ref, n0_ref, n1_ref, n2_ref, *, steps):
    rows = q_ref.shape[1]
    lead = rows - steps
    q_all = q_ref[0]
    kvn = kvn_ref[0]
    hg = HEADS_PER_GROUP
    stacked = lax.broadcasted_iota(jnp.int32, (hg * rows, 1), 0)
    row_head = stacked // rows
    tok = stacked - row_head * rows - lead
    kvn_t = jnp.concatenate([jnp.zeros((LANES - rows, 2 * MIX_WIDTH), F32), kvn], axis=0).T
    tail_lane = lax.broadcasted_iota(jnp.int32, (HEAD_DIM, LANES), 1) >= LANES - steps
    outs, lses = [], []
    for g, ((win, dil), c_ref, n_ref) in enumerate(zip(DIL_GROUPS, (c0_ref, c1_ref, c2_ref),
                                                       (n0_ref, n1_ref, n2_ref))):
        cols = slice(g * GROUP_WIDTH, (g + 1) * GROUP_WIDTH)
        k_new = kvn[:, cols]
        v_new = kvn[:, MIX_WIDTH + g * GROUP_WIDTH:MIX_WIDTH + (g + 1) * GROUP_WIDTH]
        kv_heads = [[c_ref[0, kv_i, h] for h in range(hg)] for kv_i in range(2)]
        for kv_i in range(2):
            for h in range(hg):
                shifted = pltpu.roll(kv_heads[kv_i][h], win - steps, axis=1)
                lo = kv_i * MIX_WIDTH + g * GROUP_WIDTH + h * HEAD_DIM
                new_t = kvn_t[lo:lo + HEAD_DIM, :]
                if win > LANES:
                    n_ref[0, kv_i, h, :, 0:win - LANES] = shifted[:, 0:win - LANES]
                n_ref[0, kv_i, h, :, win - LANES:win] = jnp.where(tail_lane, new_t, shifted[:, win - LANES:win])
        kt = jnp.concatenate(kv_heads[0], axis=0).astype(BF16)
        vt = jnp.concatenate(kv_heads[1], axis=0).astype(BF16)
        q_bd, head_mask = _head_rows(q_all[:, cols], hg)
        slope = jnp.zeros((hg * rows, 1), F32)
        for h in range(hg):
            slope = jnp.where(row_head == h, _slope(g * hg + h), slope)
        pos = lax.broadcasted_iota(jnp.int32, (hg * rows, win), 1)
        dist_buf = ((win + tok) - pos).astype(F32)
        valid_buf = jnp.where(pos >= tok, (pos - tok) & (dil - 1), 1) == 0
        s_buf = jnp.dot(q_bd.astype(BF16), kt, preferred_element_type=F32) * SCALE - slope * dist_buf
        s_buf = jnp.where(valid_buf, s_buf, NEG_INF)
        s_new = []
        for t2 in range(steps):
            col = jnp.sum(q_bd * k_new[lead + t2:lead + t2 + 1, :], axis=-1, keepdims=True) * SCALE \
                - slope * (tok - t2).astype(F32)
            ok = jnp.where(tok >= t2, (tok - t2) & (dil - 1), 1) == 0
            s_new.append(jnp.where(ok, col, NEG_INF))
        m = jnp.max(s_buf, axis=-1, keepdims=True)
        for col in s_new:
            m = jnp.maximum(m, col)
        p_buf = jnp.exp(s_buf - m)
        l = jnp.sum(p_buf, axis=-1, keepdims=True)
        o = lax.dot_general(p_buf.astype(BF16), vt, NT_DIMS, preferred_element_type=F32)
        for t2, col in enumerate(s_new):
            p_col = jnp.exp(col - m)
            l = l + p_col
            o = o + p_col * v_new[lead + t2:lead + t2 + 1, :]
        outs.append(o / l)
        lses.append(m + jnp.log(l))
    mx = jnp.maximum(jnp.maximum(lses[0], lses[1]), lses[2])
    es = [jnp.exp(l - mx) for l in lses]
    den = es[0] + es[1] + es[2]
    mixed = [_head_cols(outs[g] * (es[g] / den), head_mask, hg) for g in range(len(DIL_GROUPS))]
    y_ref[0] = jnp.concatenate(mixed, axis=1) * _silu(gate_ref[0])


def _dil_sample(q, kv_new, gate, caches_t):
    b, steps, _ = q.shape
    rows = _round_up(steps, SUBLANES)
    pad = ((0, 0), (rows - steps, 0), (0, 0))
    q, kv_new, gate = jnp.pad(q, pad), jnp.pad(kv_new, pad), jnp.pad(gate, pad)
    tok = lambda i: (i, 0, 0)
    cache_specs = [pl.BlockSpec((1,) + c.shape[1:], lambda i: (i, 0, 0, 0, 0)) for c in caches_t]
    res = pl.pallas_call(
        functools.partial(_dil_sample_kernel, steps=steps),
        grid=(b,),
        in_specs=[pl.BlockSpec((1, rows, MIX_WIDTH), tok), pl.BlockSpec((1, rows, 2 * MIX_WIDTH), tok),
                  pl.BlockSpec((1, rows, MIX_WIDTH), tok)] + cache_specs,
        out_specs=[pl.BlockSpec((1, rows, MIX_WIDTH), tok)] + cache_specs,
        out_shape=[jax.ShapeDtypeStruct((b, rows, MIX_WIDTH), F32)]
        + [jax.ShapeDtypeStruct(c.shape, F32) for c in caches_t],
        compiler_params=_params(),
        name="dil_sample",
    )(q, kv_new, gate, *caches_t)
    return [res[0][:, rows - steps:]] + list(res[1:])


def _trunk(x, batch, seq, mem_kv_t, h0_re, h0_im, weights, tm, s5_tk, s5_chunk, mem_block, mem_tq,
           side_dtype, kv_proj, dil_attn):
    (w_in, w_out, ln_g, ln_b, tables, w_glu, b_glu) = weights
    splits = (MIX_WIDTH, MIX_WIDTH, MEM_WIDTH, MEM_WIDTH)
    seq_mem = _round_up(seq, SUBLANES)

    def mem_branch(mq, mgate, layer):
        mq = mq.reshape(batch, seq, MEM_WIDTH)
        mgate = mgate.reshape(batch, seq, MEM_WIDTH)
        if seq_mem != seq:
            pad = ((0, 0), (0, seq_mem - seq), (0, 0))
            mq, mgate = jnp.pad(mq, pad), jnp.pad(mgate, pad)
        m = _mem_attn(mq, mem_kv_t, layer, mgate, mem_block, min(mem_tq, seq_mem))
        return m[:, :seq].reshape(batch * seq, MEM_WIDTH)

    if s5_tk == seq:
        block_of = None
    else:
        assert tm == s5_tk
        n_blocks = seq // s5_tk
        block_of = lambda i: (i % n_blocks) * batch + i // n_blocks
    u, gate, mq, mgate = _proj(x, w_in[0], splits, tm, first_out_block=block_of, rest_dtype=side_dtype)
    y, h_re, h_im = _s5_mixer(u, h0_re, h0_im, tables, batch, seq, s5_tk, s5_chunk)
    m = mem_branch(mq, mgate, 0)
    x = _layer_tail(_mix_glu, (y, gate), (w_glu, b_glu), m, x, w_out[0], ln_g[0:1], ln_b[0:1], tm,
                    first_block=block_of)
    kv, kv_extras = kv_proj(x)

    u, gate, mq, mgate = _proj(x, w_in[1], splits, tm, rest_dtype=side_dtype)
    mix_fn, mix_inputs, attn_extras = dil_attn(u, kv, gate)
    m = mem_branch(mq, mgate, 1)
    x = _layer_tail(mix_fn, mix_inputs, (), m, x, w_out[1], ln_g[1:2], ln_b[1:2], tm)
    return x, h_re, h_im, kv, kv_extras, attn_extras


def kernel(x_prompt, x_sample, cache_mem_kv, state_ssm_re, state_ssm_im, cache_dil1_kv, cache_dil4_kv,
           cache_dil16_kv, mem_prompt, w_in, w_out, ln_g, ln_b, w_mem_kv, ssm_lambda_re, ssm_lambda_im,
           ssm_log_dt, ssm_b_re, ssm_b_im, ssm_c_re, ssm_c_im, ssm_d, w_glu, b_glu, w_kv_shared):
    bp, seq, d = x_prompt.shape
    bs, steps, _ = x_sample.shape
    prompt_chunk, prompt_tk = 8, 512
    n_groups = len(DIL_GROUPS)

    w_in_b = w_in.astype(BF16)
    w_out_b = w_out.astype(BF16)
    w_glu_b = w_glu[0].astype(BF16)
    w_kv_b = w_kv_shared.astype(BF16)
    ssm = (ssm_lambda_re[0], ssm_lambda_im[0], ssm_log_dt[0], ssm_b_re[0], ssm_b_im[0],
           ssm_c_re[0], ssm_c_im[0], ssm_d[0])

    def weights(chunk):
        return (w_in_b, w_out_b, ln_g, ln_b, _s5_tables(chunk, *ssm), w_glu_b, b_glu)

    mem_kv_p = _mem_kv_t(w_mem_kv.transpose(0, 2, 1).astype(BF16), mem_prompt)
    mem_kv_p = mem_kv_p.reshape(DEPTH, bp, 2, MEM_HEADS, HEAD_DIM, N_MEM)
    zeros = jnp.zeros((bp, SSM_GROUPS, SSM_STATE), F32)
    wide = n_groups - 1
    assert DIL_GROUPS[wide][0] >= seq and all(w <= prompt_tk for w, _ in DIL_GROUPS[:wide])

    def group_cols_t(g):
        return jnp.concatenate([w_kv_b[:, kv_i * MIX_WIDTH + g * GROUP_WIDTH:
                                       kv_i * MIX_WIDTH + (g + 1) * GROUP_WIDTH] for kv_i in range(2)], axis=1).T

    wt_all = group_cols_t(wide)
    wt_tail = jnp.concatenate([group_cols_t(g) for g in range(wide)], axis=0)

    def kv_prompt(x):
        kv, kvt_all, kvt_tail = _kv_proj(x, w_kv_b, wt_all, wt_tail, bp, seq, prompt_tk)
        return kv, (kvt_all, kvt_tail)

    def attn_prompt(u, kv, gate):
        q3 = u.reshape(bp, seq, MIX_WIDTH)
        kv3 = kv.reshape(bp, seq, 2 * MIX_WIDTH)
        res = [_dil_prompt(q3, kv3, g) for g in range(n_groups)]
        outs = [r[0].reshape(bp * seq, GROUP_WIDTH) for r in res]
        lses = [r[1].reshape(bp * seq, GROUP_WIDTH) for r in res]
        return _mix_merge, (*outs, *lses, gate), None

    y_p, hre_p, him_p, _, (kvt_all, kvt_tail), _ = _trunk(
        x_prompt.reshape(bp * seq, d), bp, seq, mem_kv_p, zeros, zeros, weights(prompt_chunk),
        prompt_tk, prompt_tk, prompt_chunk, 1, 512, BF16, kv_prompt, attn_prompt)
    kvt_tail = kvt_tail.reshape(bp, wide, 2, HEADS_PER_GROUP, HEAD_DIM, prompt_tk)
    win_p = [kvt_tail[:, g, :, :, :, prompt_tk - win:].transpose(0, 4, 1, 2, 3)
             for g, (win, _) in enumerate(DIL_GROUPS[:wide])]
    win_p.append(kvt_all.reshape(bp, 2, HEADS_PER_GROUP, HEAD_DIM, seq).transpose(0, 4, 1, 2, 3))

    caches = (cache_dil1_kv, cache_dil4_kv, cache_dil16_kv)
    caches_t = [c.transpose(0, 2, 3, 4, 1) for c in caches]
    mem_kv_s = cache_mem_kv.transpose(0, 1, 3, 4, 5, 2)

    def kv_sample(x):
        return _proj(x, w_kv_b, (2 * MIX_WIDTH,), 256)[0], None

    def attn_sample(u, kv, gate):
        res = _dil_sample(u.reshape(bs, steps, MIX_WIDTH), kv.reshape(bs, steps, 2 * MIX_WIDTH),
                          gate.reshape(bs, steps, MIX_WIDTH), caches_t)
        return _mix_plain, (res[0].reshape(bs * steps, MIX_WIDTH),), res[1:]

    y_s, hre_s, him_s, _, _, rolled = _trunk(
        x_sample.reshape(bs * steps, d), bs, steps, mem_kv_s, state_ssm_re[0], state_ssm_im[0],
        weights(steps), 256, steps, steps, 8, 8, F32, kv_sample, attn_sample)
    win_s = [c.transpose(0, 4, 1, 2, 3) for c in rolled]

    mem_kv_out = mem_kv_p.transpose(0, 1, 5, 2, 3, 4)
    return (y_p.reshape(bp, seq, d), y_s.reshape(bs, steps, d), mem_kv_out,
            hre_p[None], him_p[None], win_p[0], win_p[1], win_p[2],
            hre_s[None], him_s[None], win_s[0], win_s[1], win_s[2])
```

```python
import functools

import jax
import jax.numpy as jnp
from jax import lax
from jax.experimental import pallas as pl
from jax.experimental.pallas import tpu as pltpu

F32 = jnp.float32
BF16 = jnp.bfloat16

D_MODEL = 1024
HEAD_DIM = 64
MIX_WIDTH = 768
MEM_WIDTH = 256
MEM_HEADS = 4
N_MEM = 256
SSM_GROUP = 16
SSM_GROUPS = 48
SSM_STATE = 64
DIL_GROUPS = ((128, 1), (512, 4), (2048, 16))
ATT_HEADS = 12
HEADS_PER_GROUP = 4
GROUP_WIDTH = HEADS_PER_GROUP * HEAD_DIM
DEPTH = 2
DEEPNORM_ALPHA = (2.0 * DEPTH) ** 0.25
LN_EPS = 1e-5
SCALE = HEAD_DIM ** -0.5
NEG_INF = -1e30
WINDOW_KEYS = 128
LANES = 128
SUBLANES = 8
SLAB_GROUPS = LANES // SSM_GROUP
SSM_SLABS = SSM_GROUPS // SLAB_GROUPS
SLAB_STATE = SLAB_GROUPS * SSM_STATE
VMEM_LIMIT = 56 * 1024 * 1024
NT_DIMS = (((1,), (1,)), ((), ()))


def _slope(head):
    return 2.0 ** (-8.0 * (head + 1) / ATT_HEADS)


def _params(*semantics):
    return pltpu.CompilerParams(dimension_semantics=semantics or ("parallel",),
                                vmem_limit_bytes=VMEM_LIMIT)


def _silu(x):
    return x * jax.nn.sigmoid(x)


def _round_up(n, m):
    return -(-n // m) * m


def _proj_kernel(x_ref, w_ref, *out_refs, splits):
    x = x_ref[...].astype(BF16)
    off = 0
    for o_ref, n in zip(out_refs, splits):
        o_ref[...] = jnp.dot(x, w_ref[:, off:off + n], preferred_element_type=F32).astype(o_ref.dtype)
        off += n


def _proj(x, w, splits, tm, first_out_block=None, rest_dtype=F32):
    m, k = x.shape
    n_total = w.shape[1]
    assert sum(splits) == n_total and m % tm == 0
    tok = lambda i: (i, 0)
    first = tok if first_out_block is None else (lambda i: (first_out_block(i), 0))
    out_maps = [first] + [tok] * (len(splits) - 1)
    return pl.pallas_call(
        functools.partial(_proj_kernel, splits=splits),
        grid=(m // tm,),
        in_specs=[pl.BlockSpec((tm, k), tok), pl.BlockSpec((k, n_total), lambda i: (0, 0))],
        out_specs=[pl.BlockSpec((tm, n), om) for n, om in zip(splits, out_maps)],
        out_shape=[jax.ShapeDtypeStruct((m, n), F32 if i == 0 else rest_dtype) for i, n in enumerate(splits)],
        compiler_params=_params(),
        name="proj",
    )(x, w)


def _kv_proj_kernel(x_ref, w_ref, wt_all_ref, wt_tail_ref, kv_ref, kvt_all_ref, kvt_tail_ref, *, tiles):
    x = x_ref[...].astype(BF16)
    kv_ref[...] = jnp.dot(x, w_ref[...], preferred_element_type=F32)
    kvt_all_ref[0] = lax.dot_general(wt_all_ref[...], x, NT_DIMS, preferred_element_type=F32)

    @pl.when(pl.program_id(0) % tiles == tiles - 1)
    def _():
        kvt_tail_ref[0] = lax.dot_general(wt_tail_ref[...], x, NT_DIMS, preferred_element_type=F32)


def _kv_proj(x, w, wt_all, wt_tail, batch, seq, tm):
    m, k = x.shape
    n_total, n_all, n_tail = w.shape[1], wt_all.shape[0], wt_tail.shape[0]
    tiles = seq // tm
    fixed = lambda i: (0, 0)
    return pl.pallas_call(
        functools.partial(_kv_proj_kernel, tiles=tiles),
        grid=(m // tm,),
        in_specs=[pl.BlockSpec((tm, k), lambda i: (i, 0)),
                  pl.BlockSpec((k, n_total), fixed),
                  pl.BlockSpec((n_all, k), fixed),
                  pl.BlockSpec((n_tail, k), fixed)],
        out_specs=[pl.BlockSpec((tm, n_total), lambda i: (i, 0)),
                   pl.BlockSpec((1, n_all, tm), lambda i: (i // tiles, 0, i % tiles)),
                   pl.BlockSpec((1, n_tail, tm), lambda i: (i // tiles, 0, 0))],
        out_shape=[jax.ShapeDtypeStruct((m, n_total), F32),
                   jax.ShapeDtypeStruct((batch, n_all, seq), F32),
                   jax.ShapeDtypeStruct((batch, n_tail, tm), F32)],
        compiler_params=_params("arbitrary"),
        name="kv_proj",
    )(x, w, wt_all, wt_tail)


def _mem_kv_t_kernel(w_ref, mem_ref, o_ref):
    o_ref[0, 0] = lax.dot_general(w_ref[0], mem_ref[0].astype(BF16), NT_DIMS, preferred_element_type=F32)


def _mem_kv_t(w_t, mem):
    depth, n, d = w_t.shape
    b = mem.shape[0]
    return pl.pallas_call(
        _mem_kv_t_kernel,
        grid=(depth, b),
        in_specs=[pl.BlockSpec((1, n, d), lambda l, i: (l, 0, 0)),
                  pl.BlockSpec((1, N_MEM, d), lambda l, i: (i, 0, 0))],
        out_specs=pl.BlockSpec((1, 1, n, N_MEM), lambda l, i: (l, i, 0, 0)),
        out_shape=jax.ShapeDtypeStruct((depth, b, n, N_MEM), F32),
        compiler_params=_params("parallel", "parallel"),
        name="mem_kv_t",
    )(w_t, mem)


def _s5_tables(chunk, lam_re, lam_im, log_dt, b_re, b_im, c_re, c_im, d_skip):
    hp = lax.Precision.HIGHEST
    p, c = SSM_STATE, SSM_GROUP
    ns, gs = SSM_SLABS, SLAB_GROUPS
    kl = chunk * LANES
    lr = jnp.minimum(lam_re.astype(F32), -1e-4)
    li = lam_im.astype(F32)
    dt = jnp.exp(log_dt.astype(F32))[:, None]
    def lam_bar_pow(n):
        n = n.astype(F32)[None, :, None]
        mag = jnp.exp(lr[:, None, :] * dt[:, None, :] * n)
        ang = li[:, None, :] * dt[:, None, :] * n
        return mag * jnp.cos(ang), mag * jnp.sin(ang)

    pw_re, pw_im = lam_bar_pow(jnp.arange(chunk + 1))
    nr, ni = pw_re[:, 1] - 1.0, pw_im[:, 1]
    den = lr * lr + li * li
    f_re, f_im = (nr * lr + ni * li) / den, (ni * lr - nr * li) / den
    bb_re = f_re[:, :, None] * b_re - f_im[:, :, None] * b_im
    bb_im = f_re[:, :, None] * b_im + f_im[:, :, None] * b_re
    c_re, c_im = c_re.astype(F32), c_im.astype(F32)

    x_re = pw_re[:, :chunk, :, None] * bb_re[:, None] - pw_im[:, :chunk, :, None] * bb_im[:, None]
    x_im = pw_re[:, :chunk, :, None] * bb_im[:, None] + pw_im[:, :chunk, :, None] * bb_re[:, None]
    conv = (jnp.einsum('gcp,gtpd->gtcd', c_re, x_re, precision=hp)
            - jnp.einsum('gcp,gtpd->gtcd', c_im, x_im, precision=hp))
    def slab_diag(x):
        r, w = x.shape[-2:]
        x = jnp.tile(x, (1,) * (x.ndim - 1) + (gs,))
        own = (jnp.arange(gs * w) // w)[None, :] == jnp.arange(gs)[:, None]
        x = jnp.where(own.reshape((1, gs) + (1,) * (x.ndim - 3) + (gs * w,)), x, 0.0)
        x = jnp.moveaxis(x, 1, -3)
        return x.reshape(x.shape[:-3] + (gs * r, gs * w))

    conv_d = slab_diag(conv.transpose(0, 1, 3, 2).reshape(ns, gs, chunk, c, c))
    b_diag = lambda x: slab_diag(x.transpose(0, 2, 1).reshape(ns, gs, c, p))
    c_diag = lambda x: slab_diag(x.transpose(0, 2, 1).reshape(ns, gs, p, c))

    def slab_rows(x):
        n = x.shape[1]
        return x.reshape(ns, gs, n, p).transpose(0, 2, 1, 3).reshape(ns, n, SLAB_STATE)

    strip = conv_d.transpose(0, 2, 1, 3).reshape(ns, LANES, kl)
    toep = jnp.stack([jnp.pad(strip[:, :, :kl - s * LANES], ((0, 0), (0, 0), (s * LANES, 0)))
                      for s in range(chunk)], axis=1).reshape(ns, kl, kl)

    rev_re, rev_im = lam_bar_pow(chunk - 1 - jnp.arange(chunk))
    rev_re, rev_im = slab_rows(rev_re), slab_rows(rev_im)
    bd_re, bd_im = b_diag(bb_re), b_diag(bb_im)
    rev_a = jnp.concatenate([rev_re, rev_re], axis=-1)[:, :, None, :]
    rev_b = jnp.concatenate([-rev_im, rev_im], axis=-1)[:, :, None, :]
    bd_a = jnp.concatenate([bd_re, bd_im], axis=-1)[:, None]
    bd_b = jnp.concatenate([bd_im, bd_re], axis=-1)[:, None]
    w_in = (rev_a * bd_a + rev_b * bd_b).reshape(ns, kl, 2 * SLAB_STATE)

    col_re = slab_rows(pw_re[:, 1:]).transpose(0, 2, 1)
    col_im = slab_rows(pw_im[:, 1:]).transpose(0, 2, 1)
    cd_re, cd_im = c_diag(c_re), c_diag(c_im)
    col_a = jnp.repeat(jnp.concatenate([col_re, -col_re], axis=1), LANES, axis=-1)
    col_b = jnp.repeat(jnp.concatenate([-col_im, -col_im], axis=1), LANES, axis=-1)
    cd_a = jnp.tile(jnp.concatenate([cd_re, cd_im], axis=1), (1, 1, chunk))
    cd_b = jnp.tile(jnp.concatenate([cd_im, cd_re], axis=1), (1, 1, chunk))
    w_out = col_a * cd_a + col_b * cd_b

    lam_pows = jnp.stack([slab_rows(pw_re).transpose(1, 0, 2),
                          slab_rows(pw_im).transpose(1, 0, 2)], axis=2)
    skip = jnp.tile(d_skip.astype(F32).reshape(ns, LANES), (1, chunk))[:, None]
    return toep.astype(BF16), w_in.astype(BF16), w_out.astype(BF16), lam_pows, skip


def _s5_kernel(u_ref, toep_ref, win_ref, wout_ref, lam_ref, skip_ref, h0_ref, y_ref, hout_ref,
               a_scr, g_scr, hs_scr, y_scr, h_scr, *, batch, tk, chunk):
    n_chunks = tk // chunk
    ns = SLAB_STATE

    @pl.when(pl.program_id(1) == 0)
    def _():
        h_scr[...] = h0_ref[0]

    tile_swap = batch == SUBLANES and chunk == SUBLANES

    def gather(kk, carry):
        rows = pl.ds(pl.multiple_of(kk * batch, SUBLANES), batch)
        if tile_swap:
            tiles = [u_ref[pl.ds(pl.multiple_of(b * tk + kk * chunk, SUBLANES), chunk), :] for b in range(batch)]
            by_token = pltpu.einshape("btl->tbl", jnp.stack(tiles, axis=0))
            for t in range(chunk):
                a_scr[rows, t * LANES:(t + 1) * LANES] = by_token[t]
            return carry
        for t in range(chunk):
            a_scr[rows, t * LANES:(t + 1) * LANES] = u_ref[pl.ds(kk * chunk + t, batch, stride=tk), :]
        return carry

    lax.fori_loop(0, n_chunks, gather, 0)
    a = a_scr[...].astype(BF16)
    g_scr[...] = jnp.dot(a, win_ref[0], preferred_element_type=F32)
    ar = lam_ref[0, 0:1, :]
    ai = lam_ref[0, 1:2, :]

    def step(kk, carry):
        hr, hi = carry
        rows = pl.ds(pl.multiple_of(kk * batch, SUBLANES), batch)
        hs_scr[rows, 0:ns] = hr
        hs_scr[rows, ns:2 * ns] = hi
        return (ar * hr - ai * hi + g_scr[rows, 0:ns], ar * hi + ai * hr + g_scr[rows, ns:2 * ns])

    hr, hi = lax.fori_loop(0, n_chunks, step, (h_scr[:, 0:ns], h_scr[:, ns:2 * ns]))
    h_scr[:, 0:ns] = hr
    h_scr[:, ns:2 * ns] = hi
    hout_ref[0, :, 0:ns] = hr
    hout_ref[0, :, ns:2 * ns] = hi
    y = (jnp.dot(a, toep_ref[0], preferred_element_type=F32)
         + jnp.dot(hs_scr[...].astype(BF16), wout_ref[0], preferred_element_type=F32)
         + skip_ref[0] * a_scr[...])
    y_scr[...] = jax.nn.gelu(y)

    def scatter(kk, carry):
        rows = pl.ds(pl.multiple_of(kk * batch, SUBLANES), batch)
        if tile_swap:
            by_token = jnp.stack([y_scr[rows, t * LANES:(t + 1) * LANES] for t in range(chunk)], axis=0)
            by_seq = pltpu.einshape("tbl->btl", by_token)
            for b in range(batch):
                y_ref[pl.ds(pl.multiple_of(b * tk + kk * chunk, SUBLANES), chunk), :] = by_seq[b]
            return carry
        for t in range(chunk):
            y_ref[pl.ds(kk * chunk + t, batch, stride=tk), :] = y_scr[rows, t * LANES:(t + 1) * LANES]
        return carry

    lax.fori_loop(0, n_chunks, scatter, 0)


def _s5_mixer(u, h0_re, h0_im, tables, batch, seq, tk, chunk):
    toep, w_in, w_out, lam_pows, skip = tables
    kl = chunk * LANES
    table_chunk = toep.shape[-1] // LANES
    assert (table_chunk - chunk) % chunk == 0
    w_in_block = (table_chunk - chunk) // chunk
    lam_l = lam_pows[chunk]
    n_blocks = seq // tk
    rows = (tk // chunk) * batch
    to_slabs = lambda h: h.reshape(batch, SSM_SLABS, SLAB_STATE).transpose(1, 0, 2)
    h0 = jnp.concatenate([to_slabs(h0_re), to_slabs(h0_im)], axis=-1)
    slab = lambda j, k: (j, 0, 0)
    y, h = pl.pallas_call(
        functools.partial(_s5_kernel, batch=batch, tk=tk, chunk=chunk),
        grid=(SSM_SLABS, n_blocks),
        in_specs=[pl.BlockSpec((batch * tk, LANES), lambda j, k: (k, j)),
                  pl.BlockSpec((1, kl, kl), slab),
                  pl.BlockSpec((1, kl, 2 * SLAB_STATE), lambda j, k: (j, w_in_block, 0)),
                  pl.BlockSpec((1, 2 * SLAB_STATE, kl), slab),
                  pl.BlockSpec((1, 2, SLAB_STATE), slab),
                  pl.BlockSpec((1, 1, kl), slab),
                  pl.BlockSpec((1, batch, 2 * SLAB_STATE), slab)],
        out_specs=[pl.BlockSpec((batch * tk, LANES), lambda j, k: (k, j)),
                   pl.BlockSpec((1, batch, 2 * SLAB_STATE), slab)],
        out_shape=[jax.ShapeDtypeStruct(u.shape, F32),
                   jax.ShapeDtypeStruct((SSM_SLABS, batch, 2 * SLAB_STATE), F32)],
        scratch_shapes=[pltpu.VMEM((rows, kl), F32), pltpu.VMEM((rows, 2 * SLAB_STATE), F32),
                        pltpu.VMEM((rows, 2 * SLAB_STATE), F32), pltpu.VMEM((rows, kl), F32),
                        pltpu.VMEM((batch, 2 * SLAB_STATE), F32)],
        compiler_params=_params("parallel", "arbitrary"),
        name="s5_chunks",
    )(u, toep, w_in, w_out, lam_l, skip, h0)
    from_slabs = lambda x: x.transpose(1, 0, 2).reshape(batch, SSM_GROUPS, SSM_STATE)
    return y, from_slabs(h[:, :, :SLAB_STATE]), from_slabs(h[:, :, SLAB_STATE:])


def _head_rows(x, heads):
    t, w = x.shape
    row_head = lax.broadcasted_iota(jnp.int32, (heads * t, w), 0) // t
    lane_head = lax.broadcasted_iota(jnp.int32, (heads * t, w), 1) // HEAD_DIM
    mask = row_head == lane_head
    return jnp.where(mask, jnp.concatenate([x] * heads, axis=0), 0.0), mask


def _head_cols(x, mask, heads):
    t = x.shape[0] // heads
    x = jnp.where(mask, x, 0.0)
    out = x[0:t]
    for h in range(1, heads):
        out = out + x[h * t:(h + 1) * t]
    return out


def _mem_attn_kernel(q_ref, kv_ref, mg_ref, o_ref, *, block_batch):
    for bi in range(block_batch):
        q_bd, mask = _head_rows(q_ref[bi].astype(F32), MEM_HEADS)
        kt = jnp.concatenate([kv_ref[0, bi, 0, h] for h in range(MEM_HEADS)], axis=0).astype(BF16)
        vt = jnp.concatenate([kv_ref[0, bi, 1, h] for h in range(MEM_HEADS)], axis=0).astype(BF16)
        s = jnp.dot(q_bd.astype(BF16), kt, preferred_element_type=F32) * SCALE
        p = jnp.exp(s - jnp.max(s, axis=-1, keepdims=True))
        l = jnp.sum(p, axis=-1, keepdims=True)
        o = lax.dot_general(p.astype(BF16), vt, NT_DIMS, preferred_element_type=F32) / l
        o_ref[bi] = (_head_cols(o, mask, MEM_HEADS) * _silu(mg_ref[bi].astype(F32))).astype(o_ref.dtype)


def _mem_attn(q, kv_t, layer, mgate, block_batch, tq):
    b, t, w = q.shape
    tok = lambda i, j: (i, j, 0)
    return pl.pallas_call(
        functools.partial(_mem_attn_kernel, block_batch=block_batch),
        grid=(b // block_batch, t // tq),
        in_specs=[pl.BlockSpec((block_batch, tq, w), tok),
                  pl.BlockSpec((1, block_batch) + kv_t.shape[2:], lambda i, j: (layer, i, 0, 0, 0, 0)),
                  pl.BlockSpec((block_batch, tq, w), tok)],
        out_specs=pl.BlockSpec((block_batch, tq, w), tok),
        out_shape=jax.ShapeDtypeStruct((b, t, w), q.dtype),
        compiler_params=_params("parallel", "parallel"),
        name="mem_attn",
    )(q, kv_t, mgate)


def _mix_plain(tok_refs, fixed_refs):
    (y_ref,) = tok_refs
    return y_ref[...]


def _mix_glu(tok_refs, fixed_refs):
    y_ref, gate_ref = tok_refs
    w_ref, b_ref = fixed_refs
    y = y_ref[...]
    z = jnp.dot(y.astype(BF16), w_ref[...], preferred_element_type=F32) + b_ref[...]
    return y * jax.nn.sigmoid(z) * _silu(gate_ref[...].astype(F32))


def _mix_merge(tok_refs, fixed_refs):
    o0, o1, o2, l0, l1, l2, gate_ref = tok_refs
    ls = [l0[...], l1[...], l2[...]]
    mx = jnp.maximum(jnp.maximum(ls[0], ls[1]), ls[2])
    es = [jnp.exp(l - mx) for l in ls]
    den = es[0] + es[1] + es[2]
    o = jnp.concatenate([o_ref[...] * (e / den) for o_ref, e in zip((o0, o1, o2), es)], axis=1)
    return o * _silu(gate_ref[...].astype(F32))


def _layer_tail_kernel(*refs, n_tok, n_fixed, mix_fn):
    mix = mix_fn(refs[:n_tok], refs[n_tok:n_tok + n_fixed])
    m_ref, x_ref, wy_ref, wm_ref, g_ref, b_ref, o_ref = refs[n_tok + n_fixed:]
    out = (jnp.dot(mix.astype(BF16), wy_ref[...], preferred_element_type=F32)
           + jnp.dot(m_ref[...].astype(BF16), wm_ref[...], preferred_element_type=F32))
    z = DEEPNORM_ALPHA * x_ref[...] + out
    mu = jnp.mean(z, axis=-1, keepdims=True)
    zc = z - mu
    var = jnp.mean(zc * zc, axis=-1, keepdims=True)
    o_ref[...] = zc * lax.rsqrt(var + LN_EPS) * g_ref[...] + b_ref[...]


def _layer_tail(mix_fn, tok_arrays, fixed_arrays, m, x, w_out, ln_g, ln_b, tm, first_block=None):
    rows, d = x.shape
    tok = lambda i: (i, 0)
    fixed = lambda i: (0, 0)
    first = tok if first_block is None else (lambda i: (first_block(i), 0))
    tok_maps = [first] + [tok] * (len(tok_arrays) - 1)
    return pl.pallas_call(
        functools.partial(_layer_tail_kernel, n_tok=len(tok_arrays), n_fixed=len(fixed_arrays), mix_fn=mix_fn),
        grid=(rows // tm,),
        in_specs=[pl.BlockSpec((tm, a.shape[1]), tm_map) for a, tm_map in zip(tok_arrays, tok_maps)]
        + [pl.BlockSpec(a.shape, fixed) for a in fixed_arrays]
        + [pl.BlockSpec((tm, MEM_WIDTH), tok), pl.BlockSpec((tm, d), tok),
           pl.BlockSpec((MIX_WIDTH, d), fixed), pl.BlockSpec((MEM_WIDTH, d), fixed),
           pl.BlockSpec((1, d), fixed), pl.BlockSpec((1, d), fixed)],
        out_specs=pl.BlockSpec((tm, d), tok),
        out_shape=jax.ShapeDtypeStruct((rows, d), F32),
        compiler_params=_params(),
        name="layer_tail",
    )(*tok_arrays, *fixed_arrays, m, x, w_out[:MIX_WIDTH], w_out[MIX_WIDTH:], ln_g, ln_b)


def _softmax_parts(s):
    m = jnp.max(s, axis=-1, keepdims=True)
    p = jnp.exp(s - m)
    l = jnp.sum(p, axis=-1, keepdims=True)
    return p, l, m + jnp.log(l)


def _dil_prompt_kernel(q_ref, k_ref, v_ref, o_ref, lse_ref, *, seq, dil, group):
    n = WINDOW_KEYS
    n_blocks = seq // dil // n
    heads = LANES // HEAD_DIM
    first_head = group * HEADS_PER_GROUP
    second_pair = pl.program_id(1) == 1
    slopes = [jnp.where(second_pair, _slope(first_head + heads + h), _slope(first_head + h))
              for h in range(heads)]
    qi = lax.broadcasted_iota(jnp.int32, (n, 2 * n), 0)
    ku = lax.broadcasted_iota(jnp.int32, (n, 2 * n), 1)
    delta2 = qi + n - ku
    valid2 = jnp.where(delta2 >= 0, delta2, n + 1) <= n
    delta1 = delta2[:, n:]
    valid1 = delta1 >= 0
    def stream_rows(start):
        return pl.ds(start, n, stride=dil) if dil > 1 else pl.ds(start, n)

    def block(start, has_prev):
        cur = stream_rows(start)
        q = q_ref[0, cur, :]
        k = k_ref[0, cur, :]
        v = v_ref[0, cur, :]
        if has_prev:
            prev = stream_rows(start - n * dil)
            k = jnp.concatenate([k_ref[0, prev, :], k], axis=0)
            v = jnp.concatenate([v_ref[0, prev, :], v], axis=0)
            delta, valid = delta2, valid2
        else:
            delta, valid = delta1, valid1
        dist = (delta * dil).astype(F32)
        outs, lses = [], []
        for h in range(heads):
            lo, hi = h * HEAD_DIM, (h + 1) * HEAD_DIM
            s = lax.dot_general(q[:, lo:hi].astype(BF16), k[:, lo:hi].astype(BF16),
                                NT_DIMS, preferred_element_type=F32) * SCALE
            p, l, lse = _softmax_parts(jnp.where(valid, s - slopes[h] * dist, NEG_INF))
            outs.append(jnp.dot(p.astype(BF16), v[:, lo:hi].astype(BF16),
                                preferred_element_type=F32) / l)
            lses.append(jnp.broadcast_to(lse, (n, HEAD_DIM)))
        o_ref[0, cur, :] = jnp.concatenate(outs, axis=1)
        lse_ref[0, cur, :] = jnp.concatenate(lses, axis=1)

    for r in range(dil):
        for jb in range(n_blocks):
            block(r + jb * n * dil, jb > 0)


def _dil_prompt(q, kv, group):
    b, seq, _ = q.shape
    _, dil = DIL_GROUPS[group]
    pairs = GROUP_WIDTH // LANES
    k_col = group * pairs
    v_col = MIX_WIDTH // LANES + k_col
    blk = (1, seq, LANES)
    return pl.pallas_call(
        functools.partial(_dil_prompt_kernel, seq=seq, dil=dil, group=group),
        grid=(b, pairs),
        in_specs=[pl.BlockSpec(blk, lambda i, j: (i, 0, k_col + j)),
                  pl.BlockSpec(blk, lambda i, j: (i, 0, k_col + j)),
                  pl.BlockSpec(blk, lambda i, j: (i, 0, v_col + j))],
        out_specs=[pl.BlockSpec(blk, lambda i, j: (i, 0, j)), pl.BlockSpec(blk, lambda i, j: (i, 0, j))],
        out_shape=[jax.ShapeDtypeStruct((b, seq, GROUP_WIDTH), F32)] * 2,
        compiler_params=_params("parallel", "parallel"),
        name=f"dil_prompt_g{group}",
    )(q, kv, kv)


def _dil_sample_kernel(q_ref, kvn_ref, gate_ref, c0_ref, c1_ref, c2_ref,
                       y_ref, n0_ref, n1_ref, n2_ref, *, steps):
    rows = q_ref.shape[1]
    lead = rows - steps
    q_all = q_ref[0]
    kvn = kvn_ref[0]
    hg = HEADS_PER_GROUP
    stacked = lax.broadcasted_iota(jnp.int32, (hg * rows, 1), 0)
    row_head = stacked // rows
    tok = stacked - row_head * rows - lead
    kvn_t = jnp.concatenate([jnp.zeros((LANES - rows, 2 * MIX_WIDTH), F32), kvn], axis=0).T
    tail_lane = lax.broadcasted_iota(jnp.int32, (HEAD_DIM, LANES), 1) >= LANES - steps
    outs, lses = [], []
    for g, ((win, dil), c_ref, n_ref) in enumerate(zip(DIL_GROUPS, (c0_ref, c1_ref, c2_ref),
                                                       (n0_ref, n1_ref, n2_ref))):
        cols = slice(g * GROUP_WIDTH, (g + 1) * GROUP_WIDTH)
        k_new = kvn[:, cols]
        v_new = kvn[:, MIX_WIDTH + g * GROUP_WIDTH:MIX_WIDTH + (g + 1) * GROUP_WIDTH]
        kv_heads = [[c_ref[0, kv_i, h] for h in range(hg)] for kv_i in range(2)]
        for kv_i in range(2):
            for h in range(hg):
                shifted = pltpu.roll(kv_heads[kv_i][h], win - steps, axis=1)
                lo = kv_i * MIX_WIDTH + g * GROUP_WIDTH + h * HEAD_DIM
                new_t = kvn_t[lo:lo + HEAD_DIM, :]
                if win > LANES:
                    n_ref[0, kv_i, h, :, 0:win - LANES] = shifted[:, 0:win - LANES]
                n_ref[0, kv_i, h, :, win - LANES:win] = jnp.where(tail_lane, new_t, shifted[:, win - LANES:win])
        kt = jnp.concatenate(kv_heads[0], axis=0).astype(BF16)
        vt = jnp.concatenate(kv_heads[1], axis=0).astype(BF16)
        q_bd, head_mask = _head_rows(q_all[:, cols], hg)
        slope = jnp.zeros((hg * rows, 1), F32)
        for h in range(hg):
            slope = jnp.where(row_head == h, _slope(g * hg + h), slope)
        pos = lax.broadcasted_iota(jnp.int32, (hg * rows, win), 1)
        dist_buf = ((win + tok) - pos).astype(F32)
        valid_buf = jnp.where(pos >= tok, (pos - tok) & (dil - 1), 1) == 0
        s_buf = jnp.dot(q_bd.astype(BF16), kt, preferred_element_type=F32) * SCALE - slope * dist_buf
        s_buf = jnp.where(valid_buf, s_buf, NEG_INF)
        s_new = []
        for t2 in range(steps):
            col = jnp.sum(q_bd * k_new[lead + t2:lead + t2 + 1, :], axis=-1, keepdims=True) * SCALE \
                - slope * (tok - t2).astype(F32)
            ok = jnp.where(tok >= t2, (tok - t2) & (dil - 1), 1) == 0
            s_new.append(jnp.where(ok, col, NEG_INF))
        m = jnp.max(s_buf, axis=-1, keepdims=True)
        for col in s_new:
            m = jnp.maximum(m, col)
        p_buf = jnp.exp(s_buf - m)
        l = jnp.sum(p_buf, axis=-1, keepdims=True)
        o = lax.dot_general(p_buf.astype(BF16), vt, NT_DIMS, preferred_element_type=F32)
        for t2, col in enumerate(s_new):
            p_col = jnp.exp(col - m)
            l = l + p_col
            o = o + p_col * v_new[lead + t2:lead + t2 + 1, :]
        outs.append(o / l)
        lses.append(m + jnp.log(l))
    mx = jnp.maximum(jnp.maximum(lses[0], lses[1]), lses[2])
    es = [jnp.exp(l - mx) for l in lses]
    den = es[0] + es[1] + es[2]
    mixed = [_head_cols(outs[g] * (es[g] / den), head_mask, hg) for g in range(len(DIL_GROUPS))]
    y_ref[0] = jnp.concatenate(mixed, axis=1) * _silu(gate_ref[0])


def _dil_sample(q, kv_new, gate, caches_t):
    b, steps, _ = q.shape
    rows = _round_up(steps, SUBLANES)
    pad = ((0, 0), (rows - steps, 0), (0, 0))
    q, kv_new, gate = jnp.pad(q, pad), jnp.pad(kv_new, pad), jnp.pad(gate, pad)
    tok = lambda i: (i, 0, 0)
    cache_specs = [pl.BlockSpec((1,) + c.shape[1:], lambda i: (i, 0, 0, 0, 0)) for c in caches_t]
    res = pl.pallas_call(
        functools.partial(_dil_sample_kernel, steps=steps),
        grid=(b,),
        in_specs=[pl.BlockSpec((1, rows, MIX_WIDTH), tok), pl.BlockSpec((1, rows, 2 * MIX_WIDTH), tok),
                  pl.BlockSpec((1, rows, MIX_WIDTH), tok)] + cache_specs,
        out_specs=[pl.BlockSpec((1, rows, MIX_WIDTH), tok)] + cache_specs,
        out_shape=[jax.ShapeDtypeStruct((b, rows, MIX_WIDTH), F32)]
        + [jax.ShapeDtypeStruct(c.shape, F32) for c in caches_t],
        compiler_params=_params(),
        name="dil_sample",
    )(q, kv_new, gate, *caches_t)
    return [res[0][:, rows - steps:]] + list(res[1:])


def _trunk(x, batch, seq, mem_kv_t, h0_re, h0_im, weights, tm, s5_tk, s5_chunk, mem_block, mem_tq,
           side_dtype, kv_proj, dil_attn):
    (w_in, w_out, ln_g, ln_b, tables, w_glu, b_glu) = weights
    splits = (MIX_WIDTH, MIX_WIDTH, MEM_WIDTH, MEM_WIDTH)
    seq_mem = _round_up(seq, SUBLANES)

    def mem_branch(mq, mgate, layer):
        mq = mq.reshape(batch, seq, MEM_WIDTH)
        mgate = mgate.reshape(batch, seq, MEM_WIDTH)
        if seq_mem != seq:
            pad = ((0, 0), (0, seq_mem - seq), (0, 0))
            mq, mgate = jnp.pad(mq, pad), jnp.pad(mgate, pad)
        m = _mem_attn(mq, mem_kv_t, layer, mgate, mem_block, min(mem_tq, seq_mem))
        return m[:, :seq].reshape(batch * seq, MEM_WIDTH)

    if s5_tk == seq:
        block_of = None
    else:
        assert tm == s5_tk
        n_blocks = seq // s5_tk
        block_of = lambda i: (i % n_blocks) * batch + i // n_blocks
    u, gate, mq, mgate = _proj(x, w_in[0], splits, tm, first_out_block=block_of, rest_dtype=side_dtype)
    y, h_re, h_im = _s5_mixer(u, h0_re, h0_im, tables, batch, seq, s5_tk, s5_chunk)
    m = mem_branch(mq, mgate, 0)
    x = _layer_tail(_mix_glu, (y, gate), (w_glu, b_glu), m, x, w_out[0], ln_g[0:1], ln_b[0:1], tm,
                    first_block=block_of)
    kv, kv_extras = kv_proj(x)

    u, gate, mq, mgate = _proj(x, w_in[1], splits, tm, rest_dtype=side_dtype)
    mix_fn, mix_inputs, attn_extras = dil_attn(u, kv, gate)
    m = mem_branch(mq, mgate, 1)
    x = _layer_tail(mix_fn, mix_inputs, (), m, x, w_out[1], ln_g[1:2], ln_b[1:2], tm)
    return x, h_re, h_im, kv, kv_extras, attn_extras


def kernel(x_prompt, x_sample, cache_mem_kv, state_ssm_re, state_ssm_im, cache_dil1_kv, cache_dil4_kv,
           cache_dil16_kv, mem_prompt, w_in, w_out, ln_g, ln_b, w_mem_kv, ssm_lambda_re, ssm_lambda_im,
           ssm_log_dt, ssm_b_re, ssm_b_im, ssm_c_re, ssm_c_im, ssm_d, w_glu, b_glu, w_kv_shared):
    bp, seq, d = x_prompt.shape
    bs, steps, _ = x_sample.shape
    prompt_chunk, prompt_tk = 8, 512
    n_groups = len(DIL_GROUPS)

    w_in_b = w_in.astype(BF16)
    w_out_b = w_out.astype(BF16)
    w_glu_b = w_glu[0].astype(BF16)
    w_kv_b = w_kv_shared.astype(BF16)
    ssm = (ssm_lambda_re[0], ssm_lambda_im[0], ssm_log_dt[0], ssm_b_re[0], ssm_b_im[0],
           ssm_c_re[0], ssm_c_im[0], ssm_d[0])

    assert prompt_chunk % steps == 0
    weights = (w_in_b, w_out_b, ln_g, ln_b, _s5_tables(prompt_chunk, *ssm), w_glu_b, b_glu)

    mem_kv_p = _mem_kv_t(w_mem_kv.transpose(0, 2, 1).astype(BF16), mem_prompt)
    mem_kv_p = mem_kv_p.reshape(DEPTH, bp, 2, MEM_HEADS, HEAD_DIM, N_MEM)
    zeros = jnp.zeros((bp, SSM_GROUPS, SSM_STATE), F32)
    wide = n_groups - 1
    assert DIL_GROUPS[wide][0] >= seq and all(w <= prompt_tk for w, _ in DIL_GROUPS[:wide])

    def group_cols_t(g):
        return jnp.concatenate([w_kv_b[:, kv_i * MIX_WIDTH + g * GROUP_WIDTH:
                                       kv_i * MIX_WIDTH + (g + 1) * GROUP_WIDTH] for kv_i in range(2)], axis=1).T

    wt_all = group_cols_t(wide)
    wt_tail = jnp.concatenate([group_cols_t(g) for g in range(wide)], axis=0)

    def kv_prompt(x):
        kv, kvt_all, kvt_tail = _kv_proj(x, w_kv_b, wt_all, wt_tail, bp, seq, prompt_tk)
        return kv, (kvt_all, kvt_tail)

    def attn_prompt(u, kv, gate):
        q3 = u.reshape(bp, seq, MIX_WIDTH)
        kv3 = kv.reshape(bp, seq, 2 * MIX_WIDTH)
        res = [_dil_prompt(q3, kv3, g) for g in range(n_groups)]
        outs = [r[0].reshape(bp * seq, GROUP_WIDTH) for r in res]
        lses = [r[1].reshape(bp * seq, GROUP_WIDTH) for r in res]
        return _mix_merge, (*outs, *lses, gate), None

    y_p, hre_p, him_p, _, (kvt_all, kvt_tail), _ = _trunk(
        x_prompt.reshape(bp * seq, d), bp, seq, mem_kv_p, zeros, zeros, weights,
        prompt_tk, prompt_tk, prompt_chunk, 1, 512, BF16, kv_prompt, attn_prompt)
    kvt_tail = kvt_tail.reshape(bp, wide, 2, HEADS_PER_GROUP, HEAD_DIM, prompt_tk)
    win_p = [kvt_tail[:, g, :, :, :, prompt_tk - win:].transpose(0, 4, 1, 2, 3)
             for g, (win, _) in enumerate(DIL_GROUPS[:wide])]
    win_p.append(kvt_all.reshape(bp, 2, HEADS_PER_GROUP, HEAD_DIM, seq).transpose(0, 4, 1, 2, 3))

    caches = (cache_dil1_kv, cache_dil4_kv, cache_dil16_kv)
    caches_t = [c.transpose(0, 2, 3, 4, 1) for c in caches]
    mem_kv_s = cache_mem_kv.transpose(0, 1, 3, 4, 5, 2)

    def kv_sample(x):
        return _proj(x, w_kv_b, (2 * MIX_WIDTH,), 256)[0], None

    def attn_sample(u, kv, gate):
        res = _dil_sample(u.reshape(bs, steps, MIX_WIDTH), kv.reshape(bs, steps, 2 * MIX_WIDTH),
                          gate.reshape(bs, steps, MIX_WIDTH), caches_t)
        return _mix_plain, (res[0].reshape(bs * steps, MIX_WIDTH),), res[1:]

    y_s, hre_s, him_s, _, _, rolled = _trunk(
        x_sample.reshape(bs * steps, d), bs, steps, mem_kv_s, state_ssm_re[0], state_ssm_im[0],
        weights, 256, steps, steps, 8, 8, F32, kv_sample, attn_sample)
    win_s = [c.transpose(0, 4, 1, 2, 3) for c in rolled]

    mem_kv_out = mem_kv_p.transpose(0, 1, 5, 2, 3, 4)
    return (y_p.reshape(bp, seq, d), y_s.reshape(bs, steps, d), mem_kv_out,
            hre_p[None], him_p[None], win_p[0], win_p[1], win_p[2],
            hre_s[None], him_s[None], win_s[0], win_s[1], win_s[2])
```

```python
import functools

import jax
import jax.numpy as jnp
from jax import lax
from jax.experimental import pallas as pl
from jax.experimental.pallas import tpu as pltpu

F32 = jnp.float32
BF16 = jnp.bfloat16

D_MODEL = 1024
HEAD_DIM = 64
MIX_WIDTH = 768
MEM_WIDTH = 256
MEM_HEADS = 4
N_MEM = 256
SSM_GROUP = 16
SSM_GROUPS = 48
SSM_STATE = 64
DIL_GROUPS = ((128, 1), (512, 4), (2048, 16))
ATT_HEADS = 12
HEADS_PER_GROUP = 4
GROUP_WIDTH = HEADS_PER_GROUP * HEAD_DIM
DEPTH = 2
DEEPNORM_ALPHA = (2.0 * DEPTH) ** 0.25
LN_EPS = 1e-5
SCALE = HEAD_DIM ** -0.5
NEG_INF = -1e30
WINDOW_KEYS = 128
LANES = 128
SUBLANES = 8
SLAB_GROUPS = LANES // SSM_GROUP
SSM_SLABS = SSM_GROUPS // SLAB_GROUPS
SLAB_STATE = SLAB_GROUPS * SSM_STATE
VMEM_LIMIT = 56 * 1024 * 1024
NT_DIMS = (((1,), (1,)), ((), ()))


def _slope(head):
    return 2.0 ** (-8.0 * (head + 1) / ATT_HEADS)


def _params(*semantics):
    return pltpu.CompilerParams(dimension_semantics=semantics or ("parallel",),
                                vmem_limit_bytes=VMEM_LIMIT)


def _silu(x):
    return x * jax.nn.sigmoid(x)


def _round_up(n, m):
    return -(-n // m) * m


def _proj_kernel(x_ref, w_ref, *out_refs, splits):
    x = x_ref[...].astype(BF16)
    off = 0
    for o_ref, n in zip(out_refs, splits):
        o_ref[...] = jnp.dot(x, w_ref[:, off:off + n], preferred_element_type=F32).astype(o_ref.dtype)
        off += n


def _proj(x, w, splits, tm, first_out_block=None, rest_dtype=F32):
    m, k = x.shape
    n_total = w.shape[1]
    assert sum(splits) == n_total and m % tm == 0
    tok = lambda i: (i, 0)
    first = tok if first_out_block is None else (lambda i: (first_out_block(i), 0))
    out_maps = [first] + [tok] * (len(splits) - 1)
    return pl.pallas_call(
        functools.partial(_proj_kernel, splits=splits),
        grid=(m // tm,),
        in_specs=[pl.BlockSpec((tm, k), tok), pl.BlockSpec((k, n_total), lambda i: (0, 0))],
        out_specs=[pl.BlockSpec((tm, n), om) for n, om in zip(splits, out_maps)],
        out_shape=[jax.ShapeDtypeStruct((m, n), F32 if i == 0 else rest_dtype) for i, n in enumerate(splits)],
        compiler_params=_params(),
        name="proj",
    )(x, w)


def _kv_proj_kernel(x_ref, w_ref, wt_all_ref, wt_tail_ref, kv_ref, kvt_all_ref, kvt_tail_ref, *, tiles):
    x = x_ref[...].astype(BF16)
    kv_ref[...] = jnp.dot(x, w_ref[...], preferred_element_type=F32)
    kvt_all_ref[0] = lax.dot_general(wt_all_ref[...], x, NT_DIMS, preferred_element_type=F32)

    @pl.when(pl.program_id(0) % tiles == tiles - 1)
    def _():
        kvt_tail_ref[0] = lax.dot_general(wt_tail_ref[...], x, NT_DIMS, preferred_element_type=F32)


def _kv_proj(x, w, wt_all, wt_tail, batch, seq, tm):
    m, k = x.shape
    n_total, n_all, n_tail = w.shape[1], wt_all.shape[0], wt_tail.shape[0]
    tiles = seq // tm
    fixed = lambda i: (0, 0)
    return pl.pallas_call(
        functools.partial(_kv_proj_kernel, tiles=tiles),
        grid=(m // tm,),
        in_specs=[pl.BlockSpec((tm, k), lambda i: (i, 0)),
                  pl.BlockSpec((k, n_total), fixed),
                  pl.BlockSpec((n_all, k), fixed),
                  pl.BlockSpec((n_tail, k), fixed)],
        out_specs=[pl.BlockSpec((tm, n_total), lambda i: (i, 0)),
                   pl.BlockSpec((1, n_all, tm), lambda i: (i // tiles, 0, i % tiles)),
                   pl.BlockSpec((1, n_tail, tm), lambda i: (i // tiles, 0, 0))],
        out_shape=[jax.ShapeDtypeStruct((m, n_total), F32),
                   jax.ShapeDtypeStruct((batch, n_all, seq), F32),
                   jax.ShapeDtypeStruct((batch, n_tail, tm), F32)],
        compiler_params=_params("arbitrary"),
        name="kv_proj",
    )(x, w, wt_all, wt_tail)


def _mem_kv_t_kernel(w_ref, mem_ref, o_ref):
    o_ref[0, 0] = lax.dot_general(w_ref[0], mem_ref[0].astype(BF16), NT_DIMS, preferred_element_type=F32)


def _mem_kv_t(w_t, mem):
    depth, n, d = w_t.shape
    b = mem.shape[0]
    return pl.pallas_call(
        _mem_kv_t_kernel,
        grid=(depth, b),
        in_specs=[pl.BlockSpec((1, n, d), lambda l, i: (l, 0, 0)),
                  pl.BlockSpec((1, N_MEM, d), lambda l, i: (i, 0, 0))],
        out_specs=pl.BlockSpec((1, 1, n, N_MEM), lambda l, i: (l, i, 0, 0)),
        out_shape=jax.ShapeDtypeStruct((depth, b, n, N_MEM), F32),
        compiler_params=_params("parallel", "parallel"),
        name="mem_kv_t",
    )(w_t, mem)


def _s5_tables(chunk, lam_re, lam_im, log_dt, b_re, b_im, c_re, c_im, d_skip):
    hp = lax.Precision.HIGHEST
    p, c = SSM_STATE, SSM_GROUP
    ns, gs = SSM_SLABS, SLAB_GROUPS
    kl = chunk * LANES
    lr = jnp.minimum(lam_re.astype(F32), -1e-4)
    li = lam_im.astype(F32)
    dt = jnp.exp(log_dt.astype(F32))[:, None]
    def lam_bar_pow(n):
        n = n.astype(F32)[None, :, None]
        mag = jnp.exp(lr[:, None, :] * dt[:, None, :] * n)
        ang = li[:, None, :] * dt[:, None, :] * n
        return mag * jnp.cos(ang), mag * jnp.sin(ang)

    pw_re, pw_im = lam_bar_pow(jnp.arange(chunk + 1))
    nr, ni = pw_re[:, 1] - 1.0, pw_im[:, 1]
    den = lr * lr + li * li
    f_re, f_im = (nr * lr + ni * li) / den, (ni * lr - nr * li) / den
    bb_re = f_re[:, :, None] * b_re - f_im[:, :, None] * b_im
    bb_im = f_re[:, :, None] * b_im + f_im[:, :, None] * b_re
    c_re, c_im = c_re.astype(F32), c_im.astype(F32)

    x_re = pw_re[:, :chunk, :, None] * bb_re[:, None] - pw_im[:, :chunk, :, None] * bb_im[:, None]
    x_im = pw_re[:, :chunk, :, None] * bb_im[:, None] + pw_im[:, :chunk, :, None] * bb_re[:, None]
    conv = (jnp.einsum('gcp,gtpd->gtcd', c_re, x_re, precision=hp)
            - jnp.einsum('gcp,gtpd->gtcd', c_im, x_im, precision=hp))
    def slab_diag(x):
        r, w = x.shape[-2:]
        x = jnp.tile(x, (1,) * (x.ndim - 1) + (gs,))
        own = (jnp.arange(gs * w) // w)[None, :] == jnp.arange(gs)[:, None]
        x = jnp.where(own.reshape((1, gs) + (1,) * (x.ndim - 3) + (gs * w,)), x, 0.0)
        x = jnp.moveaxis(x, 1, -3)
        return x.reshape(x.shape[:-3] + (gs * r, gs * w))

    conv_d = slab_diag(conv.transpose(0, 1, 3, 2).reshape(ns, gs, chunk, c, c))
    b_diag = lambda x: slab_diag(x.transpose(0, 2, 1).reshape(ns, gs, c, p))
    c_diag = lambda x: slab_diag(x.transpose(0, 2, 1).reshape(ns, gs, p, c))

    def slab_rows(x):
        n = x.shape[1]
        return x.reshape(ns, gs, n, p).transpose(0, 2, 1, 3).reshape(ns, n, SLAB_STATE)

    strip = conv_d.transpose(0, 2, 1, 3).reshape(ns, LANES, kl)
    toep = jnp.stack([jnp.pad(strip[:, :, :kl - s * LANES], ((0, 0), (0, 0), (s * LANES, 0)))
                      for s in range(chunk)], axis=1).reshape(ns, kl, kl)

    rev_re, rev_im = lam_bar_pow(chunk - 1 - jnp.arange(chunk))
    rev_re, rev_im = slab_rows(rev_re), slab_rows(rev_im)
    bd_re, bd_im = b_diag(bb_re), b_diag(bb_im)
    rev_a = jnp.concatenate([rev_re, rev_re], axis=-1)[:, :, None, :]
    rev_b = jnp.concatenate([-rev_im, rev_im], axis=-1)[:, :, None, :]
    bd_a = jnp.concatenate([bd_re, bd_im], axis=-1)[:, None]
    bd_b = jnp.concatenate([bd_im, bd_re], axis=-1)[:, None]
    w_in = (rev_a * bd_a + rev_b * bd_b).reshape(ns, kl, 2 * SLAB_STATE)

    col_re = slab_rows(pw_re[:, 1:]).transpose(0, 2, 1)
    col_im = slab_rows(pw_im[:, 1:]).transpose(0, 2, 1)
    cd_re, cd_im = c_diag(c_re), c_diag(c_im)
    col_a = jnp.concatenate([col_re, -col_re], axis=1)
    col_b = jnp.concatenate([-col_im, -col_im], axis=1)
    cd_a = jnp.concatenate([cd_re, cd_im], axis=1)
    cd_b = jnp.concatenate([cd_im, cd_re], axis=1)
    w_out = jnp.concatenate([col_a[:, :, t:t + 1] * cd_a + col_b[:, :, t:t + 1] * cd_b
                             for t in range(chunk)], axis=-1)

    lam_pows = jnp.stack([slab_rows(pw_re).transpose(1, 0, 2),
                          slab_rows(pw_im).transpose(1, 0, 2)], axis=2)
    skip = jnp.tile(d_skip.astype(F32).reshape(ns, LANES), (1, chunk))[:, None]
    return toep.astype(BF16), w_in.astype(BF16), w_out.astype(BF16), lam_pows, skip


def _s5_kernel(u_ref, toep_ref, win_ref, wout_ref, lam_ref, skip_ref, h0_ref, y_ref, hout_ref,
               a_scr, g_scr, hs_scr, y_scr, h_scr, *, batch, tk, chunk):
    n_chunks = tk // chunk
    ns = SLAB_STATE

    @pl.when(pl.program_id(1) == 0)
    def _():
        h_scr[...] = h0_ref[0]

    tile_swap = batch == SUBLANES and chunk == SUBLANES

    def gather(kk, carry):
        rows = pl.ds(pl.multiple_of(kk * batch, SUBLANES), batch)
        if tile_swap:
            tiles = [u_ref[pl.ds(pl.multiple_of(b * tk + kk * chunk, SUBLANES), chunk), :] for b in range(batch)]
            by_token = pltpu.einshape("btl->tbl", jnp.stack(tiles, axis=0))
            for t in range(chunk):
                a_scr[rows, t * LANES:(t + 1) * LANES] = by_token[t]
            return carry
        for t in range(chunk):
            a_scr[rows, t * LANES:(t + 1) * LANES] = u_ref[pl.ds(kk * chunk + t, batch, stride=tk), :]
        return carry

    lax.fori_loop(0, n_chunks, gather, 0)
    a = a_scr[...].astype(BF16)
    g_scr[...] = jnp.dot(a, win_ref[0], preferred_element_type=F32)
    ar = lam_ref[0, 0:1, :]
    ai = lam_ref[0, 1:2, :]

    def step(kk, carry):
        hr, hi = carry
        rows = pl.ds(pl.multiple_of(kk * batch, SUBLANES), batch)
        hs_scr[rows, 0:ns] = hr
        hs_scr[rows, ns:2 * ns] = hi
        return (ar * hr - ai * hi + g_scr[rows, 0:ns], ar * hi + ai * hr + g_scr[rows, ns:2 * ns])

    hr, hi = lax.fori_loop(0, n_chunks, step, (h_scr[:, 0:ns], h_scr[:, ns:2 * ns]))
    h_scr[:, 0:ns] = hr
    h_scr[:, ns:2 * ns] = hi
    hout_ref[0, :, 0:ns] = hr
    hout_ref[0, :, ns:2 * ns] = hi
    y = (jnp.dot(a, toep_ref[0], preferred_element_type=F32)
         + jnp.dot(hs_scr[...].astype(BF16), wout_ref[0], preferred_element_type=F32)
         + skip_ref[0] * a_scr[...])
    y_scr[...] = jax.nn.gelu(y)

    def scatter(kk, carry):
        rows = pl.ds(pl.multiple_of(kk * batch, SUBLANES), batch)
        if tile_swap:
            by_token = jnp.stack([y_scr[rows, t * LANES:(t + 1) * LANES] for t in range(chunk)], axis=0)
            by_seq = pltpu.einshape("tbl->btl", by_token)
            for b in range(batch):
                y_ref[pl.ds(pl.multiple_of(b * tk + kk * chunk, SUBLANES), chunk), :] = by_seq[b]
            return carry
        for t in range(chunk):
            y_ref[pl.ds(kk * chunk + t, batch, stride=tk), :] = y_scr[rows, t * LANES:(t + 1) * LANES]
        return carry

    lax.fori_loop(0, n_chunks, scatter, 0)


def _s5_mixer(u, h0_re, h0_im, tables, batch, seq, tk, chunk):
    toep, w_in, w_out, lam_pows, skip = tables
    kl = chunk * LANES
    table_chunk = toep.shape[-1] // LANES
    assert (table_chunk - chunk) % chunk == 0
    w_in_block = (table_chunk - chunk) // chunk
    lam_l = lam_pows[chunk]
    n_blocks = seq // tk
    rows = (tk // chunk) * batch
    to_slabs = lambda h: h.reshape(batch, SSM_SLABS, SLAB_STATE).transpose(1, 0, 2)
    h0 = jnp.concatenate([to_slabs(h0_re), to_slabs(h0_im)], axis=-1)
    slab = lambda j, k: (j, 0, 0)
    y, h = pl.pallas_call(
        functools.partial(_s5_kernel, batch=batch, tk=tk, chunk=chunk),
        grid=(SSM_SLABS, n_blocks),
        in_specs=[pl.BlockSpec((batch * tk, LANES), lambda j, k: (k, j)),
                  pl.BlockSpec((1, kl, kl), slab),
                  pl.BlockSpec((1, kl, 2 * SLAB_STATE), lambda j, k: (j, w_in_block, 0)),
                  pl.BlockSpec((1, 2 * SLAB_STATE, kl), slab),
                  pl.BlockSpec((1, 2, SLAB_STATE), slab),
                  pl.BlockSpec((1, 1, kl), slab),
                  pl.BlockSpec((1, batch, 2 * SLAB_STATE), slab)],
        out_specs=[pl.BlockSpec((batch * tk, LANES), lambda j, k: (k, j)),
                   pl.BlockSpec((1, batch, 2 * SLAB_STATE), slab)],
        out_shape=[jax.ShapeDtypeStruct(u.shape, F32),
                   jax.ShapeDtypeStruct((SSM_SLABS, batch, 2 * SLAB_STATE), F32)],
        scratch_shapes=[pltpu.VMEM((rows, kl), F32), pltpu.VMEM((rows, 2 * SLAB_STATE), F32),
                        pltpu.VMEM((rows, 2 * SLAB_STATE), F32), pltpu.VMEM((rows, kl), F32),
                        pltpu.VMEM((batch, 2 * SLAB_STATE), F32)],
        compiler_params=_params("parallel", "arbitrary"),
        name="s5_chunks",
    )(u, toep, w_in, w_out, lam_l, skip, h0)
    from_slabs = lambda x: x.transpose(1, 0, 2).reshape(batch, SSM_GROUPS, SSM_STATE)
    return y, from_slabs(h[:, :, :SLAB_STATE]), from_slabs(h[:, :, SLAB_STATE:])


def _head_rows(x, heads):
    t, w = x.shape
    row_head = lax.broadcasted_iota(jnp.int32, (heads * t, w), 0) // t
    lane_head = lax.broadcasted_iota(jnp.int32, (heads * t, w), 1) // HEAD_DIM
    mask = row_head == lane_head
    return jnp.where(mask, jnp.concatenate([x] * heads, axis=0), 0.0), mask


def _head_cols(x, mask, heads):
    t = x.shape[0] // heads
    x = jnp.where(mask, x, 0.0)
    out = x[0:t]
    for h in range(1, heads):
        out = out + x[h * t:(h + 1) * t]
    return out


def _mem_attn_kernel(q_ref, kv_ref, mg_ref, o_ref, *, block_batch):
    for bi in range(block_batch):
        q_bd, mask = _head_rows(q_ref[bi].astype(F32), MEM_HEADS)
        kt = jnp.concatenate([kv_ref[0, bi, 0, h] for h in range(MEM_HEADS)], axis=0).astype(BF16)
        vt = jnp.concatenate([kv_ref[0, bi, 1, h] for h in range(MEM_HEADS)], axis=0).astype(BF16)
        s = jnp.dot(q_bd.astype(BF16), kt, preferred_element_type=F32) * SCALE
        p = jnp.exp(s - jnp.max(s, axis=-1, keepdims=True))
        l = jnp.sum(p, axis=-1, keepdims=True)
        o = lax.dot_general(p.astype(BF16), vt, NT_DIMS, preferred_element_type=F32) / l
        o_ref[bi] = (_head_cols(o, mask, MEM_HEADS) * _silu(mg_ref[bi].astype(F32))).astype(o_ref.dtype)


def _mem_attn(q, kv_t, layer, mgate, block_batch, tq):
    b, t, w = q.shape
    tok = lambda i, j: (i, j, 0)
    return pl.pallas_call(
        functools.partial(_mem_attn_kernel, block_batch=block_batch),
        grid=(b // block_batch, t // tq),
        in_specs=[pl.BlockSpec((block_batch, tq, w), tok),
                  pl.BlockSpec((1, block_batch) + kv_t.shape[2:], lambda i, j: (layer, i, 0, 0, 0, 0)),
                  pl.BlockSpec((block_batch, tq, w), tok)],
        out_specs=pl.BlockSpec((block_batch, tq, w), tok),
        out_shape=jax.ShapeDtypeStruct((b, t, w), q.dtype),
        compiler_params=_params("parallel", "parallel"),
        name="mem_attn",
    )(q, kv_t, mgate)


def _mix_plain(tok_refs, fixed_refs):
    (y_ref,) = tok_refs
    return y_ref[...]


def _mix_glu(tok_refs, fixed_refs):
    y_ref, gate_ref = tok_refs
    w_ref, b_ref = fixed_refs
    y = y_ref[...]
    z = jnp.dot(y.astype(BF16), w_ref[...], preferred_element_type=F32) + b_ref[...]
    return y * jax.nn.sigmoid(z) * _silu(gate_ref[...].astype(F32))


def _mix_merge(tok_refs, fixed_refs):
    o0, o1, o2, l0, l1, l2, gate_ref = tok_refs
    ls = [l0[...], l1[...], l2[...]]
    mx = jnp.maximum(jnp.maximum(ls[0], ls[1]), ls[2])
    es = [jnp.exp(l - mx) for l in ls]
    den = es[0] + es[1] + es[2]
    o = jnp.concatenate([o_ref[...] * (e / den) for o_ref, e in zip((o0, o1, o2), es)], axis=1)
    return o * _silu(gate_ref[...].astype(F32))


def _layer_tail_kernel(*refs, n_tok, n_fixed, mix_fn):
    mix = mix_fn(refs[:n_tok], refs[n_tok:n_tok + n_fixed])
    m_ref, x_ref, wy_ref, wm_ref, g_ref, b_ref, o_ref = refs[n_tok + n_fixed:]
    out = (jnp.dot(mix.astype(BF16), wy_ref[...], preferred_element_type=F32)
           + jnp.dot(m_ref[...].astype(BF16), wm_ref[...], preferred_element_type=F32))
    z = DEEPNORM_ALPHA * x_ref[...] + out
    mu = jnp.mean(z, axis=-1, keepdims=True)
    zc = z - mu
    var = jnp.mean(zc * zc, axis=-1, keepdims=True)
    o_ref[...] = zc * lax.rsqrt(var + LN_EPS) * g_ref[...] + b_ref[...]


def _layer_tail(mix_fn, tok_arrays, fixed_arrays, m, x, w_out, ln_g, ln_b, tm, first_block=None):
    rows, d = x.shape
    tok = lambda i: (i, 0)
    fixed = lambda i: (0, 0)
    first = tok if first_block is None else (lambda i: (first_block(i), 0))
    tok_maps = [first] + [tok] * (len(tok_arrays) - 1)
    return pl.pallas_call(
        functools.partial(_layer_tail_kernel, n_tok=len(tok_arrays), n_fixed=len(fixed_arrays), mix_fn=mix_fn),
        grid=(rows // tm,),
        in_specs=[pl.BlockSpec((tm, a.shape[1]), tm_map) for a, tm_map in zip(tok_arrays, tok_maps)]
        + [pl.BlockSpec(a.shape, fixed) for a in fixed_arrays]
        + [pl.BlockSpec((tm, MEM_WIDTH), tok), pl.BlockSpec((tm, d), tok),
           pl.BlockSpec((MIX_WIDTH, d), fixed), pl.BlockSpec((MEM_WIDTH, d), fixed),
           pl.BlockSpec((1, d), fixed), pl.BlockSpec((1, d), fixed)],
        out_specs=pl.BlockSpec((tm, d), tok),
        out_shape=jax.ShapeDtypeStruct((rows, d), F32),
        compiler_params=_params(),
        name="layer_tail",
    )(*tok_arrays, *fixed_arrays, m, x, w_out[:MIX_WIDTH], w_out[MIX_WIDTH:], ln_g, ln_b)


def _dil_prompt_kernel(q_ref, k_ref, v_ref, o_ref, lse_ref, *, seq, dil, group):
    n = WINDOW_KEYS
    n_blocks = seq // dil // n
    heads = LANES // HEAD_DIM
    first_head = group * HEADS_PER_GROUP
    second_pair = pl.program_id(1) == 1
    slopes = [jnp.where(second_pair, _slope(first_head + heads + h), _slope(first_head + h))
              for h in range(heads)]
    ku = lax.broadcasted_iota(jnp.int32, (2 * n, n), 0)
    qi = lax.broadcasted_iota(jnp.int32, (2 * n, n), 1)
    delta2 = qi + n - ku
    valid2 = jnp.where(delta2 >= 0, delta2, n + 1) <= n
    delta1 = delta2[n:, :]
    valid1 = delta1 >= 0
    bias2 = [s * (delta2 * dil).astype(F32) for s in slopes]
    bias1 = [s * (delta1 * dil).astype(F32) for s in slopes]
    lane_head = lax.broadcasted_iota(jnp.int32, (1, LANES), 1) // HEAD_DIM
    head_lanes = [(lane_head == h).astype(F32) for h in range(heads)]
    tn_dims = (((0,), (0,)), ((), ()))

    def stream_rows(start):
        return pl.ds(start, n, stride=dil) if dil > 1 else pl.ds(start, n)

    def block(start, has_prev):
        cur = stream_rows(start)
        q = q_ref[0, cur, :]
        k = k_ref[0, cur, :]
        v = v_ref[0, cur, :]
        if has_prev:
            prev = stream_rows(start - n * dil)
            k = jnp.concatenate([k_ref[0, prev, :], k], axis=0)
            v = jnp.concatenate([v_ref[0, prev, :], v], axis=0)
            bias, valid = bias2, valid2
        else:
            bias, valid = bias1, valid1
        q = q * SCALE
        o = jnp.zeros((n, LANES), F32)
        lse_b = jnp.zeros((n, LANES), F32)
        for h in range(heads):
            lo, hi = h * HEAD_DIM, (h + 1) * HEAD_DIM
            s = lax.dot_general(k[:, lo:hi].astype(BF16), q[:, lo:hi].astype(BF16),
                                NT_DIMS, preferred_element_type=F32)
            s = jnp.where(valid, s - bias[h], NEG_INF)
            m = jnp.max(s, axis=0, keepdims=True)
            p = jnp.exp(s - m)
            l = jnp.sum(p, axis=0, keepdims=True)
            p = (p * (1.0 / l)).astype(BF16)
            o = o + lax.dot_general(p, (v * head_lanes[h]).astype(BF16), tn_dims, preferred_element_type=F32)
            lse = m + jnp.log(l)
            t1 = lse.astype(BF16)
            r1 = lse - t1.astype(F32)
            t2 = r1.astype(BF16)
            t3 = (r1 - t2.astype(F32)).astype(BF16)
            terms = jnp.concatenate([t1, t2, t3, jnp.zeros((SUBLANES - 3, n), BF16)], axis=0)
            spread = jnp.broadcast_to(head_lanes[h], (SUBLANES, LANES)).astype(BF16)
            lse_b = lse_b + lax.dot_general(terms, spread, tn_dims, preferred_element_type=F32)
        o_ref[0, cur, :] = o
        lse_ref[0, cur, :] = lse_b

    for r in range(dil):
        for jb in range(n_blocks):
            block(r + jb * n * dil, jb > 0)


def _dil_prompt(q, kv, group):
    b, seq, _ = q.shape
    _, dil = DIL_GROUPS[group]
    pairs = GROUP_WIDTH // LANES
    k_col = group * pairs
    v_col = MIX_WIDTH // LANES + k_col
    blk = (1, seq, LANES)
    return pl.pallas_call(
        functools.partial(_dil_prompt_kernel, seq=seq, dil=dil, group=group),
        grid=(b, pairs),
        in_specs=[pl.BlockSpec(blk, lambda i, j: (i, 0, k_col + j)),
                  pl.BlockSpec(blk, lambda i, j: (i, 0, k_col + j)),
                  pl.BlockSpec(blk, lambda i, j: (i, 0, v_col + j))],
        out_specs=[pl.BlockSpec(blk, lambda i, j: (i, 0, j)), pl.BlockSpec(blk, lambda i, j: (i, 0, j))],
        out_shape=[jax.ShapeDtypeStruct((b, seq, GROUP_WIDTH), F32)] * 2,
        compiler_params=_params("parallel", "parallel"),
        name=f"dil_prompt_g{group}",
    )(q, kv, kv)


def _dil_sample_kernel(q_ref, kvn_ref, gate_ref, c0_ref, c1_ref, c2_ref,
                       y_ref, n0_ref, n1_ref, n2_ref, *, steps):
    rows = q_ref.shape[1]
    lead = rows - steps
    q_all = q_ref[0]
    kvn = kvn_ref[0]
    hg = HEADS_PER_GROUP
    stacked = lax.broadcasted_iota(jnp.int32, (hg * rows, 1), 0)
    row_head = stacked // rows
    tok = stacked - row_head * rows - lead
    kvn_t = jnp.concatenate([jnp.zeros((LANES - rows, 2 * MIX_WIDTH), F32), kvn], axis=0).T
    tail_lane = lax.broadcasted_iota(jnp.int32, (HEAD_DIM, LANES), 1) >= LANES - steps
    outs, lses = [], []
    for g, ((win, dil), c_ref, n_ref) in enumerate(zip(DIL_GROUPS, (c0_ref, c1_ref, c2_ref),
                                                       (n0_ref, n1_ref, n2_ref))):
        cols = slice(g * GROUP_WIDTH, (g + 1) * GROUP_WIDTH)
        k_new = kvn[:, cols]
        v_new = kvn[:, MIX_WIDTH + g * GROUP_WIDTH:MIX_WIDTH + (g + 1) * GROUP_WIDTH]
        kv_heads = [[c_ref[0, kv_i, h] for h in range(hg)] for kv_i in range(2)]
        for kv_i in range(2):
            for h in range(hg):
                shifted = pltpu.roll(kv_heads[kv_i][h], win - steps, axis=1)
                lo = kv_i * MIX_WIDTH + g * GROUP_WIDTH + h * HEAD_DIM
                new_t = kvn_t[lo:lo + HEAD_DIM, :]
                if win > LANES:
                    n_ref[0, kv_i, h, :, 0:win - LANES] = shifted[:, 0:win - LANES]
                n_ref[0, kv_i, h, :, win - LANES:win] = jnp.where(tail_lane, new_t, shifted[:, win - LANES:win])
        kt = jnp.concatenate(kv_heads[0], axis=0).astype(BF16)
        vt = jnp.concatenate(kv_heads[1], axis=0).astype(BF16)
        q_bd, head_mask = _head_rows(q_all[:, cols], hg)
        slope = jnp.zeros((hg * rows, 1), F32)
        for h in range(hg):
            slope = jnp.where(row_head == h, _slope(g * hg + h), slope)
        pos = lax.broadcasted_iota(jnp.int32, (hg * rows, win), 1)
        dist_buf = ((win + tok) - pos).astype(F32)
        valid_buf = jnp.where(pos >= tok, (pos - tok) & (dil - 1), 1) == 0
        s_buf = jnp.dot(q_bd.astype(BF16), kt, preferred_element_type=F32) * SCALE - slope * dist_buf
        s_buf = jnp.where(valid_buf, s_buf, NEG_INF)
        s_new = []
        for t2 in range(steps):
            col = jnp.sum(q_bd * k_new[lead + t2:lead + t2 + 1, :], axis=-1, keepdims=True) * SCALE \
                - slope * (tok - t2).astype(F32)
            ok = jnp.where(tok >= t2, (tok - t2) & (dil - 1), 1) == 0
            s_new.append(jnp.where(ok, col, NEG_INF))
        m = jnp.max(s_buf, axis=-1, keepdims=True)
        for col in s_new:
            m = jnp.maximum(m, col)
        p_buf = jnp.exp(s_buf - m)
        l = jnp.sum(p_buf, axis=-1, keepdims=True)
        o = lax.dot_general(p_buf.astype(BF16), vt, NT_DIMS, preferred_element_type=F32)
        for t2, col in enumerate(s_new):
            p_col = jnp.exp(col - m)
            l = l + p_col
            o = o + p_col * v_new[lead + t2:lead + t2 + 1, :]
        outs.append(o / l)
        lses.append(m + jnp.log(l))
    mx = jnp.maximum(jnp.maximum(lses[0], lses[1]), lses[2])
    es = [jnp.exp(l - mx) for l in lses]
    den = es[0] + es[1] + es[2]
    mixed = [_head_cols(outs[g] * (es[g] / den), head_mask, hg) for g in range(len(DIL_GROUPS))]
    y_ref[0] = jnp.concatenate(mixed, axis=1) * _silu(gate_ref[0])


def _dil_sample(q, kv_new, gate, caches_t):
    b, steps, _ = q.shape
    rows = _round_up(steps, SUBLANES)
    pad = ((0, 0), (rows - steps, 0), (0, 0))
    q, kv_new, gate = jnp.pad(q, pad), jnp.pad(kv_new, pad), jnp.pad(gate, pad)
    tok = lambda i: (i, 0, 0)
    cache_specs = [pl.BlockSpec((1,) + c.shape[1:], lambda i: (i, 0, 0, 0, 0)) for c in caches_t]
    res = pl.pallas_call(
        functools.partial(_dil_sample_kernel, steps=steps),
        grid=(b,),
        in_specs=[pl.BlockSpec((1, rows, MIX_WIDTH), tok), pl.BlockSpec((1, rows, 2 * MIX_WIDTH), tok),
                  pl.BlockSpec((1, rows, MIX_WIDTH), tok)] + cache_specs,
        out_specs=[pl.BlockSpec((1, rows, MIX_WIDTH), tok)] + cache_specs,
        out_shape=[jax.ShapeDtypeStruct((b, rows, MIX_WIDTH), F32)]
        + [jax.ShapeDtypeStruct(c.shape, F32) for c in caches_t],
        compiler_params=_params(),
        name="dil_sample",
    )(q, kv_new, gate, *caches_t)
    return [res[0][:, rows - steps:]] + list(res[1:])


def _trunk(x, batch, seq, mem_kv_t, h0_re, h0_im, weights, tm, s5_tk, s5_chunk, mem_block, mem_tq,
           side_dtype, kv_proj, dil_attn):
    (w_in, w_out, ln_g, ln_b, tables, w_glu, b_glu) = weights
    splits = (MIX_WIDTH, MIX_WIDTH, MEM_WIDTH, MEM_WIDTH)
    seq_mem = _round_up(seq, SUBLANES)

    def mem_branch(mq, mgate, layer):
        mq = mq.reshape(batch, seq, MEM_WIDTH)
        mgate = mgate.reshape(batch, seq, MEM_WIDTH)
        if seq_mem != seq:
            pad = ((0, 0), (0, seq_mem - seq), (0, 0))
            mq, mgate = jnp.pad(mq, pad), jnp.pad(mgate, pad)
        m = _mem_attn(mq, mem_kv_t, layer, mgate, mem_block, min(mem_tq, seq_mem))
        return m[:, :seq].reshape(batch * seq, MEM_WIDTH)

    if s5_tk == seq:
        block_of = None
    else:
        assert tm == s5_tk
        n_blocks = seq // s5_tk
        block_of = lambda i: (i % n_blocks) * batch + i // n_blocks
    u, gate, mq, mgate = _proj(x, w_in[0], splits, tm, first_out_block=block_of, rest_dtype=side_dtype)
    y, h_re, h_im = _s5_mixer(u, h0_re, h0_im, tables, batch, seq, s5_tk, s5_chunk)
    m = mem_branch(mq, mgate, 0)
    x = _layer_tail(_mix_glu, (y, gate), (w_glu, b_glu), m, x, w_out[0], ln_g[0:1], ln_b[0:1], tm,
                    first_block=block_of)
    kv, kv_extras = kv_proj(x)

    u, gate, mq, mgate = _proj(x, w_in[1], splits, tm, rest_dtype=side_dtype)
    mix_fn, mix_inputs, attn_extras = dil_attn(u, kv, gate)
    m = mem_branch(mq, mgate, 1)
    x = _layer_tail(mix_fn, mix_inputs, (), m, x, w_out[1], ln_g[1:2], ln_b[1:2], tm)
    return x, h_re, h_im, kv, kv_extras, attn_extras


def kernel(x_prompt, x_sample, cache_mem_kv, state_ssm_re, state_ssm_im, cache_dil1_kv, cache_dil4_kv,
           cache_dil16_kv, mem_prompt, w_in, w_out, ln_g, ln_b, w_mem_kv, ssm_lambda_re, ssm_lambda_im,
           ssm_log_dt, ssm_b_re, ssm_b_im, ssm_c_re, ssm_c_im, ssm_d, w_glu, b_glu, w_kv_shared):
    bp, seq, d = x_prompt.shape
    bs, steps, _ = x_sample.shape
    prompt_chunk, prompt_tk = 8, 512
    n_groups = len(DIL_GROUPS)

    w_in_b = w_in.astype(BF16)
    w_out_b = w_out.astype(BF16)
    w_glu_b = w_glu[0].astype(BF16)
    w_kv_b = w_kv_shared.astype(BF16)
    ssm = (ssm_lambda_re[0], ssm_lambda_im[0], ssm_log_dt[0], ssm_b_re[0], ssm_b_im[0],
           ssm_c_re[0], ssm_c_im[0], ssm_d[0])

    assert prompt_chunk % steps == 0
    weights = (w_in_b, w_out_b, ln_g, ln_b, _s5_tables(prompt_chunk, *ssm), w_glu_b, b_glu)

    mem_kv_p = _mem_kv_t(w_mem_kv.transpose(0, 2, 1).astype(BF16), mem_prompt)
    mem_kv_p = mem_kv_p.reshape(DEPTH, bp, 2, MEM_HEADS, HEAD_DIM, N_MEM)
    zeros = jnp.zeros((bp, SSM_GROUPS, SSM_STATE), F32)
    wide = n_groups - 1
    assert DIL_GROUPS[wide][0] >= seq and all(w <= prompt_tk for w, _ in DIL_GROUPS[:wide])

    def group_cols_t(g):
        return jnp.concatenate([w_kv_b[:, kv_i * MIX_WIDTH + g * GROUP_WIDTH:
                                       kv_i * MIX_WIDTH + (g + 1) * GROUP_WIDTH] for kv_i in range(2)], axis=1).T

    wt_all = group_cols_t(wide)
    wt_tail = jnp.concatenate([group_cols_t(g) for g in range(wide)], axis=0)

    def kv_prompt(x):
        kv, kvt_all, kvt_tail = _kv_proj(x, w_kv_b, wt_all, wt_tail, bp, seq, prompt_tk)
        return kv, (kvt_all, kvt_tail)

    def attn_prompt(u, kv, gate):
        q3 = u.reshape(bp, seq, MIX_WIDTH)
        kv3 = kv.reshape(bp, seq, 2 * MIX_WIDTH)
        res = [_dil_prompt(q3, kv3, g) for g in range(n_groups)]
        outs = [r[0].reshape(bp * seq, GROUP_WIDTH) for r in res]
        lses = [r[1].reshape(bp * seq, GROUP_WIDTH) for r in res]
        return _mix_merge, (*outs, *lses, gate), None

    y_p, hre_p, him_p, _, (kvt_all, kvt_tail), _ = _trunk(
        x_prompt.reshape(bp * seq, d), bp, seq, mem_kv_p, zeros, zeros, weights,
        prompt_tk, prompt_tk, prompt_chunk, 1, 512, BF16, kv_prompt, attn_prompt)
    kvt_tail = kvt_tail.reshape(bp, wide, 2, HEADS_PER_GROUP, HEAD_DIM, prompt_tk)
    win_p = [kvt_tail[:, g, :, :, :, prompt_tk - win:].transpose(0, 4, 1, 2, 3)
             for g, (win, _) in enumerate(DIL_GROUPS[:wide])]
    win_p.append(kvt_all.reshape(bp, 2, HEADS_PER_GROUP, HEAD_DIM, seq).transpose(0, 4, 1, 2, 3))

    caches = (cache_dil1_kv, cache_dil4_kv, cache_dil16_kv)
    caches_t = [c.transpose(0, 2, 3, 4, 1) for c in caches]
    mem_kv_s = cache_mem_kv.transpose(0, 1, 3, 4, 5, 2)

    def kv_sample(x):
        return _proj(x, w_kv_b, (2 * MIX_WIDTH,), 256)[0], None

    def attn_sample(u, kv, gate):
        res = _dil_sample(u.reshape(bs, steps, MIX_WIDTH), kv.reshape(bs, steps, 2 * MIX_WIDTH),
                          gate.reshape(bs, steps, MIX_WIDTH), caches_t)
        return _mix_plain, (res[0].reshape(bs * steps, MIX_WIDTH),), res[1:]

    y_s, hre_s, him_s, _, _, rolled = _trunk(
        x_sample.reshape(bs * steps, d), bs, steps, mem_kv_s, state_ssm_re[0], state_ssm_im[0],
        weights, 256, steps, steps, 8, 8, F32, kv_sample, attn_sample)
    win_s = [c.transpose(0, 4, 1, 2, 3) for c in rolled]

    mem_kv_out = mem_kv_p.transpose(0, 1, 5, 2, 3, 4)
    return (y_p.reshape(bp, seq, d), y_s.reshape(bs, steps, d), mem_kv_out,
            hre_p[None], him_p[None], win_p[0], win_p[1], win_p[2],
            hre_s[None], him_s[None], win_s[0], win_s[1], win_s[2])
```

```python
import functools

import jax
import jax.numpy as jnp
from jax import lax
from jax.experimental import pallas as pl
from jax.experimental.pallas import tpu as pltpu

F32 = jnp.float32
BF16 = jnp.bfloat16

D_MODEL = 1024
HEAD_DIM = 64
MIX_WIDTH = 768
MEM_WIDTH = 256
MEM_HEADS = 4
N_MEM = 256
SSM_GROUP = 16
SSM_GROUPS = 48
SSM_STATE = 64
DIL_GROUPS = ((128, 1), (512, 4), (2048, 16))
ATT_HEADS = 12
HEADS_PER_GROUP = 4
GROUP_WIDTH = HEADS_PER_GROUP * HEAD_DIM
DEPTH = 2
DEEPNORM_ALPHA = (2.0 * DEPTH) ** 0.25
LN_EPS = 1e-5
SCALE = HEAD_DIM ** -0.5
NEG_INF = -1e30
WINDOW_KEYS = 128
LANES = 128
SUBLANES = 8
SLAB_GROUPS = LANES // SSM_GROUP
SSM_SLABS = SSM_GROUPS // SLAB_GROUPS
SLAB_STATE = SLAB_GROUPS * SSM_STATE
VMEM_LIMIT = 56 * 1024 * 1024
NT_DIMS = (((1,), (1,)), ((), ()))


def _slope(head):
    return 2.0 ** (-8.0 * (head + 1) / ATT_HEADS)


def _params(*semantics):
    return pltpu.CompilerParams(dimension_semantics=semantics or ("parallel",),
                                vmem_limit_bytes=VMEM_LIMIT)


def _silu(x):
    return x * jax.nn.sigmoid(x)


def _round_up(n, m):
    return -(-n // m) * m


def _proj_kernel(x_ref, w_ref, *out_refs, splits):
    x = x_ref[...].astype(BF16)
    off = 0
    for o_ref, n in zip(out_refs, splits):
        o_ref[...] = jnp.dot(x, w_ref[:, off:off + n], preferred_element_type=F32).astype(o_ref.dtype)
        off += n


def _proj(x, w, splits, tm, first_out_block=None, rest_dtype=F32):
    m, k = x.shape
    n_total = w.shape[1]
    assert sum(splits) == n_total and m % tm == 0
    tok = lambda i: (i, 0)
    first = tok if first_out_block is None else (lambda i: (first_out_block(i), 0))
    out_maps = [first] + [tok] * (len(splits) - 1)
    return pl.pallas_call(
        functools.partial(_proj_kernel, splits=splits),
        grid=(m // tm,),
        in_specs=[pl.BlockSpec((tm, k), tok), pl.BlockSpec((k, n_total), lambda i: (0, 0))],
        out_specs=[pl.BlockSpec((tm, n), om) for n, om in zip(splits, out_maps)],
        out_shape=[jax.ShapeDtypeStruct((m, n), F32 if i == 0 else rest_dtype) for i, n in enumerate(splits)],
        compiler_params=_params(),
        name="proj",
    )(x, w)


def _kv_proj_kernel(x_ref, w_kv_ref, w_in_ref, kv_ref, kvt_all_ref, kvt_tail_ref, *proj_refs,
                    tiles, wide, splits):
    x = x_ref[...].astype(BF16)
    kv = jnp.dot(x, w_kv_ref[...], preferred_element_type=F32)
    kv_ref[...] = kv

    def group_t(g):
        cols = [kv[:, i * MIX_WIDTH + g * GROUP_WIDTH:i * MIX_WIDTH + (g + 1) * GROUP_WIDTH] for i in range(2)]
        return jnp.concatenate(cols, axis=1).T

    kvt_all_ref[0] = group_t(wide)

    @pl.when(pl.program_id(0) % tiles == tiles - 1)
    def _():
        for g in range(wide):
            kvt_tail_ref[0, g * 2 * GROUP_WIDTH:(g + 1) * 2 * GROUP_WIDTH, :] = group_t(g)

    off = 0
    for o_ref, n in zip(proj_refs, splits):
        o_ref[...] = jnp.dot(x, w_in_ref[:, off:off + n], preferred_element_type=F32).astype(o_ref.dtype)
        off += n


def _kv_proj(x, w_kv, w_in, splits, rest_dtype, batch, seq, tm, wide):
    m, k = x.shape
    n_kv, n_in = w_kv.shape[1], w_in.shape[1]
    group_rows = 2 * GROUP_WIDTH
    tiles = seq // tm
    tok = lambda i: (i, 0)
    fixed = lambda i: (0, 0)
    res = pl.pallas_call(
        functools.partial(_kv_proj_kernel, tiles=tiles, wide=wide, splits=splits),
        grid=(m // tm,),
        in_specs=[pl.BlockSpec((tm, k), tok), pl.BlockSpec((k, n_kv), fixed), pl.BlockSpec((k, n_in), fixed)],
        out_specs=[pl.BlockSpec((tm, n_kv), tok),
                   pl.BlockSpec((1, group_rows, tm), lambda i: (i // tiles, 0, i % tiles)),
                   pl.BlockSpec((1, wide * group_rows, tm), lambda i: (i // tiles, 0, 0))]
        + [pl.BlockSpec((tm, n), tok) for n in splits],
        out_shape=[jax.ShapeDtypeStruct((m, n_kv), F32),
                   jax.ShapeDtypeStruct((batch, group_rows, seq), F32),
                   jax.ShapeDtypeStruct((batch, wide * group_rows, tm), F32)]
        + [jax.ShapeDtypeStruct((m, n), F32 if i == 0 else rest_dtype) for i, n in enumerate(splits)],
        compiler_params=_params("arbitrary"),
        name="kv_proj",
    )(x, w_kv, w_in)
    return res[0], res[1], res[2], res[3:]


def _mem_kv_t_kernel(w_ref, mem_ref, o_ref):
    o_ref[0, 0] = lax.dot_general(w_ref[0], mem_ref[0].astype(BF16), NT_DIMS, preferred_element_type=F32)


def _mem_kv_t(w_t, mem):
    depth, n, d = w_t.shape
    b = mem.shape[0]
    return pl.pallas_call(
        _mem_kv_t_kernel,
        grid=(depth, b),
        in_specs=[pl.BlockSpec((1, n, d), lambda l, i: (l, 0, 0)),
                  pl.BlockSpec((1, N_MEM, d), lambda l, i: (i, 0, 0))],
        out_specs=pl.BlockSpec((1, 1, n, N_MEM), lambda l, i: (l, i, 0, 0)),
        out_shape=jax.ShapeDtypeStruct((depth, b, n, N_MEM), F32),
        compiler_params=_params("parallel", "parallel"),
        name="mem_kv_t",
    )(w_t, mem)


def _s5_tables(chunk, lam_re, lam_im, log_dt, b_re, b_im, c_re, c_im, d_skip):
    hp = lax.Precision.HIGHEST
    p, c = SSM_STATE, SSM_GROUP
    ns, gs = SSM_SLABS, SLAB_GROUPS
    kl = chunk * LANES
    lr = jnp.minimum(lam_re.astype(F32), -1e-4)
    li = lam_im.astype(F32)
    dt = jnp.exp(log_dt.astype(F32))[:, None]
    def lam_bar_pow(n):
        n = n.astype(F32)[None, :, None]
        mag = jnp.exp(lr[:, None, :] * dt[:, None, :] * n)
        ang = li[:, None, :] * dt[:, None, :] * n
        return mag * jnp.cos(ang), mag * jnp.sin(ang)

    pw_re, pw_im = lam_bar_pow(jnp.arange(chunk + 1))
    nr, ni = pw_re[:, 1] - 1.0, pw_im[:, 1]
    den = lr * lr + li * li
    f_re, f_im = (nr * lr + ni * li) / den, (ni * lr - nr * li) / den
    bb_re = f_re[:, :, None] * b_re - f_im[:, :, None] * b_im
    bb_im = f_re[:, :, None] * b_im + f_im[:, :, None] * b_re
    c_re, c_im = c_re.astype(F32), c_im.astype(F32)

    x_re = pw_re[:, :chunk, :, None] * bb_re[:, None] - pw_im[:, :chunk, :, None] * bb_im[:, None]
    x_im = pw_re[:, :chunk, :, None] * bb_im[:, None] + pw_im[:, :chunk, :, None] * bb_re[:, None]
    conv = (jnp.einsum('gcp,gtpd->gtcd', c_re, x_re, precision=hp)
            - jnp.einsum('gcp,gtpd->gtcd', c_im, x_im, precision=hp))
    def slab_diag(x):
        r, w = x.shape[-2:]
        x = jnp.tile(x, (1,) * (x.ndim - 1) + (gs,))
        own = (jnp.arange(gs * w) // w)[None, :] == jnp.arange(gs)[:, None]
        x = jnp.where(own.reshape((1, gs) + (1,) * (x.ndim - 3) + (gs * w,)), x, 0.0)
        x = jnp.moveaxis(x, 1, -3)
        return x.reshape(x.shape[:-3] + (gs * r, gs * w))

    conv_d = slab_diag(conv.transpose(0, 1, 3, 2).reshape(ns, gs, chunk, c, c))
    b_diag = lambda x: slab_diag(x.transpose(0, 2, 1).reshape(ns, gs, c, p))
    c_diag = lambda x: slab_diag(x.transpose(0, 2, 1).reshape(ns, gs, p, c))

    def slab_rows(x):
        n = x.shape[1]
        return x.reshape(ns, gs, n, p).transpose(0, 2, 1, 3).reshape(ns, n, SLAB_STATE)

    strip = conv_d.transpose(0, 2, 1, 3).reshape(ns, LANES, kl)
    toep = jnp.stack([jnp.pad(strip[:, :, :kl - s * LANES], ((0, 0), (0, 0), (s * LANES, 0)))
                      for s in range(chunk)], axis=1).reshape(ns, kl, kl)

    rev_re, rev_im = lam_bar_pow(chunk - 1 - jnp.arange(chunk))
    rev_re, rev_im = slab_rows(rev_re), slab_rows(rev_im)
    bd_re, bd_im = b_diag(bb_re), b_diag(bb_im)
    rev_a = jnp.concatenate([rev_re, rev_re], axis=-1)[:, :, None, :]
    rev_b = jnp.concatenate([-rev_im, rev_im], axis=-1)[:, :, None, :]
    bd_a = jnp.concatenate([bd_re, bd_im], axis=-1)[:, None]
    bd_b = jnp.concatenate([bd_im, bd_re], axis=-1)[:, None]
    w_in = (rev_a * bd_a + rev_b * bd_b).reshape(ns, kl, 2 * SLAB_STATE)

    col_re = slab_rows(pw_re[:, 1:]).transpose(0, 2, 1)
    col_im = slab_rows(pw_im[:, 1:]).transpose(0, 2, 1)
    cd_re, cd_im = c_diag(c_re), c_diag(c_im)
    col_a = jnp.concatenate([col_re, -col_re], axis=1)
    col_b = jnp.concatenate([-col_im, -col_im], axis=1)
    cd_a = jnp.concatenate([cd_re, cd_im], axis=1)
    cd_b = jnp.concatenate([cd_im, cd_re], axis=1)
    w_out = jnp.concatenate([col_a[:, :, t:t + 1] * cd_a + col_b[:, :, t:t + 1] * cd_b
                             for t in range(chunk)], axis=-1)

    lam_pows = jnp.stack([slab_rows(pw_re).transpose(1, 0, 2),
                          slab_rows(pw_im).transpose(1, 0, 2)], axis=2)
    skip = jnp.tile(d_skip.astype(F32).reshape(ns, LANES), (1, chunk))[:, None]
    return toep.astype(BF16), w_in.astype(BF16), w_out.astype(BF16), lam_pows, skip


def _s5_kernel(u_ref, toep_ref, win_ref, wout_ref, lam_ref, skip_ref, h0_ref, y_ref, hout_ref,
               a_scr, g_scr, hs_scr, y_scr, h_scr, *, batch, tk, chunk):
    n_chunks = tk // chunk
    ns = SLAB_STATE

    @pl.when(pl.program_id(1) == 0)
    def _():
        h_scr[...] = h0_ref[0]

    tile_swap = batch == SUBLANES and chunk == SUBLANES

    def gather(kk, carry):
        rows = pl.ds(pl.multiple_of(kk * batch, SUBLANES), batch)
        if tile_swap:
            tiles = [u_ref[pl.ds(pl.multiple_of(b * tk + kk * chunk, SUBLANES), chunk), :] for b in range(batch)]
            by_token = pltpu.einshape("btl->tbl", jnp.stack(tiles, axis=0))
            for t in range(chunk):
                a_scr[rows, t * LANES:(t + 1) * LANES] = by_token[t]
            return carry
        for t in range(chunk):
            a_scr[rows, t * LANES:(t + 1) * LANES] = u_ref[pl.ds(kk * chunk + t, batch, stride=tk), :]
        return carry

    lax.fori_loop(0, n_chunks, gather, 0)
    a = a_scr[...].astype(BF16)
    g_scr[...] = jnp.dot(a, win_ref[0], preferred_element_type=F32)
    ar = lam_ref[0, 0:1, :]
    ai = lam_ref[0, 1:2, :]

    def step(kk, carry):
        hr, hi = carry
        rows = pl.ds(pl.multiple_of(kk * batch, SUBLANES), batch)
        hs_scr[rows, 0:ns] = hr
        hs_scr[rows, ns:2 * ns] = hi
        return (ar * hr - ai * hi + g_scr[rows, 0:ns], ar * hi + ai * hr + g_scr[rows, ns:2 * ns])

    hr, hi = lax.fori_loop(0, n_chunks, step, (h_scr[:, 0:ns], h_scr[:, ns:2 * ns]))
    h_scr[:, 0:ns] = hr
    h_scr[:, ns:2 * ns] = hi
    hout_ref[0, :, 0:ns] = hr
    hout_ref[0, :, ns:2 * ns] = hi
    y = (jnp.dot(a, toep_ref[0], preferred_element_type=F32)
         + jnp.dot(hs_scr[...].astype(BF16), wout_ref[0], preferred_element_type=F32)
         + skip_ref[0] * a_scr[...])
    y_scr[...] = jax.nn.gelu(y)

    def scatter(kk, carry):
        rows = pl.ds(pl.multiple_of(kk * batch, SUBLANES), batch)
        if tile_swap:
            by_token = jnp.stack([y_scr[rows, t * LANES:(t + 1) * LANES] for t in range(chunk)], axis=0)
            by_seq = pltpu.einshape("tbl->btl", by_token)
            for b in range(batch):
                y_ref[pl.ds(pl.multiple_of(b * tk + kk * chunk, SUBLANES), chunk), :] = by_seq[b]
            return carry
        for t in range(chunk):
            y_ref[pl.ds(kk * chunk + t, batch, stride=tk), :] = y_scr[rows, t * LANES:(t + 1) * LANES]
        return carry

    lax.fori_loop(0, n_chunks, scatter, 0)


def _s5_mixer(u, h0_re, h0_im, tables, batch, seq, tk, chunk):
    toep, w_in, w_out, lam_pows, skip = tables
    kl = chunk * LANES
    table_chunk = toep.shape[-1] // LANES
    assert (table_chunk - chunk) % chunk == 0
    w_in_block = (table_chunk - chunk) // chunk
    lam_l = lam_pows[chunk]
    n_blocks = seq // tk
    rows = (tk // chunk) * batch
    to_slabs = lambda h: h.reshape(batch, SSM_SLABS, SLAB_STATE).transpose(1, 0, 2)
    h0 = jnp.concatenate([to_slabs(h0_re), to_slabs(h0_im)], axis=-1)
    slab = lambda j, k: (j, 0, 0)
    y, h = pl.pallas_call(
        functools.partial(_s5_kernel, batch=batch, tk=tk, chunk=chunk),
        grid=(SSM_SLABS, n_blocks),
        in_specs=[pl.BlockSpec((batch * tk, LANES), lambda j, k: (k, j)),
                  pl.BlockSpec((1, kl, kl), slab),
                  pl.BlockSpec((1, kl, 2 * SLAB_STATE), lambda j, k: (j, w_in_block, 0)),
                  pl.BlockSpec((1, 2 * SLAB_STATE, kl), slab),
                  pl.BlockSpec((1, 2, SLAB_STATE), slab),
                  pl.BlockSpec((1, 1, kl), slab),
                  pl.BlockSpec((1, batch, 2 * SLAB_STATE), slab)],
        out_specs=[pl.BlockSpec((batch * tk, LANES), lambda j, k: (k, j)),
                   pl.BlockSpec((1, batch, 2 * SLAB_STATE), slab)],
        out_shape=[jax.ShapeDtypeStruct(u.shape, F32),
                   jax.ShapeDtypeStruct((SSM_SLABS, batch, 2 * SLAB_STATE), F32)],
        scratch_shapes=[pltpu.VMEM((rows, kl), F32), pltpu.VMEM((rows, 2 * SLAB_STATE), F32),
                        pltpu.VMEM((rows, 2 * SLAB_STATE), F32), pltpu.VMEM((rows, kl), F32),
                        pltpu.VMEM((batch, 2 * SLAB_STATE), F32)],
        compiler_params=_params("parallel", "arbitrary"),
        name="s5_chunks",
    )(u, toep, w_in, w_out, lam_l, skip, h0)
    from_slabs = lambda x: x.transpose(1, 0, 2).reshape(batch, SSM_GROUPS, SSM_STATE)
    return y, from_slabs(h[:, :, :SLAB_STATE]), from_slabs(h[:, :, SLAB_STATE:])


def _head_rows(x, heads):
    t, w = x.shape
    row_head = lax.broadcasted_iota(jnp.int32, (heads * t, w), 0) // t
    lane_head = lax.broadcasted_iota(jnp.int32, (heads * t, w), 1) // HEAD_DIM
    mask = row_head == lane_head
    return jnp.where(mask, jnp.concatenate([x] * heads, axis=0), 0.0), mask


def _head_cols(x, mask, heads):
    t = x.shape[0] // heads
    x = jnp.where(mask, x, 0.0)
    out = x[0:t]
    for h in range(1, heads):
        out = out + x[h * t:(h + 1) * t]
    return out


def _mem_attn_kernel(q_ref, kv_ref, mg_ref, o_ref, *, block_batch):
    for bi in range(block_batch):
        q_bd, mask = _head_rows(q_ref[bi].astype(F32), MEM_HEADS)
        kt = jnp.concatenate([kv_ref[0, bi, 0, h] for h in range(MEM_HEADS)], axis=0).astype(BF16)
        vt = jnp.concatenate([kv_ref[0, bi, 1, h] for h in range(MEM_HEADS)], axis=0).astype(BF16)
        s = jnp.dot(q_bd.astype(BF16), kt, preferred_element_type=F32) * SCALE
        p = jnp.exp(s - jnp.max(s, axis=-1, keepdims=True))
        l = jnp.sum(p, axis=-1, keepdims=True)
        o = lax.dot_general(p.astype(BF16), vt, NT_DIMS, preferred_element_type=F32) / l
        o_ref[bi] = (_head_cols(o, mask, MEM_HEADS) * _silu(mg_ref[bi].astype(F32))).astype(o_ref.dtype)


def _mem_attn(q, kv_t, layer, mgate, block_batch, tq):
    b, t, w = q.shape
    tok = lambda i, j: (i, j, 0)
    return pl.pallas_call(
        functools.partial(_mem_attn_kernel, block_batch=block_batch),
        grid=(b // block_batch, t // tq),
        in_specs=[pl.BlockSpec((block_batch, tq, w), tok),
                  pl.BlockSpec((1, block_batch) + kv_t.shape[2:], lambda i, j: (layer, i, 0, 0, 0, 0)),
                  pl.BlockSpec((block_batch, tq, w), tok)],
        out_specs=pl.BlockSpec((block_batch, tq, w), tok),
        out_shape=jax.ShapeDtypeStruct((b, t, w), q.dtype),
        compiler_params=_params("parallel", "parallel"),
        name="mem_attn",
    )(q, kv_t, mgate)


def _mix_plain(tok_refs, fixed_refs):
    (y_ref,) = tok_refs
    return y_ref[...]


def _mix_glu(tok_refs, fixed_refs):
    y_ref, gate_ref = tok_refs
    w_ref, b_ref = fixed_refs
    y = y_ref[...]
    z = jnp.dot(y.astype(BF16), w_ref[...], preferred_element_type=F32) + b_ref[...]
    return y * jax.nn.sigmoid(z) * _silu(gate_ref[...].astype(F32))


def _mix_merge(tok_refs, fixed_refs):
    o0, o1, o2, l0, l1, l2, gate_ref = tok_refs
    ls = [l0[...], l1[...], l2[...]]
    mx = jnp.maximum(jnp.maximum(ls[0], ls[1]), ls[2])
    es = [jnp.exp(l - mx) for l in ls]
    den = es[0] + es[1] + es[2]
    o = jnp.concatenate([o_ref[...] * (e / den) for o_ref, e in zip((o0, o1, o2), es)], axis=1)
    return o * _silu(gate_ref[...].astype(F32))


def _layer_tail_kernel(*refs, n_tok, n_fixed, mix_fn):
    mix = mix_fn(refs[:n_tok], refs[n_tok:n_tok + n_fixed])
    m_ref, x_ref, wy_ref, wm_ref, g_ref, b_ref, o_ref = refs[n_tok + n_fixed:]
    out = (jnp.dot(mix.astype(BF16), wy_ref[...], preferred_element_type=F32)
           + jnp.dot(m_ref[...].astype(BF16), wm_ref[...], preferred_element_type=F32))
    z = DEEPNORM_ALPHA * x_ref[...] + out
    mu = jnp.mean(z, axis=-1, keepdims=True)
    zc = z - mu
    var = jnp.mean(zc * zc, axis=-1, keepdims=True)
    o_ref[...] = zc * lax.rsqrt(var + LN_EPS) * g_ref[...] + b_ref[...]


def _layer_tail(mix_fn, tok_arrays, fixed_arrays, m, x, w_out, ln_g, ln_b, tm, first_block=None):
    rows, d = x.shape
    tok = lambda i: (i, 0)
    fixed = lambda i: (0, 0)
    first = tok if first_block is None else (lambda i: (first_block(i), 0))
    tok_maps = [first] + [tok] * (len(tok_arrays) - 1)
    return pl.pallas_call(
        functools.partial(_layer_tail_kernel, n_tok=len(tok_arrays), n_fixed=len(fixed_arrays), mix_fn=mix_fn),
        grid=(rows // tm,),
        in_specs=[pl.BlockSpec((tm, a.shape[1]), tm_map) for a, tm_map in zip(tok_arrays, tok_maps)]
        + [pl.BlockSpec(a.shape, fixed) for a in fixed_arrays]
        + [pl.BlockSpec((tm, MEM_WIDTH), tok), pl.BlockSpec((tm, d), tok),
           pl.BlockSpec((MIX_WIDTH, d), fixed), pl.BlockSpec((MEM_WIDTH, d), fixed),
           pl.BlockSpec((1, d), fixed), pl.BlockSpec((1, d), fixed)],
        out_specs=pl.BlockSpec((tm, d), tok),
        out_shape=jax.ShapeDtypeStruct((rows, d), F32),
        compiler_params=_params(),
        name="layer_tail",
    )(*tok_arrays, *fixed_arrays, m, x, w_out[:MIX_WIDTH], w_out[MIX_WIDTH:], ln_g, ln_b)


def _dil_prompt_kernel(q_ref, k_ref, v_ref, o_ref, lse_ref, *, seq, dil, group):
    n = WINDOW_KEYS
    n_blocks = seq // dil // n
    heads = LANES // HEAD_DIM
    first_head = group * HEADS_PER_GROUP
    second_pair = pl.program_id(1) == 1
    slopes = [jnp.where(second_pair, _slope(first_head + heads + h), _slope(first_head + h))
              for h in range(heads)]
    ku = lax.broadcasted_iota(jnp.int32, (2 * n, n), 0)
    qi = lax.broadcasted_iota(jnp.int32, (2 * n, n), 1)
    delta2 = qi + n - ku
    valid2 = jnp.where(delta2 >= 0, delta2, n + 1) <= n
    delta1 = delta2[n:, :]
    valid1 = delta1 >= 0
    bias2 = [s * (delta2 * dil).astype(F32) for s in slopes]
    bias1 = [s * (delta1 * dil).astype(F32) for s in slopes]
    lane_head = lax.broadcasted_iota(jnp.int32, (1, LANES), 1) // HEAD_DIM
    head_lanes = [(lane_head == h).astype(F32) for h in range(heads)]
    tn_dims = (((0,), (0,)), ((), ()))

    def stream_rows(start):
        return pl.ds(start, n, stride=dil) if dil > 1 else pl.ds(start, n)

    def block(start, has_prev):
        cur = stream_rows(start)
        q = q_ref[0, cur, :]
        k = k_ref[0, cur, :]
        v = v_ref[0, cur, :]
        if has_prev:
            prev = stream_rows(start - n * dil)
            k = jnp.concatenate([k_ref[0, prev, :], k], axis=0)
            v = jnp.concatenate([v_ref[0, prev, :], v], axis=0)
            bias, valid = bias2, valid2
        else:
            bias, valid = bias1, valid1
        q = q * SCALE
        o = jnp.zeros((n, LANES), F32)
        lse_b = jnp.zeros((n, LANES), F32)
        for h in range(heads):
            lo, hi = h * HEAD_DIM, (h + 1) * HEAD_DIM
            s = lax.dot_general(k[:, lo:hi].astype(BF16), q[:, lo:hi].astype(BF16),
                                NT_DIMS, preferred_element_type=F32)
            s = jnp.where(valid, s - bias[h], NEG_INF)
            m = jnp.max(s, axis=0, keepdims=True)
            p = jnp.exp(s - m)
            l = jnp.sum(p, axis=0, keepdims=True)
            p = (p * (1.0 / l)).astype(BF16)
            o = o + lax.dot_general(p, (v * head_lanes[h]).astype(BF16), tn_dims, preferred_element_type=F32)
            lse = m + jnp.log(l)
            t1 = lse.astype(BF16)
            r1 = lse - t1.astype(F32)
            t2 = r1.astype(BF16)
            t3 = (r1 - t2.astype(F32)).astype(BF16)
            terms = jnp.concatenate([t1, t2, t3, jnp.zeros((SUBLANES - 3, n), BF16)], axis=0)
            spread = jnp.broadcast_to(head_lanes[h], (SUBLANES, LANES)).astype(BF16)
            lse_b = lse_b + lax.dot_general(terms, spread, tn_dims, preferred_element_type=F32)
        o_ref[0, cur, :] = o
        lse_ref[0, cur, :] = lse_b

    for r in range(dil):
        for jb in range(n_blocks):
            block(r + jb * n * dil, jb > 0)


def _dil_prompt(q, kv, group):
    b, seq, _ = q.shape
    _, dil = DIL_GROUPS[group]
    pairs = GROUP_WIDTH // LANES
    k_col = group * pairs
    v_col = MIX_WIDTH // LANES + k_col
    blk = (1, seq, LANES)
    return pl.pallas_call(
        functools.partial(_dil_prompt_kernel, seq=seq, dil=dil, group=group),
        grid=(b, pairs),
        in_specs=[pl.BlockSpec(blk, lambda i, j: (i, 0, k_col + j)),
                  pl.BlockSpec(blk, lambda i, j: (i, 0, k_col + j)),
                  pl.BlockSpec(blk, lambda i, j: (i, 0, v_col + j))],
        out_specs=[pl.BlockSpec(blk, lambda i, j: (i, 0, j)), pl.BlockSpec(blk, lambda i, j: (i, 0, j))],
        out_shape=[jax.ShapeDtypeStruct((b, seq, GROUP_WIDTH), F32)] * 2,
        compiler_params=_params("parallel", "parallel"),
        name=f"dil_prompt_g{group}",
    )(q, kv, kv)


def _dil_sample_kernel(q_ref, kvn_ref, gate_ref, c0_ref, c1_ref, c2_ref,
                       y_ref, n0_ref, n1_ref, n2_ref, *, steps):
    rows = q_ref.shape[1]
    lead = rows - steps
    q_all = q_ref[0]
    kvn = kvn_ref[0]
    hg = HEADS_PER_GROUP
    stacked = lax.broadcasted_iota(jnp.int32, (hg * rows, 1), 0)
    row_head = stacked // rows
    tok = stacked - row_head * rows - lead
    kvn_t = jnp.concatenate([jnp.zeros((LANES - rows, 2 * MIX_WIDTH), F32), kvn], axis=0).T
    tail_lane = lax.broadcasted_iota(jnp.int32, (HEAD_DIM, LANES), 1) >= LANES - steps
    outs, lses = [], []
    for g, ((win, dil), c_ref, n_ref) in enumerate(zip(DIL_GROUPS, (c0_ref, c1_ref, c2_ref),
                                                       (n0_ref, n1_ref, n2_ref))):
        cols = slice(g * GROUP_WIDTH, (g + 1) * GROUP_WIDTH)
        k_new = kvn[:, cols]
        v_new = kvn[:, MIX_WIDTH + g * GROUP_WIDTH:MIX_WIDTH + (g + 1) * GROUP_WIDTH]
        kv_heads = [[c_ref[0, kv_i, h] for h in range(hg)] for kv_i in range(2)]
        for kv_i in range(2):
            for h in range(hg):
                shifted = pltpu.roll(kv_heads[kv_i][h], win - steps, axis=1)
                lo = kv_i * MIX_WIDTH + g * GROUP_WIDTH + h * HEAD_DIM
                new_t = kvn_t[lo:lo + HEAD_DIM, :]
                if win > LANES:
                    n_ref[0, kv_i, h, :, 0:win - LANES] = shifted[:, 0:win - LANES]
                n_ref[0, kv_i, h, :, win - LANES:win] = jnp.where(tail_lane, new_t, shifted[:, win - LANES:win])
        kt = jnp.concatenate(kv_heads[0], axis=0).astype(BF16)
        vt = jnp.concatenate(kv_heads[1], axis=0).astype(BF16)
        q_bd, head_mask = _head_rows(q_all[:, cols], hg)
        slope = jnp.zeros((hg * rows, 1), F32)
        for h in range(hg):
            slope = jnp.where(row_head == h, _slope(g * hg + h), slope)
        pos = lax.broadcasted_iota(jnp.int32, (hg * rows, win), 1)
        dist_buf = ((win + tok) - pos).astype(F32)
        valid_buf = jnp.where(pos >= tok, (pos - tok) & (dil - 1), 1) == 0
        s_buf = jnp.dot(q_bd.astype(BF16), kt, preferred_element_type=F32) * SCALE - slope * dist_buf
        s_buf = jnp.where(valid_buf, s_buf, NEG_INF)
        s_new = []
        for t2 in range(steps):
            col = jnp.sum(q_bd * k_new[lead + t2:lead + t2 + 1, :], axis=-1, keepdims=True) * SCALE \
                - slope * (tok - t2).astype(F32)
            ok = jnp.where(tok >= t2, (tok - t2) & (dil - 1), 1) == 0
            s_new.append(jnp.where(ok, col, NEG_INF))
        m = jnp.max(s_buf, axis=-1, keepdims=True)
        for col in s_new:
            m = jnp.maximum(m, col)
        p_buf = jnp.exp(s_buf - m)
        l = jnp.sum(p_buf, axis=-1, keepdims=True)
        o = lax.dot_general(p_buf.astype(BF16), vt, NT_DIMS, preferred_element_type=F32)
        for t2, col in enumerate(s_new):
            p_col = jnp.exp(col - m)
            l = l + p_col
            o = o + p_col * v_new[lead + t2:lead + t2 + 1, :]
        outs.append(o / l)
        lses.append(m + jnp.log(l))
    mx = jnp.maximum(jnp.maximum(lses[0], lses[1]), lses[2])
    es = [jnp.exp(l - mx) for l in lses]
    den = es[0] + es[1] + es[2]
    mixed = [_head_cols(outs[g] * (es[g] / den), head_mask, hg) for g in range(len(DIL_GROUPS))]
    y_ref[0] = jnp.concatenate(mixed, axis=1) * _silu(gate_ref[0])


def _dil_sample(q, kv_new, gate, caches_t):
    b, steps, _ = q.shape
    rows = _round_up(steps, SUBLANES)
    pad = ((0, 0), (rows - steps, 0), (0, 0))
    q, kv_new, gate = jnp.pad(q, pad), jnp.pad(kv_new, pad), jnp.pad(gate, pad)
    tok = lambda i: (i, 0, 0)
    cache_specs = [pl.BlockSpec((1,) + c.shape[1:], lambda i: (i, 0, 0, 0, 0)) for c in caches_t]
    res = pl.pallas_call(
        functools.partial(_dil_sample_kernel, steps=steps),
        grid=(b,),
        in_specs=[pl.BlockSpec((1, rows, MIX_WIDTH), tok), pl.BlockSpec((1, rows, 2 * MIX_WIDTH), tok),
                  pl.BlockSpec((1, rows, MIX_WIDTH), tok)] + cache_specs,
        out_specs=[pl.BlockSpec((1, rows, MIX_WIDTH), tok)] + cache_specs,
        out_shape=[jax.ShapeDtypeStruct((b, rows, MIX_WIDTH), F32)]
        + [jax.ShapeDtypeStruct(c.shape, F32) for c in caches_t],
        compiler_params=_params(),
        name="dil_sample",
    )(q, kv_new, gate, *caches_t)
    return [res[0][:, rows - steps:]] + list(res[1:])


def _trunk(x, batch, seq, mem_kv_t, h0_re, h0_im, weights, tm, s5_tk, s5_chunk, mem_block, mem_tq,
           side_dtype, kv_proj, dil_attn):
    (w_in, w_out, ln_g, ln_b, tables, w_glu, b_glu) = weights
    splits = (MIX_WIDTH, MIX_WIDTH, MEM_WIDTH, MEM_WIDTH)
    seq_mem = _round_up(seq, SUBLANES)

    def mem_branch(mq, mgate, layer):
        mq = mq.reshape(batch, seq, MEM_WIDTH)
        mgate = mgate.reshape(batch, seq, MEM_WIDTH)
        if seq_mem != seq:
            pad = ((0, 0), (0, seq_mem - seq), (0, 0))
            mq, mgate = jnp.pad(mq, pad), jnp.pad(mgate, pad)
        m = _mem_attn(mq, mem_kv_t, layer, mgate, mem_block, min(mem_tq, seq_mem))
        return m[:, :seq].reshape(batch * seq, MEM_WIDTH)

    if s5_tk == seq:
        block_of = None
    else:
        assert tm == s5_tk
        n_blocks = seq // s5_tk
        block_of = lambda i: (i % n_blocks) * batch + i // n_blocks
    u, gate, mq, mgate = _proj(x, w_in[0], splits, tm, first_out_block=block_of, rest_dtype=side_dtype)
    y, h_re, h_im = _s5_mixer(u, h0_re, h0_im, tables, batch, seq, s5_tk, s5_chunk)
    m = mem_branch(mq, mgate, 0)
    x = _layer_tail(_mix_glu, (y, gate), (w_glu, b_glu), m, x, w_out[0], ln_g[0:1], ln_b[0:1], tm,
                    first_block=block_of)
    kv, (u, gate, mq, mgate), kv_extras = kv_proj(x, w_in[1], splits, side_dtype)

    mix_fn, mix_inputs, attn_extras = dil_attn(u, kv, gate)
    m = mem_branch(mq, mgate, 1)
    x = _layer_tail(mix_fn, mix_inputs, (), m, x, w_out[1], ln_g[1:2], ln_b[1:2], tm)
    return x, h_re, h_im, kv, kv_extras, attn_extras


def kernel(x_prompt, x_sample, cache_mem_kv, state_ssm_re, state_ssm_im, cache_dil1_kv, cache_dil4_kv,
           cache_dil16_kv, mem_prompt, w_in, w_out, ln_g, ln_b, w_mem_kv, ssm_lambda_re, ssm_lambda_im,
           ssm_log_dt, ssm_b_re, ssm_b_im, ssm_c_re, ssm_c_im, ssm_d, w_glu, b_glu, w_kv_shared):
    bp, seq, d = x_prompt.shape
    bs, steps, _ = x_sample.shape
    prompt_chunk, prompt_tk = 8, 512
    n_groups = len(DIL_GROUPS)

    w_in_b = w_in.astype(BF16)
    w_out_b = w_out.astype(BF16)
    w_glu_b = w_glu[0].astype(BF16)
    w_kv_b = w_kv_shared.astype(BF16)
    ssm = (ssm_lambda_re[0], ssm_lambda_im[0], ssm_log_dt[0], ssm_b_re[0], ssm_b_im[0],
           ssm_c_re[0], ssm_c_im[0], ssm_d[0])

    assert prompt_chunk % steps == 0
    weights = (w_in_b, w_out_b, ln_g, ln_b, _s5_tables(prompt_chunk, *ssm), w_glu_b, b_glu)

    mem_kv_p = _mem_kv_t(w_mem_kv.transpose(0, 2, 1).astype(BF16), mem_prompt)
    mem_kv_p = mem_kv_p.reshape(DEPTH, bp, 2, MEM_HEADS, HEAD_DIM, N_MEM)
    zeros = jnp.zeros((bp, SSM_GROUPS, SSM_STATE), F32)
    wide = n_groups - 1
    assert DIL_GROUPS[wide][0] >= seq and all(w <= prompt_tk for w, _ in DIL_GROUPS[:wide])

    def kv_prompt(x, w_in_next, splits, side_dtype):
        kv, kvt_all, kvt_tail, proj = _kv_proj(x, w_kv_b, w_in_next, splits, side_dtype, bp, seq, prompt_tk, wide)
        return kv, proj, (kvt_all, kvt_tail)

    def attn_prompt(u, kv, gate):
        q3 = u.reshape(bp, seq, MIX_WIDTH)
        kv3 = kv.reshape(bp, seq, 2 * MIX_WIDTH)
        res = [_dil_prompt(q3, kv3, g) for g in range(n_groups)]
        outs = [r[0].reshape(bp * seq, GROUP_WIDTH) for r in res]
        lses = [r[1].reshape(bp * seq, GROUP_WIDTH) for r in res]
        return _mix_merge, (*outs, *lses, gate), None

    y_p, hre_p, him_p, _, (kvt_all, kvt_tail), _ = _trunk(
        x_prompt.reshape(bp * seq, d), bp, seq, mem_kv_p, zeros, zeros, weights,
        prompt_tk, prompt_tk, prompt_chunk, 1, 512, BF16, kv_prompt, attn_prompt)
    kvt_tail = kvt_tail.reshape(bp, wide, 2, HEADS_PER_GROUP, HEAD_DIM, prompt_tk)
    win_p = [kvt_tail[:, g, :, :, :, prompt_tk - win:].transpose(0, 4, 1, 2, 3)
             for g, (win, _) in enumerate(DIL_GROUPS[:wide])]
    win_p.append(kvt_all.reshape(bp, 2, HEADS_PER_GROUP, HEAD_DIM, seq).transpose(0, 4, 1, 2, 3))

    caches = (cache_dil1_kv, cache_dil4_kv, cache_dil16_kv)
    caches_t = [c.transpose(0, 2, 3, 4, 1) for c in caches]
    mem_kv_s = cache_mem_kv.transpose(0, 1, 3, 4, 5, 2)

    def kv_sample(x, w_in_next, splits, side_dtype):
        (kv,) = _proj(x, w_kv_b, (2 * MIX_WIDTH,), 256)
        return kv, _proj(x, w_in_next, splits, 256, rest_dtype=side_dtype), None

    def attn_sample(u, kv, gate):
        res = _dil_sample(u.reshape(bs, steps, MIX_WIDTH), kv.reshape(bs, steps, 2 * MIX_WIDTH),
                          gate.reshape(bs, steps, MIX_WIDTH), caches_t)
        return _mix_plain, (res[0].reshape(bs * steps, MIX_WIDTH),), res[1:]

    y_s, hre_s, him_s, _, _, rolled = _trunk(
        x_sample.reshape(bs * steps, d), bs, steps, mem_kv_s, state_ssm_re[0], state_ssm_im[0],
        weights, 256, steps, steps, 8, 8, F32, kv_sample, attn_sample)
    win_s = [c.transpose(0, 4, 1, 2, 3) for c in rolled]

    mem_kv_out = mem_kv_p.transpose(0, 1, 5, 2, 3, 4)
    return (y_p.reshape(bp, seq, d), y_s.reshape(bs, steps, d), mem_kv_out,
            hre_p[None], him_p[None], win_p[0], win_p[1], win_p[2],
            hre_s[None], him_s[None], win_s[0], win_s[1], win_s[2])
```

```python
import functools

import jax
import jax.numpy as jnp
from jax import lax
from jax.experimental import pallas as pl
from jax.experimental.pallas import tpu as pltpu

F32 = jnp.float32
BF16 = jnp.bfloat16

D_MODEL = 1024
HEAD_DIM = 64
MIX_WIDTH = 768
MEM_WIDTH = 256
MEM_HEADS = 4
N_MEM = 256
SSM_GROUP = 16
SSM_GROUPS = 48
SSM_STATE = 64
DIL_GROUPS = ((128, 1), (512, 4), (2048, 16))
ATT_HEADS = 12
HEADS_PER_GROUP = 4
GROUP_WIDTH = HEADS_PER_GROUP * HEAD_DIM
DEPTH = 2
DEEPNORM_ALPHA = (2.0 * DEPTH) ** 0.25
LN_EPS = 1e-5
SCALE = HEAD_DIM ** -0.5
NEG_INF = -1e30
WINDOW_KEYS = 128
LANES = 128
SUBLANES = 8
SLAB_GROUPS = LANES // SSM_GROUP
SSM_SLABS = SSM_GROUPS // SLAB_GROUPS
SLAB_STATE = SLAB_GROUPS * SSM_STATE
VMEM_LIMIT = 56 * 1024 * 1024
NT_DIMS = (((1,), (1,)), ((), ()))

PROMPT_TILE = 512
SAMPLE_TILE = 256
S5_CHUNK = 8
S5_BLOCK = 512
MEM_QUERY_TILE = 512
SAMPLE_MEM_BATCH = 8


def _slope(head):
    return 2.0 ** (-8.0 * (head + 1) / ATT_HEADS)


def _params(*semantics):
    return pltpu.CompilerParams(dimension_semantics=semantics or ("parallel",),
                                vmem_limit_bytes=VMEM_LIMIT)


def _silu(x):
    return x * jax.nn.sigmoid(x)


def _round_up(n, m):
    return -(-n // m) * m


def _proj_kernel(x_ref, w_ref, *out_refs, splits):
    x = x_ref[...].astype(BF16)
    off = 0
    for o_ref, n in zip(out_refs, splits):
        o_ref[...] = jnp.dot(x, w_ref[:, off:off + n], preferred_element_type=F32).astype(o_ref.dtype)
        off += n


def _proj(x, w, splits, tm, first_out_block=None, rest_dtype=F32):
    m, k = x.shape
    n_total = w.shape[1]
    assert sum(splits) == n_total and m % tm == 0
    tok = lambda i: (i, 0)
    first = tok if first_out_block is None else (lambda i: (first_out_block(i), 0))
    out_maps = [first] + [tok] * (len(splits) - 1)
    return pl.pallas_call(
        functools.partial(_proj_kernel, splits=splits),
        grid=(m // tm,),
        in_specs=[pl.BlockSpec((tm, k), tok), pl.BlockSpec((k, n_total), lambda i: (0, 0))],
        out_specs=[pl.BlockSpec((tm, n), om) for n, om in zip(splits, out_maps)],
        out_shape=[jax.ShapeDtypeStruct((m, n), F32 if i == 0 else rest_dtype) for i, n in enumerate(splits)],
        compiler_params=_params(),
        name="proj",
    )(x, w)


def _kv_proj_kernel(x_ref, w_kv_ref, w_in_ref, kv_ref, kvt_all_ref, kvt_tail_ref, *proj_refs,
                    tiles, wide, splits):
    x = x_ref[...].astype(BF16)
    kv = jnp.dot(x, w_kv_ref[...], preferred_element_type=F32)
    kv_ref[...] = kv

    def group_t(g):
        cols = [kv[:, i * MIX_WIDTH + g * GROUP_WIDTH:i * MIX_WIDTH + (g + 1) * GROUP_WIDTH] for i in range(2)]
        return jnp.concatenate(cols, axis=1).T

    kvt_all_ref[0] = group_t(wide)

    @pl.when(pl.program_id(0) % tiles == tiles - 1)
    def _():
        for g in range(wide):
            kvt_tail_ref[0, g * 2 * GROUP_WIDTH:(g + 1) * 2 * GROUP_WIDTH, :] = group_t(g)

    off = 0
    for o_ref, n in zip(proj_refs, splits):
        o_ref[...] = jnp.dot(x, w_in_ref[:, off:off + n], preferred_element_type=F32).astype(o_ref.dtype)
        off += n


def _kv_proj(x, w_kv, w_in, splits, rest_dtype, batch, seq, tm, wide):
    m, k = x.shape
    n_kv, n_in = w_kv.shape[1], w_in.shape[1]
    group_rows = 2 * GROUP_WIDTH
    tiles = seq // tm
    tok = lambda i: (i, 0)
    fixed = lambda i: (0, 0)
    res = pl.pallas_call(
        functools.partial(_kv_proj_kernel, tiles=tiles, wide=wide, splits=splits),
        grid=(m // tm,),
        in_specs=[pl.BlockSpec((tm, k), tok), pl.BlockSpec((k, n_kv), fixed), pl.BlockSpec((k, n_in), fixed)],
        out_specs=[pl.BlockSpec((tm, n_kv), tok),
                   pl.BlockSpec((1, group_rows, tm), lambda i: (i // tiles, 0, i % tiles)),
                   pl.BlockSpec((1, wide * group_rows, tm), lambda i: (i // tiles, 0, 0))]
        + [pl.BlockSpec((tm, n), tok) for n in splits],
        out_shape=[jax.ShapeDtypeStruct((m, n_kv), F32),
                   jax.ShapeDtypeStruct((batch, group_rows, seq), F32),
                   jax.ShapeDtypeStruct((batch, wide * group_rows, tm), F32)]
        + [jax.ShapeDtypeStruct((m, n), F32 if i == 0 else rest_dtype) for i, n in enumerate(splits)],
        compiler_params=_params("arbitrary"),
        name="kv_proj",
    )(x, w_kv, w_in)
    return res[0], res[1], res[2], res[3:]


def _mem_kv_t_kernel(w_ref, mem_ref, o_ref):
    o_ref[0, 0] = lax.dot_general(w_ref[0], mem_ref[0].astype(BF16), NT_DIMS, preferred_element_type=F32)


def _mem_kv_t(w_t, mem):
    depth, n, d = w_t.shape
    b = mem.shape[0]
    return pl.pallas_call(
        _mem_kv_t_kernel,
        grid=(depth, b),
        in_specs=[pl.BlockSpec((1, n, d), lambda l, i: (l, 0, 0)),
                  pl.BlockSpec((1, N_MEM, d), lambda l, i: (i, 0, 0))],
        out_specs=pl.BlockSpec((1, 1, n, N_MEM), lambda l, i: (l, i, 0, 0)),
        out_shape=jax.ShapeDtypeStruct((depth, b, n, N_MEM), F32),
        compiler_params=_params("parallel", "parallel"),
        name="mem_kv_t",
    )(w_t, mem)


def _s5_tables(chunk, lam_re, lam_im, log_dt, b_re, b_im, c_re, c_im, d_skip):
    hp = lax.Precision.HIGHEST
    p, c = SSM_STATE, SSM_GROUP
    ns, gs = SSM_SLABS, SLAB_GROUPS
    kl = chunk * LANES
    lr = jnp.minimum(lam_re.astype(F32), -1e-4)
    li = lam_im.astype(F32)
    dt = jnp.exp(log_dt.astype(F32))[:, None]
    def lam_bar_pow(n):
        n = n.astype(F32)[None, :, None]
        mag = jnp.exp(lr[:, None, :] * dt[:, None, :] * n)
        ang = li[:, None, :] * dt[:, None, :] * n
        return mag * jnp.cos(ang), mag * jnp.sin(ang)

    pw_re, pw_im = lam_bar_pow(jnp.arange(chunk + 1))
    nr, ni = pw_re[:, 1] - 1.0, pw_im[:, 1]
    den = lr * lr + li * li
    f_re, f_im = (nr * lr + ni * li) / den, (ni * lr - nr * li) / den
    bb_re = f_re[:, :, None] * b_re - f_im[:, :, None] * b_im
    bb_im = f_re[:, :, None] * b_im + f_im[:, :, None] * b_re
    c_re, c_im = c_re.astype(F32), c_im.astype(F32)

    x_re = pw_re[:, :chunk, :, None] * bb_re[:, None] - pw_im[:, :chunk, :, None] * bb_im[:, None]
    x_im = pw_re[:, :chunk, :, None] * bb_im[:, None] + pw_im[:, :chunk, :, None] * bb_re[:, None]
    conv = (jnp.einsum('gcp,gtpd->gtcd', c_re, x_re, precision=hp)
            - jnp.einsum('gcp,gtpd->gtcd', c_im, x_im, precision=hp))
    def slab_diag(x):
        r, w = x.shape[-2:]
        x = jnp.moveaxis(x, 1, -3)
        x = x.reshape(x.shape[:-3] + (gs * r, w))
        x = jnp.tile(x, (1,) * (x.ndim - 1) + (gs,))
        own = (jnp.arange(gs * r) // r)[:, None] == (jnp.arange(gs * w) // w)[None, :]
        return jnp.where(own, x, 0.0)

    conv_d = slab_diag(conv.transpose(0, 1, 3, 2).reshape(ns, gs, chunk, c, c))
    b_diag = lambda x: slab_diag(x.transpose(0, 2, 1).reshape(ns, gs, c, p))
    c_diag = lambda x: slab_diag(x.transpose(0, 2, 1).reshape(ns, gs, p, c))

    def slab_rows(x):
        n = x.shape[1]
        return x.reshape(ns, gs, n, p).transpose(0, 2, 1, 3).reshape(ns, n, SLAB_STATE)

    strip = conv_d.transpose(0, 2, 1, 3).reshape(ns, LANES, kl)
    toep = jnp.stack([jnp.pad(strip[:, :, :kl - s * LANES], ((0, 0), (0, 0), (s * LANES, 0)))
                      for s in range(chunk)], axis=1).reshape(ns, kl, kl)

    rev_re, rev_im = lam_bar_pow(chunk - 1 - jnp.arange(chunk))
    rev_re, rev_im = slab_rows(rev_re), slab_rows(rev_im)
    bd_re, bd_im = b_diag(bb_re), b_diag(bb_im)
    rev_a = jnp.concatenate([rev_re, rev_re], axis=-1)[:, :, None, :]
    rev_b = jnp.concatenate([-rev_im, rev_im], axis=-1)[:, :, None, :]
    bd_a = jnp.concatenate([bd_re, bd_im], axis=-1)[:, None]
    bd_b = jnp.concatenate([bd_im, bd_re], axis=-1)[:, None]
    w_in = (rev_a * bd_a + rev_b * bd_b).reshape(ns, kl, 2 * SLAB_STATE)

    col_re = slab_rows(pw_re[:, 1:]).transpose(0, 2, 1)
    col_im = slab_rows(pw_im[:, 1:]).transpose(0, 2, 1)
    cd_re, cd_im = c_diag(c_re), c_diag(c_im)
    col_a = jnp.concatenate([col_re, -col_re], axis=1)
    col_b = jnp.concatenate([-col_im, -col_im], axis=1)
    cd_a = jnp.concatenate([cd_re, cd_im], axis=1)
    cd_b = jnp.concatenate([cd_im, cd_re], axis=1)
    w_out = jnp.concatenate([col_a[:, :, t:t + 1] * cd_a + col_b[:, :, t:t + 1] * cd_b
                             for t in range(chunk)], axis=-1)

    lam_pows = jnp.stack([slab_rows(pw_re).transpose(1, 0, 2),
                          slab_rows(pw_im).transpose(1, 0, 2)], axis=2)
    skip = jnp.tile(d_skip.astype(F32).reshape(ns, LANES), (1, chunk))[:, None]
    return toep.astype(BF16), w_in.astype(BF16), w_out.astype(BF16), lam_pows, skip


def _s5_kernel(u_ref, toep_ref, win_ref, wout_ref, lam_ref, skip_ref, h0_ref, y_ref, hout_ref,
               a_scr, g_scr, hs_scr, y_scr, h_scr, *, batch, tk, chunk):
    n_chunks = tk // chunk
    ns = SLAB_STATE

    @pl.when(pl.program_id(1) == 0)
    def _():
        h_scr[...] = h0_ref[0]

    tile_swap = batch == SUBLANES and chunk == SUBLANES

    def gather(kk, carry):
        rows = pl.ds(pl.multiple_of(kk * batch, SUBLANES), batch)
        if tile_swap:
            tiles = [u_ref[pl.ds(pl.multiple_of(b * tk + kk * chunk, SUBLANES), chunk), :] for b in range(batch)]
            by_token = pltpu.einshape("btl->tbl", jnp.stack(tiles, axis=0))
            for t in range(chunk):
                a_scr[rows, t * LANES:(t + 1) * LANES] = by_token[t]
            return carry
        for t in range(chunk):
            a_scr[rows, t * LANES:(t + 1) * LANES] = u_ref[pl.ds(kk * chunk + t, batch, stride=tk), :]
        return carry

    lax.fori_loop(0, n_chunks, gather, 0)
    a = a_scr[...].astype(BF16)
    g_scr[...] = jnp.dot(a, win_ref[0], preferred_element_type=F32)
    ar = lam_ref[0, 0:1, :]
    ai = lam_ref[0, 1:2, :]

    def step(kk, carry):
        hr, hi = carry
        rows = pl.ds(pl.multiple_of(kk * batch, SUBLANES), batch)
        hs_scr[rows, 0:ns] = hr
        hs_scr[rows, ns:2 * ns] = hi
        return (ar * hr - ai * hi + g_scr[rows, 0:ns], ar * hi + ai * hr + g_scr[rows, ns:2 * ns])

    hr, hi = lax.fori_loop(0, n_chunks, step, (h_scr[:, 0:ns], h_scr[:, ns:2 * ns]))
    h_scr[:, 0:ns] = hr
    h_scr[:, ns:2 * ns] = hi
    hout_ref[0, :, 0:ns] = hr
    hout_ref[0, :, ns:2 * ns] = hi
    y = (jnp.dot(a, toep_ref[0], preferred_element_type=F32)
         + jnp.dot(hs_scr[...].astype(BF16), wout_ref[0], preferred_element_type=F32)
         + skip_ref[0] * a_scr[...])
    y_scr[...] = jax.nn.gelu(y)

    def scatter(kk, carry):
        rows = pl.ds(pl.multiple_of(kk * batch, SUBLANES), batch)
        if tile_swap:
            by_token = jnp.stack([y_scr[rows, t * LANES:(t + 1) * LANES] for t in range(chunk)], axis=0)
            by_seq = pltpu.einshape("tbl->btl", by_token)
            for b in range(batch):
                y_ref[pl.ds(pl.multiple_of(b * tk + kk * chunk, SUBLANES), chunk), :] = by_seq[b]
            return carry
        for t in range(chunk):
            y_ref[pl.ds(kk * chunk + t, batch, stride=tk), :] = y_scr[rows, t * LANES:(t + 1) * LANES]
        return carry

    lax.fori_loop(0, n_chunks, scatter, 0)


def _s5_mixer(u, h0_re, h0_im, tables, batch, seq, tk, chunk):
    toep, w_in, w_out, lam_pows, skip = tables
    kl = chunk * LANES
    table_chunk = toep.shape[-1] // LANES
    assert (table_chunk - chunk) % chunk == 0
    w_in_block = (table_chunk - chunk) // chunk
    lam_l = lam_pows[chunk]
    n_blocks = seq // tk
    rows = (tk // chunk) * batch
    to_slabs = lambda h: h.reshape(batch, SSM_SLABS, SLAB_STATE).transpose(1, 0, 2)
    h0 = jnp.concatenate([to_slabs(h0_re), to_slabs(h0_im)], axis=-1)
    slab = lambda j, k: (j, 0, 0)
    y, h = pl.pallas_call(
        functools.partial(_s5_kernel, batch=batch, tk=tk, chunk=chunk),
        grid=(SSM_SLABS, n_blocks),
        in_specs=[pl.BlockSpec((batch * tk, LANES), lambda j, k: (k, j)),
                  pl.BlockSpec((1, kl, kl), slab),
                  pl.BlockSpec((1, kl, 2 * SLAB_STATE), lambda j, k: (j, w_in_block, 0)),
                  pl.BlockSpec((1, 2 * SLAB_STATE, kl), slab),
                  pl.BlockSpec((1, 2, SLAB_STATE), slab),
                  pl.BlockSpec((1, 1, kl), slab),
                  pl.BlockSpec((1, batch, 2 * SLAB_STATE), slab)],
        out_specs=[pl.BlockSpec((batch * tk, LANES), lambda j, k: (k, j)),
                   pl.BlockSpec((1, batch, 2 * SLAB_STATE), slab)],
        out_shape=[jax.ShapeDtypeStruct(u.shape, F32),
                   jax.ShapeDtypeStruct((SSM_SLABS, batch, 2 * SLAB_STATE), F32)],
        scratch_shapes=[pltpu.VMEM((rows, kl), F32), pltpu.VMEM((rows, 2 * SLAB_STATE), F32),
                        pltpu.VMEM((rows, 2 * SLAB_STATE), F32), pltpu.VMEM((rows, kl), F32),
                        pltpu.VMEM((batch, 2 * SLAB_STATE), F32)],
        compiler_params=_params("parallel", "arbitrary"),
        name="s5_chunks",
    )(u, toep, w_in, w_out, lam_l, skip, h0)
    from_slabs = lambda x: x.transpose(1, 0, 2).reshape(batch, SSM_GROUPS, SSM_STATE)
    return y, from_slabs(h[:, :, :SLAB_STATE]), from_slabs(h[:, :, SLAB_STATE:])


def _head_rows(x, heads):
    t, w = x.shape
    row_head = lax.broadcasted_iota(jnp.int32, (heads * t, w), 0) // t
    lane_head = lax.broadcasted_iota(jnp.int32, (heads * t, w), 1) // HEAD_DIM
    mask = row_head == lane_head
    return jnp.where(mask, jnp.concatenate([x] * heads, axis=0), 0.0), mask


def _head_cols(x, mask, heads):
    t = x.shape[0] // heads
    x = jnp.where(mask, x, 0.0)
    out = x[0:t]
    for h in range(1, heads):
        out = out + x[h * t:(h + 1) * t]
    return out


def _mem_attn_kernel(q_ref, kv_ref, mg_ref, o_ref, *, block_batch):
    for bi in range(block_batch):
        q_bd, mask = _head_rows(q_ref[bi].astype(F32), MEM_HEADS)
        kt = jnp.concatenate([kv_ref[0, bi, 0, h] for h in range(MEM_HEADS)], axis=0).astype(BF16)
        vt = jnp.concatenate([kv_ref[0, bi, 1, h] for h in range(MEM_HEADS)], axis=0).astype(BF16)
        s = jnp.dot(q_bd.astype(BF16), kt, preferred_element_type=F32) * SCALE
        p = jnp.exp(s - jnp.max(s, axis=-1, keepdims=True))
        l = jnp.sum(p, axis=-1, keepdims=True)
        o = lax.dot_general(p.astype(BF16), vt, NT_DIMS, preferred_element_type=F32) / l
        o_ref[bi] = (_head_cols(o, mask, MEM_HEADS) * _silu(mg_ref[bi].astype(F32))).astype(o_ref.dtype)


def _mem_attn(q, kv_t, layer, mgate, block_batch, tq):
    b, t, w = q.shape
    tok = lambda i, j: (i, j, 0)
    return pl.pallas_call(
        functools.partial(_mem_attn_kernel, block_batch=block_batch),
        grid=(b // block_batch, t // tq),
        in_specs=[pl.BlockSpec((block_batch, tq, w), tok),
                  pl.BlockSpec((1, block_batch) + kv_t.shape[2:], lambda i, j: (layer, i, 0, 0, 0, 0)),
                  pl.BlockSpec((block_batch, tq, w), tok)],
        out_specs=pl.BlockSpec((block_batch, tq, w), tok),
        out_shape=jax.ShapeDtypeStruct((b, t, w), q.dtype),
        compiler_params=_params("parallel", "parallel"),
        name="mem_attn",
    )(q, kv_t, mgate)


def _mix_plain(tok_refs, fixed_refs):
    (y_ref,) = tok_refs
    return y_ref[...]


def _mix_glu(tok_refs, fixed_refs):
    y_ref, gate_ref = tok_refs
    w_ref, b_ref = fixed_refs
    y = y_ref[...]
    z = jnp.dot(y.astype(BF16), w_ref[...], preferred_element_type=F32) + b_ref[...]
    return y * jax.nn.sigmoid(z) * _silu(gate_ref[...].astype(F32))


def _mix_merge(tok_refs, fixed_refs):
    o0, o1, o2, l0, l1, l2, gate_ref = tok_refs
    ls = [l0[...], l1[...], l2[...]]
    mx = jnp.maximum(jnp.maximum(ls[0], ls[1]), ls[2])
    es = [jnp.exp(l - mx) for l in ls]
    den = es[0] + es[1] + es[2]
    o = jnp.concatenate([o_ref[...] * (e / den) for o_ref, e in zip((o0, o1, o2), es)], axis=1)
    return o * _silu(gate_ref[...].astype(F32))


def _layer_tail_kernel(*refs, n_tok, n_fixed, mix_fn):
    mix = mix_fn(refs[:n_tok], refs[n_tok:n_tok + n_fixed])
    m_ref, x_ref, wy_ref, wm_ref, g_ref, b_ref, o_ref = refs[n_tok + n_fixed:]
    out = (jnp.dot(mix.astype(BF16), wy_ref[...], preferred_element_type=F32)
           + jnp.dot(m_ref[...].astype(BF16), wm_ref[...], preferred_element_type=F32))
    z = DEEPNORM_ALPHA * x_ref[...] + out
    mu = jnp.mean(z, axis=-1, keepdims=True)
    zc = z - mu
    var = jnp.mean(zc * zc, axis=-1, keepdims=True)
    o_ref[...] = zc * lax.rsqrt(var + LN_EPS) * g_ref[...] + b_ref[...]


def _layer_tail(mix_fn, tok_arrays, fixed_arrays, m, x, w_out, ln_g, ln_b, tm, first_block=None):
    rows, d = x.shape
    tok = lambda i: (i, 0)
    fixed = lambda i: (0, 0)
    first = tok if first_block is None else (lambda i: (first_block(i), 0))
    tok_maps = [first] + [tok] * (len(tok_arrays) - 1)
    return pl.pallas_call(
        functools.partial(_layer_tail_kernel, n_tok=len(tok_arrays), n_fixed=len(fixed_arrays), mix_fn=mix_fn),
        grid=(rows // tm,),
        in_specs=[pl.BlockSpec((tm, a.shape[1]), tm_map) for a, tm_map in zip(tok_arrays, tok_maps)]
        + [pl.BlockSpec(a.shape, fixed) for a in fixed_arrays]
        + [pl.BlockSpec((tm, MEM_WIDTH), tok), pl.BlockSpec((tm, d), tok),
           pl.BlockSpec((MIX_WIDTH, d), fixed), pl.BlockSpec((MEM_WIDTH, d), fixed),
           pl.BlockSpec((1, d), fixed), pl.BlockSpec((1, d), fixed)],
        out_specs=pl.BlockSpec((tm, d), tok),
        out_shape=jax.ShapeDtypeStruct((rows, d), F32),
        compiler_params=_params(),
        name="layer_tail",
    )(*tok_arrays, *fixed_arrays, m, x, w_out[:MIX_WIDTH], w_out[MIX_WIDTH:], ln_g, ln_b)


def _dil_prompt_kernel(q_ref, k_ref, v_ref, o_ref, lse_ref, *, seq, dil, group):
    n = WINDOW_KEYS
    n_blocks = seq // dil // n
    heads = LANES // HEAD_DIM
    first_head = group * HEADS_PER_GROUP
    second_pair = pl.program_id(1) == 1
    slopes = [jnp.where(second_pair, _slope(first_head + heads + h), _slope(first_head + h))
              for h in range(heads)]
    ku = lax.broadcasted_iota(jnp.int32, (2 * n, n), 0)
    qi = lax.broadcasted_iota(jnp.int32, (2 * n, n), 1)
    delta2 = qi + n - ku
    valid2 = jnp.where(delta2 >= 0, delta2, n + 1) <= n
    delta1 = delta2[n:, :]
    valid1 = delta1 >= 0
    bias2 = [s * (delta2 * dil).astype(F32) for s in slopes]
    bias1 = [s * (delta1 * dil).astype(F32) for s in slopes]
    lane_head = lax.broadcasted_iota(jnp.int32, (1, LANES), 1) // HEAD_DIM
    head_lanes = [(lane_head == h).astype(F32) for h in range(heads)]
    tn_dims = (((0,), (0,)), ((), ()))

    def stream_rows(start):
        return pl.ds(start, n, stride=dil) if dil > 1 else pl.ds(start, n)

    def block(start, has_prev):
        cur = stream_rows(start)
        q = q_ref[0, cur, :]
        k = k_ref[0, cur, :]
        v = v_ref[0, cur, :]
        if has_prev:
            prev = stream_rows(start - n * dil)
            k = jnp.concatenate([k_ref[0, prev, :], k], axis=0)
            v = jnp.concatenate([v_ref[0, prev, :], v], axis=0)
            bias, valid = bias2, valid2
        else:
            bias, valid = bias1, valid1
        q = q * SCALE
        o = jnp.zeros((n, LANES), F32)
        lse_b = jnp.zeros((n, LANES), F32)
        k = k.astype(BF16)
        for h in range(heads):
            s = lax.dot_general(k, (q * head_lanes[h]).astype(BF16),
                                NT_DIMS, preferred_element_type=F32)
            s = jnp.where(valid, s - bias[h], NEG_INF)
            m = jnp.max(s, axis=0, keepdims=True)
            p = jnp.exp(s - m)
            l = jnp.sum(p, axis=0, keepdims=True)
            p = (p * (1.0 / l)).astype(BF16)
            o = o + lax.dot_general(p, (v * head_lanes[h]).astype(BF16), tn_dims, preferred_element_type=F32)
            lse = m + jnp.log(l)
            t1 = lse.astype(BF16)
            r1 = lse - t1.astype(F32)
            t2 = r1.astype(BF16)
            t3 = (r1 - t2.astype(F32)).astype(BF16)
            terms = jnp.concatenate([t1, t2, t3, jnp.zeros((SUBLANES - 3, n), BF16)], axis=0)
            spread = jnp.broadcast_to(head_lanes[h], (SUBLANES, LANES)).astype(BF16)
            lse_b = lse_b + lax.dot_general(terms, spread, tn_dims, preferred_element_type=F32)
        o_ref[0, cur, :] = o
        lse_ref[0, cur, :] = lse_b

    for r in range(dil):
        for jb in range(n_blocks):
            block(r + jb * n * dil, jb > 0)


def _dil_prompt(q, kv, group):
    b, seq, _ = q.shape
    _, dil = DIL_GROUPS[group]
    pairs = GROUP_WIDTH // LANES
    k_col = group * pairs
    v_col = MIX_WIDTH // LANES + k_col
    blk = (1, seq, LANES)
    return pl.pallas_call(
        functools.partial(_dil_prompt_kernel, seq=seq, dil=dil, group=group),
        grid=(b, pairs),
        in_specs=[pl.BlockSpec(blk, lambda i, j: (i, 0, k_col + j)),
                  pl.BlockSpec(blk, lambda i, j: (i, 0, k_col + j)),
                  pl.BlockSpec(blk, lambda i, j: (i, 0, v_col + j))],
        out_specs=[pl.BlockSpec(blk, lambda i, j: (i, 0, j)), pl.BlockSpec(blk, lambda i, j: (i, 0, j))],
        out_shape=[jax.ShapeDtypeStruct((b, seq, GROUP_WIDTH), F32)] * 2,
        compiler_params=_params("parallel", "parallel"),
        name=f"dil_prompt_g{group}",
    )(q, kv, kv)


def _dil_sample_kernel(q_ref, kvn_ref, gate_ref, c0_ref, c1_ref, c2_ref,
                       y_ref, n0_ref, n1_ref, n2_ref, *, steps):
    rows = q_ref.shape[1]
    lead = rows - steps
    q_all = q_ref[0]
    kvn = kvn_ref[0]
    hg = HEADS_PER_GROUP
    stacked = lax.broadcasted_iota(jnp.int32, (hg * rows, 1), 0)
    row_head = stacked // rows
    tok = stacked - row_head * rows - lead
    kvn_t = jnp.concatenate([jnp.zeros((LANES - rows, 2 * MIX_WIDTH), F32), kvn], axis=0).T
    tail_lane = lax.broadcasted_iota(jnp.int32, (HEAD_DIM, LANES), 1) >= LANES - steps
    outs, lses = [], []
    for g, ((win, dil), c_ref, n_ref) in enumerate(zip(DIL_GROUPS, (c0_ref, c1_ref, c2_ref),
                                                       (n0_ref, n1_ref, n2_ref))):
        cols = slice(g * GROUP_WIDTH, (g + 1) * GROUP_WIDTH)
        k_new = kvn[:, cols]
        v_new = kvn[:, MIX_WIDTH + g * GROUP_WIDTH:MIX_WIDTH + (g + 1) * GROUP_WIDTH]
        kv_heads = [[c_ref[0, kv_i, h] for h in range(hg)] for kv_i in range(2)]
        for kv_i in range(2):
            for h in range(hg):
                shifted = pltpu.roll(kv_heads[kv_i][h], win - steps, axis=1)
                lo = kv_i * MIX_WIDTH + g * GROUP_WIDTH + h * HEAD_DIM
                new_t = kvn_t[lo:lo + HEAD_DIM, :]
                if win > LANES:
                    n_ref[0, kv_i, h, :, 0:win - LANES] = shifted[:, 0:win - LANES]
                n_ref[0, kv_i, h, :, win - LANES:win] = jnp.where(tail_lane, new_t, shifted[:, win - LANES:win])
        kt = jnp.concatenate(kv_heads[0], axis=0).astype(BF16)
        vt = jnp.concatenate(kv_heads[1], axis=0).astype(BF16)
        q_bd, head_mask = _head_rows(q_all[:, cols], hg)
        slope = jnp.zeros((hg * rows, 1), F32)
        for h in range(hg):
            slope = jnp.where(row_head == h, _slope(g * hg + h), slope)
        pos = lax.broadcasted_iota(jnp.int32, (hg * rows, win), 1)
        dist_buf = ((win + tok) - pos).astype(F32)
        valid_buf = jnp.where(pos >= tok, (pos - tok) & (dil - 1), 1) == 0
        s_buf = jnp.dot(q_bd.astype(BF16), kt, preferred_element_type=F32) * SCALE - slope * dist_buf
        s_buf = jnp.where(valid_buf, s_buf, NEG_INF)
        s_new = []
        for t2 in range(steps):
            col = jnp.sum(q_bd * k_new[lead + t2:lead + t2 + 1, :], axis=-1, keepdims=True) * SCALE \
                - slope * (tok - t2).astype(F32)
            ok = jnp.where(tok >= t2, (tok - t2) & (dil - 1), 1) == 0
            s_new.append(jnp.where(ok, col, NEG_INF))
        m = jnp.max(s_buf, axis=-1, keepdims=True)
        for col in s_new:
            m = jnp.maximum(m, col)
        p_buf = jnp.exp(s_buf - m)
        l = jnp.sum(p_buf, axis=-1, keepdims=True)
        o = lax.dot_general(p_buf.astype(BF16), vt, NT_DIMS, preferred_element_type=F32)
        for t2, col in enumerate(s_new):
            p_col = jnp.exp(col - m)
            l = l + p_col
            o = o + p_col * v_new[lead + t2:lead + t2 + 1, :]
        outs.append(o / l)
        lses.append(m + jnp.log(l))
    mx = jnp.maximum(jnp.maximum(lses[0], lses[1]), lses[2])
    es = [jnp.exp(l - mx) for l in lses]
    den = es[0] + es[1] + es[2]
    mixed = [_head_cols(outs[g] * (es[g] / den), head_mask, hg) for g in range(len(DIL_GROUPS))]
    y_ref[0] = jnp.concatenate(mixed, axis=1) * _silu(gate_ref[0])


def _dil_sample(q, kv_new, gate, caches_t):
    b, steps, _ = q.shape
    rows = _round_up(steps, SUBLANES)
    pad = ((0, 0), (rows - steps, 0), (0, 0))
    q, kv_new, gate = jnp.pad(q, pad), jnp.pad(kv_new, pad), jnp.pad(gate, pad)
    tok = lambda i: (i, 0, 0)
    cache_specs = [pl.BlockSpec((1,) + c.shape[1:], lambda i: (i, 0, 0, 0, 0)) for c in caches_t]
    res = pl.pallas_call(
        functools.partial(_dil_sample_kernel, steps=steps),
        grid=(b,),
        in_specs=[pl.BlockSpec((1, rows, MIX_WIDTH), tok), pl.BlockSpec((1, rows, 2 * MIX_WIDTH), tok),
                  pl.BlockSpec((1, rows, MIX_WIDTH), tok)] + cache_specs,
        out_specs=[pl.BlockSpec((1, rows, MIX_WIDTH), tok)] + cache_specs,
        out_shape=[jax.ShapeDtypeStruct((b, rows, MIX_WIDTH), F32)]
        + [jax.ShapeDtypeStruct(c.shape, F32) for c in caches_t],
        compiler_params=_params(),
        name="dil_sample",
    )(q, kv_new, gate, *caches_t)
    return [res[0][:, rows - steps:]] + list(res[1:])


def _trunk(x, batch, seq, mem_kv_t, h0_re, h0_im, weights, tm, s5_tk, s5_chunk, mem_block, mem_tq,
           side_dtype, kv_proj, dil_attn):
    (w_in, w_out, ln_g, ln_b, tables, w_glu, b_glu) = weights
    splits = (MIX_WIDTH, MIX_WIDTH, MEM_WIDTH, MEM_WIDTH)
    seq_mem = _round_up(seq, SUBLANES)

    def mem_branch(mq, mgate, layer):
        mq = mq.reshape(batch, seq, MEM_WIDTH)
        mgate = mgate.reshape(batch, seq, MEM_WIDTH)
        if seq_mem != seq:
            pad = ((0, 0), (0, seq_mem - seq), (0, 0))
            mq, mgate = jnp.pad(mq, pad), jnp.pad(mgate, pad)
        m = _mem_attn(mq, mem_kv_t, layer, mgate, mem_block, min(mem_tq, seq_mem))
        return m[:, :seq].reshape(batch * seq, MEM_WIDTH)

    if s5_tk == seq:
        block_of = None
    else:
        assert s5_tk % tm == 0 and seq % s5_tk == 0
        per_seq, per_block = seq // tm, s5_tk // tm

        def block_of(i):
            b, t = i // per_seq, i % per_seq
            return ((t // per_block) * batch + b) * per_block + t % per_block
    u, gate, mq, mgate = _proj(x, w_in[0], splits, tm, first_out_block=block_of, rest_dtype=side_dtype)
    y, h_re, h_im = _s5_mixer(u, h0_re, h0_im, tables, batch, seq, s5_tk, s5_chunk)
    m = mem_branch(mq, mgate, 0)
    x = _layer_tail(_mix_glu, (y, gate), (w_glu, b_glu), m, x, w_out[0], ln_g[0:1], ln_b[0:1], tm,
                    first_block=block_of)
    kv, (u, gate, mq, mgate), kv_extras = kv_proj(x, w_in[1], splits, side_dtype)

    mix_fn, mix_inputs, attn_extras = dil_attn(u, kv, gate)
    m = mem_branch(mq, mgate, 1)
    x = _layer_tail(mix_fn, mix_inputs, (), m, x, w_out[1], ln_g[1:2], ln_b[1:2], tm)
    return x, h_re, h_im, kv, kv_extras, attn_extras


def kernel(x_prompt, x_sample, cache_mem_kv, state_ssm_re, state_ssm_im, cache_dil1_kv, cache_dil4_kv,
           cache_dil16_kv, mem_prompt, w_in, w_out, ln_g, ln_b, w_mem_kv, ssm_lambda_re, ssm_lambda_im,
           ssm_log_dt, ssm_b_re, ssm_b_im, ssm_c_re, ssm_c_im, ssm_d, w_glu, b_glu, w_kv_shared):
    bp, seq, d = x_prompt.shape
    bs, steps, _ = x_sample.shape
    prompt_chunk, prompt_tk = S5_CHUNK, PROMPT_TILE
    n_groups = len(DIL_GROUPS)

    w_in_b = w_in.astype(BF16)
    w_out_b = w_out.astype(BF16)
    w_glu_b = w_glu[0].astype(BF16)
    w_kv_b = w_kv_shared.astype(BF16)
    ssm = (ssm_lambda_re[0], ssm_lambda_im[0], ssm_log_dt[0], ssm_b_re[0], ssm_b_im[0],
           ssm_c_re[0], ssm_c_im[0], ssm_d[0])

    assert prompt_chunk % steps == 0
    weights = (w_in_b, w_out_b, ln_g, ln_b, _s5_tables(prompt_chunk, *ssm), w_glu_b, b_glu)

    mem_kv_p = _mem_kv_t(w_mem_kv.transpose(0, 2, 1).astype(BF16), mem_prompt)
    mem_kv_p = mem_kv_p.reshape(DEPTH, bp, 2, MEM_HEADS, HEAD_DIM, N_MEM)
    zeros = jnp.zeros((bp, SSM_GROUPS, SSM_STATE), F32)
    wide = n_groups - 1
    assert DIL_GROUPS[wide][0] >= seq and all(w <= prompt_tk for w, _ in DIL_GROUPS[:wide])

    def kv_prompt(x, w_in_next, splits, side_dtype):
        kv, kvt_all, kvt_tail, proj = _kv_proj(x, w_kv_b, w_in_next, splits, side_dtype, bp, seq, prompt_tk, wide)
        return kv, proj, (kvt_all, kvt_tail)

    def attn_prompt(u, kv, gate):
        q3 = u.reshape(bp, seq, MIX_WIDTH)
        kv3 = kv.reshape(bp, seq, 2 * MIX_WIDTH)
        res = [_dil_prompt(q3, kv3, g) for g in range(n_groups)]
        outs = [r[0].reshape(bp * seq, GROUP_WIDTH) for r in res]
        lses = [r[1].reshape(bp * seq, GROUP_WIDTH) for r in res]
        return _mix_merge, (*outs, *lses, gate), None

    y_p, hre_p, him_p, _, (kvt_all, kvt_tail), _ = _trunk(
        x_prompt.reshape(bp * seq, d), bp, seq, mem_kv_p, zeros, zeros, weights,
        prompt_tk, S5_BLOCK, prompt_chunk, 1, MEM_QUERY_TILE, BF16, kv_prompt, attn_prompt)
    kvt_tail = kvt_tail.reshape(bp, wide, 2, HEADS_PER_GROUP, HEAD_DIM, prompt_tk)
    win_p = [kvt_tail[:, g, :, :, :, prompt_tk - win:].transpose(0, 4, 1, 2, 3)
             for g, (win, _) in enumerate(DIL_GROUPS[:wide])]
    win_p.append(kvt_all.reshape(bp, 2, HEADS_PER_GROUP, HEAD_DIM, seq).transpose(0, 4, 1, 2, 3))

    caches = (cache_dil1_kv, cache_dil4_kv, cache_dil16_kv)
    caches_t = [c.transpose(0, 2, 3, 4, 1) for c in caches]
    mem_kv_s = cache_mem_kv.transpose(0, 1, 3, 4, 5, 2)

    def kv_sample(x, w_in_next, splits, side_dtype):
        (kv,) = _proj(x, w_kv_b, (2 * MIX_WIDTH,), SAMPLE_TILE)
        return kv, _proj(x, w_in_next, splits, SAMPLE_TILE, rest_dtype=side_dtype), None

    def attn_sample(u, kv, gate):
        res = _dil_sample(u.reshape(bs, steps, MIX_WIDTH), kv.reshape(bs, steps, 2 * MIX_WIDTH),
                          gate.reshape(bs, steps, MIX_WIDTH), caches_t)
        return _mix_plain, (res[0].reshape(bs * steps, MIX_WIDTH),), res[1:]

    y_s, hre_s, him_s, _, _, rolled = _trunk(
        x_sample.reshape(bs * steps, d), bs, steps, mem_kv_s, state_ssm_re[0], state_ssm_im[0],
        weights, SAMPLE_TILE, steps, steps, SAMPLE_MEM_BATCH, SUBLANES, F32, kv_sample, attn_sample)
    win_s = [c.transpose(0, 4, 1, 2, 3) for c in rolled]

    mem_kv_out = mem_kv_p.transpose(0, 1, 5, 2, 3, 4)
    return (y_p.reshape(bp, seq, d), y_s.reshape(bs, steps, d), mem_kv_out,
            hre_p[None], him_p[None], win_p[0], win_p[1], win_p[2],
            hre_s[None], him_s[None], win_s[0], win_s[1], win_s[2])
```

```python
import functools
import math

import jax
import jax.numpy as jnp
from jax import lax
from jax.experimental import pallas as pl
from jax.experimental.pallas import tpu as pltpu

F32 = jnp.float32
BF16 = jnp.bfloat16

D_MODEL = 1024
HEAD_DIM = 64
MIX_WIDTH = 768
MEM_WIDTH = 256
MEM_HEADS = 4
N_MEM = 256
SSM_GROUP = 16
SSM_GROUPS = 48
SSM_STATE = 64
DIL_GROUPS = ((128, 1), (512, 4), (2048, 16))
ATT_HEADS = 12
HEADS_PER_GROUP = 4
GROUP_WIDTH = HEADS_PER_GROUP * HEAD_DIM
DEPTH = 2
DEEPNORM_ALPHA = (2.0 * DEPTH) ** 0.25
LN_EPS = 1e-5
SCALE = HEAD_DIM ** -0.5
NEG_INF = -1e30
WINDOW_KEYS = 128
LANES = 128
SUBLANES = 8
SLAB_GROUPS = LANES // SSM_GROUP
SSM_SLABS = SSM_GROUPS // SLAB_GROUPS
SLAB_STATE = SLAB_GROUPS * SSM_STATE
VMEM_LIMIT = 56 * 1024 * 1024
NT_DIMS = (((1,), (1,)), ((), ()))

PROMPT_TILE = 512
SAMPLE_TILE = 256
S5_CHUNK = 8
S5_BLOCK = 512
MEM_QUERY_TILE = 512
SAMPLE_MEM_BATCH = 8
SAMPLE_DIL_BATCH = 2


def _slope(head):
    return 2.0 ** (-8.0 * (head + 1) / ATT_HEADS)


def _params(*semantics):
    return pltpu.CompilerParams(dimension_semantics=semantics or ("parallel",),
                                vmem_limit_bytes=VMEM_LIMIT)


def _silu(x):
    return x * jax.nn.sigmoid(x)


def _round_up(n, m):
    return -(-n // m) * m


def _proj_kernel(x_ref, w_ref, *out_refs, splits):
    x = x_ref[...].astype(BF16)
    off = 0
    for o_ref, n in zip(out_refs, splits):
        o_ref[...] = jnp.dot(x, w_ref[:, off:off + n], preferred_element_type=F32).astype(o_ref.dtype)
        off += n


def _proj(x, w, splits, tm, first_out_block=None, rest_dtype=F32):
    m, k = x.shape
    n_total = w.shape[1]
    assert sum(splits) == n_total and m % tm == 0
    tok = lambda i: (i, 0)
    first = tok if first_out_block is None else (lambda i: (first_out_block(i), 0))
    out_maps = [first] + [tok] * (len(splits) - 1)
    return pl.pallas_call(
        functools.partial(_proj_kernel, splits=splits),
        grid=(m // tm,),
        in_specs=[pl.BlockSpec((tm, k), tok), pl.BlockSpec((k, n_total), lambda i: (0, 0))],
        out_specs=[pl.BlockSpec((tm, n), om) for n, om in zip(splits, out_maps)],
        out_shape=[jax.ShapeDtypeStruct((m, n), F32 if i == 0 else rest_dtype) for i, n in enumerate(splits)],
        compiler_params=_params(),
        name="proj",
    )(x, w)


def _kv_proj_kernel(x_ref, w_kv_ref, w_in_ref, kv_ref, kvt_all_ref, kvt_tail_ref, *proj_refs,
                    tiles, wide, splits):
    x = x_ref[...].astype(BF16)
    kv = jnp.dot(x, w_kv_ref[...], preferred_element_type=F32)
    kv_ref[...] = kv

    def group_t(g):
        cols = [kv[:, i * MIX_WIDTH + g * GROUP_WIDTH:i * MIX_WIDTH + (g + 1) * GROUP_WIDTH] for i in range(2)]
        return jnp.concatenate(cols, axis=1).T

    kvt_all_ref[0] = group_t(wide)

    @pl.when(pl.program_id(0) % tiles == tiles - 1)
    def _():
        for g in range(wide):
            kvt_tail_ref[0, g * 2 * GROUP_WIDTH:(g + 1) * 2 * GROUP_WIDTH, :] = group_t(g)

    off = 0
    for o_ref, n in zip(proj_refs, splits):
        o_ref[...] = jnp.dot(x, w_in_ref[:, off:off + n], preferred_element_type=F32).astype(o_ref.dtype)
        off += n


def _kv_proj(x, w_kv, w_in, splits, rest_dtype, batch, seq, tm, wide):
    m, k = x.shape
    n_kv, n_in = w_kv.shape[1], w_in.shape[1]
    group_rows = 2 * GROUP_WIDTH
    tiles = seq // tm
    tok = lambda i: (i, 0)
    fixed = lambda i: (0, 0)
    res = pl.pallas_call(
        functools.partial(_kv_proj_kernel, tiles=tiles, wide=wide, splits=splits),
        grid=(m // tm,),
        in_specs=[pl.BlockSpec((tm, k), tok), pl.BlockSpec((k, n_kv), fixed), pl.BlockSpec((k, n_in), fixed)],
        out_specs=[pl.BlockSpec((tm, n_kv), tok),
                   pl.BlockSpec((1, group_rows, tm), lambda i: (i // tiles, 0, i % tiles)),
                   pl.BlockSpec((1, wide * group_rows, tm), lambda i: (i // tiles, 0, 0))]
        + [pl.BlockSpec((tm, n), tok) for n in splits],
        out_shape=[jax.ShapeDtypeStruct((m, n_kv), F32),
                   jax.ShapeDtypeStruct((batch, group_rows, seq), F32),
                   jax.ShapeDtypeStruct((batch, wide * group_rows, tm), F32)]
        + [jax.ShapeDtypeStruct((m, n), F32 if i == 0 else rest_dtype) for i, n in enumerate(splits)],
        compiler_params=_params("arbitrary"),
        name="kv_proj",
    )(x, w_kv, w_in)
    return res[0], res[1], res[2], res[3:]


def _mem_kv_t_kernel(w_ref, mem_ref, o_ref):
    o_ref[0, 0] = lax.dot_general(w_ref[0], mem_ref[0].astype(BF16), NT_DIMS, preferred_element_type=F32)


def _mem_kv_t(w_t, mem):
    depth, n, d = w_t.shape
    b = mem.shape[0]
    return pl.pallas_call(
        _mem_kv_t_kernel,
        grid=(depth, b),
        in_specs=[pl.BlockSpec((1, n, d), lambda l, i: (l, 0, 0)),
                  pl.BlockSpec((1, N_MEM, d), lambda l, i: (i, 0, 0))],
        out_specs=pl.BlockSpec((1, 1, n, N_MEM), lambda l, i: (l, i, 0, 0)),
        out_shape=jax.ShapeDtypeStruct((depth, b, n, N_MEM), F32),
        compiler_params=_params("parallel", "parallel"),
        name="mem_kv_t",
    )(w_t, mem)


def _s5_tables(chunk, lam_re, lam_im, log_dt, b_re, b_im, c_re, c_im, d_skip):
    hp = lax.Precision.HIGHEST
    p, c = SSM_STATE, SSM_GROUP
    ns, gs = SSM_SLABS, SLAB_GROUPS
    kl = chunk * LANES
    lr = jnp.minimum(lam_re.astype(F32), -1e-4)
    li = lam_im.astype(F32)
    dt = jnp.exp(log_dt.astype(F32))[:, None]
    def lam_bar_pow(n):
        n = n.astype(F32)[None, :, None]
        mag = jnp.exp(lr[:, None, :] * dt[:, None, :] * n)
        ang = li[:, None, :] * dt[:, None, :] * n
        return mag * jnp.cos(ang), mag * jnp.sin(ang)

    pw_re, pw_im = lam_bar_pow(jnp.arange(chunk + 1))
    nr, ni = pw_re[:, 1] - 1.0, pw_im[:, 1]
    den = lr * lr + li * li
    f_re, f_im = (nr * lr + ni * li) / den, (ni * lr - nr * li) / den
    bb_re = f_re[:, :, None] * b_re - f_im[:, :, None] * b_im
    bb_im = f_re[:, :, None] * b_im + f_im[:, :, None] * b_re
    c_re, c_im = c_re.astype(F32), c_im.astype(F32)

    x_re = pw_re[:, :chunk, :, None] * bb_re[:, None] - pw_im[:, :chunk, :, None] * bb_im[:, None]
    x_im = pw_re[:, :chunk, :, None] * bb_im[:, None] + pw_im[:, :chunk, :, None] * bb_re[:, None]
    conv = (jnp.einsum('gcp,gtpd->gtcd', c_re, x_re, precision=hp)
            - jnp.einsum('gcp,gtpd->gtcd', c_im, x_im, precision=hp))
    def slab_diag(x):
        r, w = x.shape[-2:]
        x = jnp.moveaxis(x, 1, -3)
        x = x.reshape(x.shape[:-3] + (gs * r, w))
        x = jnp.tile(x, (1,) * (x.ndim - 1) + (gs,))
        own = (jnp.arange(gs * r) // r)[:, None] == (jnp.arange(gs * w) // w)[None, :]
        return jnp.where(own, x, 0.0)

    conv_d = slab_diag(conv.transpose(0, 1, 3, 2).reshape(ns, gs, chunk, c, c))
    b_diag = lambda x: slab_diag(x.transpose(0, 2, 1).reshape(ns, gs, c, p))
    c_diag = lambda x: slab_diag(x.transpose(0, 2, 1).reshape(ns, gs, p, c))

    def slab_rows(x):
        n = x.shape[1]
        return x.reshape(ns, gs, n, p).transpose(0, 2, 1, 3).reshape(ns, n, SLAB_STATE)

    strip = conv_d.transpose(0, 2, 1, 3).reshape(ns, LANES, kl)
    toep = jnp.stack([jnp.pad(strip[:, :, :kl - s * LANES], ((0, 0), (0, 0), (s * LANES, 0)))
                      for s in range(chunk)], axis=1).reshape(ns, kl, kl)

    rev_re, rev_im = lam_bar_pow(chunk - 1 - jnp.arange(chunk))
    rev_re, rev_im = slab_rows(rev_re), slab_rows(rev_im)
    bd_re, bd_im = b_diag(bb_re), b_diag(bb_im)
    rev_a = jnp.concatenate([rev_re, rev_re], axis=-1)[:, :, None, :]
    rev_b = jnp.concatenate([-rev_im, rev_im], axis=-1)[:, :, None, :]
    bd_a = jnp.concatenate([bd_re, bd_im], axis=-1)[:, None]
    bd_b = jnp.concatenate([bd_im, bd_re], axis=-1)[:, None]
    w_in = (rev_a * bd_a + rev_b * bd_b).reshape(ns, kl, 2 * SLAB_STATE)

    col_re = slab_rows(pw_re[:, 1:]).transpose(0, 2, 1)
    col_im = slab_rows(pw_im[:, 1:]).transpose(0, 2, 1)
    cd_re, cd_im = c_diag(c_re), c_diag(c_im)
    col_a = jnp.concatenate([col_re, -col_re], axis=1)
    col_b = jnp.concatenate([-col_im, -col_im], axis=1)
    cd_a = jnp.concatenate([cd_re, cd_im], axis=1)
    cd_b = jnp.concatenate([cd_im, cd_re], axis=1)
    w_out = jnp.concatenate([col_a[:, :, t:t + 1] * cd_a + col_b[:, :, t:t + 1] * cd_b
                             for t in range(chunk)], axis=-1)

    lam_pows = jnp.stack([slab_rows(pw_re).transpose(1, 0, 2),
                          slab_rows(pw_im).transpose(1, 0, 2)], axis=2)
    skip = jnp.tile(d_skip.astype(F32).reshape(ns, LANES), (1, chunk))[:, None]
    return toep.astype(BF16), w_in.astype(BF16), w_out.astype(BF16), lam_pows, skip


def _s5_kernel(u_ref, toep_ref, win_ref, wout_ref, lam_ref, skip_ref, h0_ref, y_ref, hout_ref,
               a_scr, g_scr, hs_scr, y_scr, h_scr, *, batch, tk, chunk):
    n_chunks = tk // chunk
    ns = SLAB_STATE

    @pl.when(pl.program_id(1) == 0)
    def _():
        h_scr[...] = h0_ref[0]

    tile_swap = batch == SUBLANES and chunk == SUBLANES

    def gather(kk, carry):
        rows = pl.ds(pl.multiple_of(kk * batch, SUBLANES), batch)
        if tile_swap:
            tiles = [u_ref[pl.ds(pl.multiple_of(b * tk + kk * chunk, SUBLANES), chunk), :] for b in range(batch)]
            by_token = pltpu.einshape("btl->tbl", jnp.stack(tiles, axis=0))
            for t in range(chunk):
                a_scr[rows, t * LANES:(t + 1) * LANES] = by_token[t]
            return carry
        for t in range(chunk):
            a_scr[rows, t * LANES:(t + 1) * LANES] = u_ref[pl.ds(kk * chunk + t, batch, stride=tk), :]
        return carry

    lax.fori_loop(0, n_chunks, gather, 0)
    a = a_scr[...].astype(BF16)
    g_scr[...] = jnp.dot(a, win_ref[0], preferred_element_type=F32)
    ar = lam_ref[0, 0:1, :]
    ai = lam_ref[0, 1:2, :]

    def step(kk, carry):
        hr, hi = carry
        rows = pl.ds(pl.multiple_of(kk * batch, SUBLANES), batch)
        hs_scr[rows, 0:ns] = hr
        hs_scr[rows, ns:2 * ns] = hi
        return (ar * hr - ai * hi + g_scr[rows, 0:ns], ar * hi + ai * hr + g_scr[rows, ns:2 * ns])

    hr, hi = lax.fori_loop(0, n_chunks, step, (h_scr[:, 0:ns], h_scr[:, ns:2 * ns]))
    h_scr[:, 0:ns] = hr
    h_scr[:, ns:2 * ns] = hi
    hout_ref[0, :, 0:ns] = hr
    hout_ref[0, :, ns:2 * ns] = hi
    y = (jnp.dot(a, toep_ref[0], preferred_element_type=F32)
         + jnp.dot(hs_scr[...].astype(BF16), wout_ref[0], preferred_element_type=F32)
         + skip_ref[0] * a_scr[...])
    y_scr[...] = jax.nn.gelu(y)

    def scatter(kk, carry):
        rows = pl.ds(pl.multiple_of(kk * batch, SUBLANES), batch)
        if tile_swap:
            by_token = jnp.stack([y_scr[rows, t * LANES:(t + 1) * LANES] for t in range(chunk)], axis=0)
            by_seq = pltpu.einshape("tbl->btl", by_token)
            for b in range(batch):
                y_ref[pl.ds(pl.multiple_of(b * tk + kk * chunk, SUBLANES), chunk), :] = by_seq[b]
            return carry
        for t in range(chunk):
            y_ref[pl.ds(kk * chunk + t, batch, stride=tk), :] = y_scr[rows, t * LANES:(t + 1) * LANES]
        return carry

    lax.fori_loop(0, n_chunks, scatter, 0)


def _s5_mixer(u, h0_re, h0_im, tables, batch, seq, tk, chunk):
    toep, w_in, w_out, lam_pows, skip = tables
    kl = chunk * LANES
    table_chunk = toep.shape[-1] // LANES
    assert (table_chunk - chunk) % chunk == 0
    w_in_block = (table_chunk - chunk) // chunk
    lam_l = lam_pows[chunk]
    n_blocks = seq // tk
    rows = (tk // chunk) * batch
    to_slabs = lambda h: h.reshape(batch, SSM_SLABS, SLAB_STATE).transpose(1, 0, 2)
    h0 = jnp.concatenate([to_slabs(h0_re), to_slabs(h0_im)], axis=-1)
    slab = lambda j, k: (j, 0, 0)
    y, h = pl.pallas_call(
        functools.partial(_s5_kernel, batch=batch, tk=tk, chunk=chunk),
        grid=(SSM_SLABS, n_blocks),
        in_specs=[pl.BlockSpec((batch * tk, LANES), lambda j, k: (k, j)),
                  pl.BlockSpec((1, kl, kl), slab),
                  pl.BlockSpec((1, kl, 2 * SLAB_STATE), lambda j, k: (j, w_in_block, 0)),
                  pl.BlockSpec((1, 2 * SLAB_STATE, kl), slab),
                  pl.BlockSpec((1, 2, SLAB_STATE), slab),
                  pl.BlockSpec((1, 1, kl), slab),
                  pl.BlockSpec((1, batch, 2 * SLAB_STATE), slab)],
        out_specs=[pl.BlockSpec((batch * tk, LANES), lambda j, k: (k, j)),
                   pl.BlockSpec((1, batch, 2 * SLAB_STATE), slab)],
        out_shape=[jax.ShapeDtypeStruct(u.shape, F32),
                   jax.ShapeDtypeStruct((SSM_SLABS, batch, 2 * SLAB_STATE), F32)],
        scratch_shapes=[pltpu.VMEM((rows, kl), F32), pltpu.VMEM((rows, 2 * SLAB_STATE), F32),
                        pltpu.VMEM((rows, 2 * SLAB_STATE), F32), pltpu.VMEM((rows, kl), F32),
                        pltpu.VMEM((batch, 2 * SLAB_STATE), F32)],
        compiler_params=_params("parallel", "arbitrary"),
        name="s5_chunks",
    )(u, toep, w_in, w_out, lam_l, skip, h0)
    from_slabs = lambda x: x.transpose(1, 0, 2).reshape(batch, SSM_GROUPS, SSM_STATE)
    return y, from_slabs(h[:, :, :SLAB_STATE]), from_slabs(h[:, :, SLAB_STATE:])


def _head_rows(x, heads):
    t, w = x.shape
    row_head = lax.broadcasted_iota(jnp.int32, (heads * t, w), 0) // t
    lane_head = lax.broadcasted_iota(jnp.int32, (heads * t, w), 1) // HEAD_DIM
    mask = row_head == lane_head
    return jnp.where(mask, jnp.concatenate([x] * heads, axis=0), 0.0), mask


def _head_cols(x, mask, heads):
    t = x.shape[0] // heads
    x = jnp.where(mask, x, 0.0)
    out = x[0:t]
    for h in range(1, heads):
        out = out + x[h * t:(h + 1) * t]
    return out


def _mem_attn_kernel(q_ref, kv_ref, mg_ref, o_ref, *, block_batch):
    for bi in range(block_batch):
        q_bd, mask = _head_rows(q_ref[bi].astype(F32), MEM_HEADS)
        kt = jnp.concatenate([kv_ref[0, bi, 0, h] for h in range(MEM_HEADS)], axis=0).astype(BF16)
        vt = jnp.concatenate([kv_ref[0, bi, 1, h] for h in range(MEM_HEADS)], axis=0).astype(BF16)
        s = jnp.dot(q_bd.astype(BF16), kt, preferred_element_type=F32) * SCALE
        p = jnp.exp(s - jnp.max(s, axis=-1, keepdims=True))
        l = jnp.sum(p, axis=-1, keepdims=True)
        o = lax.dot_general(p.astype(BF16), vt, NT_DIMS, preferred_element_type=F32) / l
        o_ref[bi] = (_head_cols(o, mask, MEM_HEADS) * _silu(mg_ref[bi].astype(F32))).astype(o_ref.dtype)


def _mem_attn(q, kv_t, layer, mgate, block_batch, tq):
    b, t, w = q.shape
    tok = lambda i, j: (i, j, 0)
    return pl.pallas_call(
        functools.partial(_mem_attn_kernel, block_batch=block_batch),
        grid=(b // block_batch, t // tq),
        in_specs=[pl.BlockSpec((block_batch, tq, w), tok),
                  pl.BlockSpec((1, block_batch) + kv_t.shape[2:], lambda i, j: (layer, i, 0, 0, 0, 0)),
                  pl.BlockSpec((block_batch, tq, w), tok)],
        out_specs=pl.BlockSpec((block_batch, tq, w), tok),
        out_shape=jax.ShapeDtypeStruct((b, t, w), q.dtype),
        compiler_params=_params("parallel", "parallel"),
        name="mem_attn",
    )(q, kv_t, mgate)


def _mix_plain(tok_refs, fixed_refs):
    (y_ref,) = tok_refs
    return y_ref[...]


def _mix_glu(tok_refs, fixed_refs):
    y_ref, gate_ref = tok_refs
    w_ref, b_ref = fixed_refs
    y = y_ref[...]
    z = jnp.dot(y.astype(BF16), w_ref[...], preferred_element_type=F32) + b_ref[...]
    return y * jax.nn.sigmoid(z) * _silu(gate_ref[...].astype(F32))


def _mix_merge(tok_refs, fixed_refs):
    o0, o1, o2, l0, l1, l2, gate_ref = tok_refs
    ls = [l0[...], l1[...], l2[...]]
    mx = jnp.maximum(jnp.maximum(ls[0], ls[1]), ls[2])
    es = [jnp.exp(l - mx) for l in ls]
    den = es[0] + es[1] + es[2]
    o = jnp.concatenate([o_ref[...] * (e / den) for o_ref, e in zip((o0, o1, o2), es)], axis=1)
    return o * _silu(gate_ref[...].astype(F32))


def _layer_tail_kernel(*refs, n_tok, n_fixed, mix_fn):
    mix = mix_fn(refs[:n_tok], refs[n_tok:n_tok + n_fixed])
    m_ref, x_ref, wy_ref, wm_ref, g_ref, b_ref, o_ref = refs[n_tok + n_fixed:]
    out = (jnp.dot(mix.astype(BF16), wy_ref[...], preferred_element_type=F32)
           + jnp.dot(m_ref[...].astype(BF16), wm_ref[...], preferred_element_type=F32))
    z = DEEPNORM_ALPHA * x_ref[...] + out
    mu = jnp.mean(z, axis=-1, keepdims=True)
    zc = z - mu
    var = jnp.mean(zc * zc, axis=-1, keepdims=True)
    o_ref[...] = zc * lax.rsqrt(var + LN_EPS) * g_ref[...] + b_ref[...]


def _layer_tail(mix_fn, tok_arrays, fixed_arrays, m, x, w_out, ln_g, ln_b, tm, first_block=None):
    rows, d = x.shape
    tok = lambda i: (i, 0)
    fixed = lambda i: (0, 0)
    first = tok if first_block is None else (lambda i: (first_block(i), 0))
    tok_maps = [first] + [tok] * (len(tok_arrays) - 1)
    return pl.pallas_call(
        functools.partial(_layer_tail_kernel, n_tok=len(tok_arrays), n_fixed=len(fixed_arrays), mix_fn=mix_fn),
        grid=(rows // tm,),
        in_specs=[pl.BlockSpec((tm, a.shape[1]), tm_map) for a, tm_map in zip(tok_arrays, tok_maps)]
        + [pl.BlockSpec(a.shape, fixed) for a in fixed_arrays]
        + [pl.BlockSpec((tm, MEM_WIDTH), tok), pl.BlockSpec((tm, d), tok),
           pl.BlockSpec((MIX_WIDTH, d), fixed), pl.BlockSpec((MEM_WIDTH, d), fixed),
           pl.BlockSpec((1, d), fixed), pl.BlockSpec((1, d), fixed)],
        out_specs=pl.BlockSpec((tm, d), tok),
        out_shape=jax.ShapeDtypeStruct((rows, d), F32),
        compiler_params=_params(),
        name="layer_tail",
    )(*tok_arrays, *fixed_arrays, m, x, w_out[:MIX_WIDTH], w_out[MIX_WIDTH:], ln_g, ln_b)


def _dil_prompt_kernel(q_ref, k_ref, v_ref, o_ref, lse_ref, *, seq, dil, group):
    n = WINDOW_KEYS
    n_blocks = seq // dil // n
    heads = LANES // HEAD_DIM
    first_head = group * HEADS_PER_GROUP
    second_pair = pl.program_id(1) == 1
    slopes = [jnp.where(second_pair, _slope(first_head + heads + h), _slope(first_head + h))
              for h in range(heads)]
    ku = lax.broadcasted_iota(jnp.int32, (2 * n, n), 0)
    qi = lax.broadcasted_iota(jnp.int32, (2 * n, n), 1)
    delta2 = qi + n - ku
    valid2 = jnp.where(delta2 >= 0, delta2, n + 1) <= n
    delta1 = delta2[n:, :]
    valid1 = delta1 >= 0
    bias2 = [s * (delta2 * dil).astype(F32) for s in slopes]
    bias1 = [s * (delta1 * dil).astype(F32) for s in slopes]
    lane_head = lax.broadcasted_iota(jnp.int32, (1, LANES), 1) // HEAD_DIM
    head_lanes = [(lane_head == h).astype(F32) for h in range(heads)]
    tn_dims = (((0,), (0,)), ((), ()))

    def stream_rows(start):
        return pl.ds(start, n, stride=dil) if dil > 1 else pl.ds(start, n)

    def block(start, has_prev):
        cur = stream_rows(start)
        q = q_ref[0, cur, :]
        k = k_ref[0, cur, :]
        v = v_ref[0, cur, :]
        if has_prev:
            prev = stream_rows(start - n * dil)
            k = jnp.concatenate([k_ref[0, prev, :], k], axis=0)
            v = jnp.concatenate([v_ref[0, prev, :], v], axis=0)
            bias, valid = bias2, valid2
        else:
            bias, valid = bias1, valid1
        q = q * SCALE
        o = jnp.zeros((n, LANES), F32)
        lse_b = jnp.zeros((n, LANES), F32)
        k = k.astype(BF16)
        for h in range(heads):
            s = lax.dot_general(k, (q * head_lanes[h]).astype(BF16),
                                NT_DIMS, preferred_element_type=F32)
            s = jnp.where(valid, s - bias[h], NEG_INF)
            m = jnp.max(s, axis=0, keepdims=True)
            p = jnp.exp(s - m)
            l = jnp.sum(p, axis=0, keepdims=True)
            p = (p * (1.0 / l)).astype(BF16)
            o = o + lax.dot_general(p, (v * head_lanes[h]).astype(BF16), tn_dims, preferred_element_type=F32)
            lse = m + jnp.log(l)
            t1 = lse.astype(BF16)
            r1 = lse - t1.astype(F32)
            t2 = r1.astype(BF16)
            t3 = (r1 - t2.astype(F32)).astype(BF16)
            terms = jnp.concatenate([t1, t2, t3, jnp.zeros((SUBLANES - 3, n), BF16)], axis=0)
            spread = jnp.broadcast_to(head_lanes[h], (SUBLANES, LANES)).astype(BF16)
            lse_b = lse_b + lax.dot_general(terms, spread, tn_dims, preferred_element_type=F32)
        o_ref[0, cur, :] = o
        lse_ref[0, cur, :] = lse_b

    for r in range(dil):
        for jb in range(n_blocks):
            block(r + jb * n * dil, jb > 0)


def _dil_prompt(q, kv, group):
    b, seq, _ = q.shape
    _, dil = DIL_GROUPS[group]
    pairs = GROUP_WIDTH // LANES
    k_col = group * pairs
    v_col = MIX_WIDTH // LANES + k_col
    blk = (1, seq, LANES)
    return pl.pallas_call(
        functools.partial(_dil_prompt_kernel, seq=seq, dil=dil, group=group),
        grid=(b, pairs),
        in_specs=[pl.BlockSpec(blk, lambda i, j: (i, 0, k_col + j)),
                  pl.BlockSpec(blk, lambda i, j: (i, 0, k_col + j)),
                  pl.BlockSpec(blk, lambda i, j: (i, 0, v_col + j))],
        out_specs=[pl.BlockSpec(blk, lambda i, j: (i, 0, j)), pl.BlockSpec(blk, lambda i, j: (i, 0, j))],
        out_shape=[jax.ShapeDtypeStruct((b, seq, GROUP_WIDTH), F32)] * 2,
        compiler_params=_params("parallel", "parallel"),
        name=f"dil_prompt_g{group}",
    )(q, kv, kv)


def _dil_sample_kernel(q_ref, kvn_ref, gate_ref, c0_ref, c1_ref, c2_ref,
                       y_ref, n0_ref, n1_ref, n2_ref, *, steps):
    rows = q_ref.shape[1]
    lead = rows - steps
    q_all = q_ref[0]
    kvn = kvn_ref[0]
    hg = HEADS_PER_GROUP
    stacked = lax.broadcasted_iota(jnp.int32, (hg * rows, 1), 0)
    row_head = stacked // rows
    tok = stacked - row_head * rows - lead
    kvn_t = jnp.concatenate([jnp.zeros((LANES - rows, 2 * MIX_WIDTH), F32), kvn], axis=0).T
    tail_lane = lax.broadcasted_iota(jnp.int32, (HEAD_DIM, LANES), 1) >= LANES - steps
    outs, lses = [], []
    for g, ((win, dil), c_ref, n_ref) in enumerate(zip(DIL_GROUPS, (c0_ref, c1_ref, c2_ref),
                                                       (n0_ref, n1_ref, n2_ref))):
        cols = slice(g * GROUP_WIDTH, (g + 1) * GROUP_WIDTH)
        k_new = kvn[:, cols]
        v_new = kvn[:, MIX_WIDTH + g * GROUP_WIDTH:MIX_WIDTH + (g + 1) * GROUP_WIDTH]
        kv_heads = [[c_ref[0, kv_i, h] for h in range(hg)] for kv_i in range(2)]
        for kv_i in range(2):
            for h in range(hg):
                shifted = pltpu.roll(kv_heads[kv_i][h], win - steps, axis=1)
                lo = kv_i * MIX_WIDTH + g * GROUP_WIDTH + h * HEAD_DIM
                new_t = kvn_t[lo:lo + HEAD_DIM, :]
                if win > LANES:
                    n_ref[0, kv_i, h, :, 0:win - LANES] = shifted[:, 0:win - LANES]
                n_ref[0, kv_i, h, :, win - LANES:win] = jnp.where(tail_lane, new_t, shifted[:, win - LANES:win])
        kt = jnp.concatenate(kv_heads[0], axis=0).astype(BF16)
        vt = jnp.concatenate(kv_heads[1], axis=0).astype(BF16)
        q_bd, head_mask = _head_rows(q_all[:, cols], hg)
        slope = jnp.zeros((hg * rows, 1), F32)
        for h in range(hg):
            slope = jnp.where(row_head == h, _slope(g * hg + h), slope)
        pos = lax.broadcasted_iota(jnp.int32, (hg * rows, win), 1)
        dist_buf = ((win + tok) - pos).astype(F32)
        valid_buf = jnp.where(pos >= tok, (pos - tok) & (dil - 1), 1) == 0
        s_buf = jnp.dot(q_bd.astype(BF16), kt, preferred_element_type=F32) * SCALE - slope * dist_buf
        s_buf = jnp.where(valid_buf, s_buf, NEG_INF)
        s_new = []
        for t2 in range(steps):
            col = jnp.sum(q_bd * k_new[lead + t2:lead + t2 + 1, :], axis=-1, keepdims=True) * SCALE \
                - slope * (tok - t2).astype(F32)
            ok = jnp.where(tok >= t2, (tok - t2) & (dil - 1), 1) == 0
            s_new.append(jnp.where(ok, col, NEG_INF))
        m = jnp.max(s_buf, axis=-1, keepdims=True)
        for col in s_new:
            m = jnp.maximum(m, col)
        p_buf = jnp.exp(s_buf - m)
        l = jnp.sum(p_buf, axis=-1, keepdims=True)
        o = lax.dot_general(p_buf.astype(BF16), vt, NT_DIMS, preferred_element_type=F32)
        for t2, col in enumerate(s_new):
            p_col = jnp.exp(col - m)
            l = l + p_col
            o = o + p_col * v_new[lead + t2:lead + t2 + 1, :]
        outs.append(o / l)
        lses.append(m + jnp.log(l))
    mx = jnp.maximum(jnp.maximum(lses[0], lses[1]), lses[2])
    es = [jnp.exp(l - mx) for l in lses]
    den = es[0] + es[1] + es[2]
    mixed = [_head_cols(outs[g] * (es[g] / den), head_mask, hg) for g in range(len(DIL_GROUPS))]
    y_ref[0] = jnp.concatenate(mixed, axis=1) * _silu(gate_ref[0])


def _dil_sample_block_kernel(*refs, steps, block_batch):
    for bi in range(block_batch):
        _dil_sample_kernel(*[r.at[pl.ds(bi, 1)] for r in refs], steps=steps)


def _dil_sample(q, kv_new, gate, caches_t, block_batch=SAMPLE_DIL_BATCH):
    b, steps, _ = q.shape
    block_batch = math.gcd(b, block_batch)
    rows = _round_up(steps, SUBLANES)
    pad = ((0, 0), (rows - steps, 0), (0, 0))
    q, kv_new, gate = jnp.pad(q, pad), jnp.pad(kv_new, pad), jnp.pad(gate, pad)
    tok = lambda i: (i, 0, 0)
    cache_specs = [pl.BlockSpec((block_batch,) + c.shape[1:], lambda i: (i, 0, 0, 0, 0)) for c in caches_t]
    res = pl.pallas_call(
        functools.partial(_dil_sample_block_kernel, steps=steps, block_batch=block_batch),
        grid=(b // block_batch,),
        in_specs=[pl.BlockSpec((block_batch, rows, MIX_WIDTH), tok),
                  pl.BlockSpec((block_batch, rows, 2 * MIX_WIDTH), tok),
                  pl.BlockSpec((block_batch, rows, MIX_WIDTH), tok)] + cache_specs,
        out_specs=[pl.BlockSpec((block_batch, rows, MIX_WIDTH), tok)] + cache_specs,
        out_shape=[jax.ShapeDtypeStruct((b, rows, MIX_WIDTH), F32)]
        + [jax.ShapeDtypeStruct(c.shape, F32) for c in caches_t],
        compiler_params=_params(),
        name="dil_sample",
    )(q, kv_new, gate, *caches_t)
    return [res[0][:, rows - steps:]] + list(res[1:])


def _trunk(x, batch, seq, mem_kv_t, h0_re, h0_im, weights, tm, s5_tk, s5_chunk, mem_block, mem_tq,
           side_dtype, kv_proj, dil_attn):
    (w_in, w_out, ln_g, ln_b, tables, w_glu, b_glu) = weights
    splits = (MIX_WIDTH, MIX_WIDTH, MEM_WIDTH, MEM_WIDTH)
    seq_mem = _round_up(seq, SUBLANES)

    def mem_branch(mq, mgate, layer):
        mq = mq.reshape(batch, seq, MEM_WIDTH)
        mgate = mgate.reshape(batch, seq, MEM_WIDTH)
        if seq_mem != seq:
            pad = ((0, 0), (0, seq_mem - seq), (0, 0))
            mq, mgate = jnp.pad(mq, pad), jnp.pad(mgate, pad)
        m = _mem_attn(mq, mem_kv_t, layer, mgate, mem_block, min(mem_tq, seq_mem))
        return m[:, :seq].reshape(batch * seq, MEM_WIDTH)

    if s5_tk == seq:
        block_of = None
    else:
        assert s5_tk % tm == 0 and seq % s5_tk == 0
        per_seq, per_block = seq // tm, s5_tk // tm

        def block_of(i):
            b, t = i // per_seq, i % per_seq
            return ((t // per_block) * batch + b) * per_block + t % per_block
    u, gate, mq, mgate = _proj(x, w_in[0], splits, tm, first_out_block=block_of, rest_dtype=side_dtype)
    y, h_re, h_im = _s5_mixer(u, h0_re, h0_im, tables, batch, seq, s5_tk, s5_chunk)
    m = mem_branch(mq, mgate, 0)
    x = _layer_tail(_mix_glu, (y, gate), (w_glu, b_glu), m, x, w_out[0], ln_g[0:1], ln_b[0:1], tm,
                    first_block=block_of)
    kv, (u, gate, mq, mgate), kv_extras = kv_proj(x, w_in[1], splits, side_dtype)

    mix_fn, mix_inputs, attn_extras = dil_attn(u, kv, gate)
    m = mem_branch(mq, mgate, 1)
    x = _layer_tail(mix_fn, mix_inputs, (), m, x, w_out[1], ln_g[1:2], ln_b[1:2], tm)
    return x, h_re, h_im, kv, kv_extras, attn_extras


def kernel(x_prompt, x_sample, cache_mem_kv, state_ssm_re, state_ssm_im, cache_dil1_kv, cache_dil4_kv,
           cache_dil16_kv, mem_prompt, w_in, w_out, ln_g, ln_b, w_mem_kv, ssm_lambda_re, ssm_lambda_im,
           ssm_log_dt, ssm_b_re, ssm_b_im, ssm_c_re, ssm_c_im, ssm_d, w_glu, b_glu, w_kv_shared):
    bp, seq, d = x_prompt.shape
    bs, steps, _ = x_sample.shape
    prompt_chunk, prompt_tk = S5_CHUNK, PROMPT_TILE
    n_groups = len(DIL_GROUPS)

    w_in_b = w_in.astype(BF16)
    w_out_b = w_out.astype(BF16)
    w_glu_b = w_glu[0].astype(BF16)
    w_kv_b = w_kv_shared.astype(BF16)
    ssm = (ssm_lambda_re[0], ssm_lambda_im[0], ssm_log_dt[0], ssm_b_re[0], ssm_b_im[0],
           ssm_c_re[0], ssm_c_im[0], ssm_d[0])

    assert prompt_chunk % steps == 0
    weights = (w_in_b, w_out_b, ln_g, ln_b, _s5_tables(prompt_chunk, *ssm), w_glu_b, b_glu)

    mem_kv_p = _mem_kv_t(w_mem_kv.transpose(0, 2, 1).astype(BF16), mem_prompt)
    mem_kv_p = mem_kv_p.reshape(DEPTH, bp, 2, MEM_HEADS, HEAD_DIM, N_MEM)
    zeros = jnp.zeros((bp, SSM_GROUPS, SSM_STATE), F32)
    wide = n_groups - 1
    assert DIL_GROUPS[wide][0] >= seq and all(w <= prompt_tk for w, _ in DIL_GROUPS[:wide])

    def kv_prompt(x, w_in_next, splits, side_dtype):
        kv, kvt_all, kvt_tail, proj = _kv_proj(x, w_kv_b, w_in_next, splits, side_dtype, bp, seq, prompt_tk, wide)
        return kv, proj, (kvt_all, kvt_tail)

    def attn_prompt(u, kv, gate):
        q3 = u.reshape(bp, seq, MIX_WIDTH)
        kv3 = kv.reshape(bp, seq, 2 * MIX_WIDTH)
        res = [_dil_prompt(q3, kv3, g) for g in range(n_groups)]
        outs = [r[0].reshape(bp * seq, GROUP_WIDTH) for r in res]
        lses = [r[1].reshape(bp * seq, GROUP_WIDTH) for r in res]
        return _mix_merge, (*outs, *lses, gate), None

    y_p, hre_p, him_p, _, (kvt_all, kvt_tail), _ = _trunk(
        x_prompt.reshape(bp * seq, d), bp, seq, mem_kv_p, zeros, zeros, weights,
        prompt_tk, S5_BLOCK, prompt_chunk, 1, MEM_QUERY_TILE, BF16, kv_prompt, attn_prompt)
    kvt_tail = kvt_tail.reshape(bp, wide, 2, HEADS_PER_GROUP, HEAD_DIM, prompt_tk)
    win_p = [kvt_tail[:, g, :, :, :, prompt_tk - win:].transpose(0, 4, 1, 2, 3)
             for g, (win, _) in enumerate(DIL_GROUPS[:wide])]
    win_p.append(kvt_all.reshape(bp, 2, HEADS_PER_GROUP, HEAD_DIM, seq).transpose(0, 4, 1, 2, 3))

    caches = (cache_dil1_kv, cache_dil4_kv, cache_dil16_kv)
    caches_t = [c.transpose(0, 2, 3, 4, 1) for c in caches]
    mem_kv_s = cache_mem_kv.transpose(0, 1, 3, 4, 5, 2)

    def kv_sample(x, w_in_next, splits, side_dtype):
        (kv,) = _proj(x, w_kv_b, (2 * MIX_WIDTH,), SAMPLE_TILE)
        return kv, _proj(x, w_in_next, splits, SAMPLE_TILE, rest_dtype=side_dtype), None

    def attn_sample(u, kv, gate):
        res = _dil_sample(u.reshape(bs, steps, MIX_WIDTH), kv.reshape(bs, steps, 2 * MIX_WIDTH),
                          gate.reshape(bs, steps, MIX_WIDTH), caches_t)
        return _mix_plain, (res[0].reshape(bs * steps, MIX_WIDTH),), res[1:]

    y_s, hre_s, him_s, _, _, rolled = _trunk(
        x_sample.reshape(bs * steps, d), bs, steps, mem_kv_s, state_ssm_re[0], state_ssm_im[0],
        weights, SAMPLE_TILE, steps, steps, SAMPLE_MEM_BATCH, SUBLANES, F32, kv_sample, attn_sample)
    win_s = [c.transpose(0, 4, 1, 2, 3) for c in rolled]

    mem_kv_out = mem_kv_p.transpose(0, 1, 5, 2, 3, 4)
    return (y_p.reshape(bp, seq, d), y_s.reshape(bs, steps, d), mem_kv_out,
            hre_p[None], him_p[None], win_p[0], win_p[1], win_p[2],
            hre_s[None], him_s[None], win_s[0], win_s[1], win_s[2])
```

```python
import functools
import math

import jax
import jax.numpy as jnp
from jax import lax
from jax.experimental import pallas as pl
from jax.experimental.pallas import tpu as pltpu

F32 = jnp.float32
BF16 = jnp.bfloat16

D_MODEL = 1024
HEAD_DIM = 64
MIX_WIDTH = 768
MEM_WIDTH = 256
MEM_HEADS = 4
N_MEM = 256
SSM_GROUP = 16
SSM_GROUPS = 48
SSM_STATE = 64
DIL_GROUPS = ((128, 1), (512, 4), (2048, 16))
ATT_HEADS = 12
HEADS_PER_GROUP = 4
GROUP_WIDTH = HEADS_PER_GROUP * HEAD_DIM
DEPTH = 2
DEEPNORM_ALPHA = (2.0 * DEPTH) ** 0.25
LN_EPS = 1e-5
SCALE = HEAD_DIM ** -0.5
NEG_INF = -1e30
WINDOW_KEYS = 128
LANES = 128
SUBLANES = 8
SLAB_GROUPS = LANES // SSM_GROUP
SSM_SLABS = SSM_GROUPS // SLAB_GROUPS
SLAB_STATE = SLAB_GROUPS * SSM_STATE
VMEM_LIMIT = 56 * 1024 * 1024
NT_DIMS = (((1,), (1,)), ((), ()))

PROMPT_TILE = 512
SAMPLE_TILE = 256
S5_CHUNK = 8
S5_BLOCK = 512
MEM_QUERY_TILE = 1024
SAMPLE_MEM_BATCH = 8
SAMPLE_DIL_BATCH = 2


def _slope(head):
    return 2.0 ** (-8.0 * (head + 1) / ATT_HEADS)


def _params(*semantics):
    return pltpu.CompilerParams(dimension_semantics=semantics or ("parallel",),
                                vmem_limit_bytes=VMEM_LIMIT)


def _silu(x):
    return x * jax.nn.sigmoid(x)


def _round_up(n, m):
    return -(-n // m) * m


def _proj_kernel(x_ref, w_ref, *out_refs, splits):
    x = x_ref[...].astype(BF16)
    off = 0
    for o_ref, n in zip(out_refs, splits):
        o_ref[...] = jnp.dot(x, w_ref[:, off:off + n], preferred_element_type=F32).astype(o_ref.dtype)
        off += n


def _proj(x, w, splits, tm, first_out_block=None, rest_dtype=F32):
    m, k = x.shape
    n_total = w.shape[1]
    assert sum(splits) == n_total and m % tm == 0
    tok = lambda i: (i, 0)
    first = tok if first_out_block is None else (lambda i: (first_out_block(i), 0))
    out_maps = [first] + [tok] * (len(splits) - 1)
    return pl.pallas_call(
        functools.partial(_proj_kernel, splits=splits),
        grid=(m // tm,),
        in_specs=[pl.BlockSpec((tm, k), tok), pl.BlockSpec((k, n_total), lambda i: (0, 0))],
        out_specs=[pl.BlockSpec((tm, n), om) for n, om in zip(splits, out_maps)],
        out_shape=[jax.ShapeDtypeStruct((m, n), F32 if i == 0 else rest_dtype) for i, n in enumerate(splits)],
        compiler_params=_params(),
        name="proj",
    )(x, w)


def _kv_proj_kernel(x_ref, w_kv_ref, w_in_ref, kv_ref, kvt_all_ref, kvt_tail_ref, *proj_refs,
                    tiles, wide, splits):
    x = x_ref[...].astype(BF16)
    kv = jnp.dot(x, w_kv_ref[...], preferred_element_type=F32)
    kv_ref[...] = kv

    def group_t(g):
        cols = [kv[:, i * MIX_WIDTH + g * GROUP_WIDTH:i * MIX_WIDTH + (g + 1) * GROUP_WIDTH] for i in range(2)]
        return jnp.concatenate(cols, axis=1).T

    kvt_all_ref[0] = group_t(wide)

    @pl.when(pl.program_id(0) % tiles == tiles - 1)
    def _():
        for g in range(wide):
            kvt_tail_ref[0, g * 2 * GROUP_WIDTH:(g + 1) * 2 * GROUP_WIDTH, :] = group_t(g)

    off = 0
    for o_ref, n in zip(proj_refs, splits):
        o_ref[...] = jnp.dot(x, w_in_ref[:, off:off + n], preferred_element_type=F32).astype(o_ref.dtype)
        off += n


def _kv_proj(x, w_kv, w_in, splits, rest_dtype, batch, seq, tm, wide):
    m, k = x.shape
    n_kv, n_in = w_kv.shape[1], w_in.shape[1]
    group_rows = 2 * GROUP_WIDTH
    tiles = seq // tm
    tok = lambda i: (i, 0)
    fixed = lambda i: (0, 0)
    res = pl.pallas_call(
        functools.partial(_kv_proj_kernel, tiles=tiles, wide=wide, splits=splits),
        grid=(m // tm,),
        in_specs=[pl.BlockSpec((tm, k), tok), pl.BlockSpec((k, n_kv), fixed), pl.BlockSpec((k, n_in), fixed)],
        out_specs=[pl.BlockSpec((tm, n_kv), tok),
                   pl.BlockSpec((1, group_rows, tm), lambda i: (i // tiles, 0, i % tiles)),
                   pl.BlockSpec((1, wide * group_rows, tm), lambda i: (i // tiles, 0, 0))]
        + [pl.BlockSpec((tm, n), tok) for n in splits],
        out_shape=[jax.ShapeDtypeStruct((m, n_kv), F32),
                   jax.ShapeDtypeStruct((batch, group_rows, seq), F32),
                   jax.ShapeDtypeStruct((batch, wide * group_rows, tm), F32)]
        + [jax.ShapeDtypeStruct((m, n), F32 if i == 0 else rest_dtype) for i, n in enumerate(splits)],
        compiler_params=_params("arbitrary"),
        name="kv_proj",
    )(x, w_kv, w_in)
    return res[0], res[1], res[2], res[3:]


def _mem_kv_t_kernel(w_ref, mem_ref, o_ref):
    o_ref[0, 0] = lax.dot_general(w_ref[0], mem_ref[0].astype(BF16), NT_DIMS, preferred_element_type=F32)


def _mem_kv_t(w_t, mem):
    depth, n, d = w_t.shape
    b = mem.shape[0]
    return pl.pallas_call(
        _mem_kv_t_kernel,
        grid=(depth, b),
        in_specs=[pl.BlockSpec((1, n, d), lambda l, i: (l, 0, 0)),
                  pl.BlockSpec((1, N_MEM, d), lambda l, i: (i, 0, 0))],
        out_specs=pl.BlockSpec((1, 1, n, N_MEM), lambda l, i: (l, i, 0, 0)),
        out_shape=jax.ShapeDtypeStruct((depth, b, n, N_MEM), F32),
        compiler_params=_params("parallel", "parallel"),
        name="mem_kv_t",
    )(w_t, mem)


def _s5_tables(chunk, lam_re, lam_im, log_dt, b_re, b_im, c_re, c_im, d_skip):
    hp = lax.Precision.HIGHEST
    p, c = SSM_STATE, SSM_GROUP
    ns, gs = SSM_SLABS, SLAB_GROUPS
    kl = chunk * LANES
    lr = jnp.minimum(lam_re.astype(F32), -1e-4)
    li = lam_im.astype(F32)
    dt = jnp.exp(log_dt.astype(F32))[:, None]
    def lam_bar_pow(n):
        n = n.astype(F32)[None, :, None]
        mag = jnp.exp(lr[:, None, :] * dt[:, None, :] * n)
        ang = li[:, None, :] * dt[:, None, :] * n
        return mag * jnp.cos(ang), mag * jnp.sin(ang)

    pw_re, pw_im = lam_bar_pow(jnp.arange(chunk + 1))
    nr, ni = pw_re[:, 1] - 1.0, pw_im[:, 1]
    den = lr * lr + li * li
    f_re, f_im = (nr * lr + ni * li) / den, (ni * lr - nr * li) / den
    bb_re = f_re[:, :, None] * b_re - f_im[:, :, None] * b_im
    bb_im = f_re[:, :, None] * b_im + f_im[:, :, None] * b_re
    c_re, c_im = c_re.astype(F32), c_im.astype(F32)

    x_re = pw_re[:, :chunk, :, None] * bb_re[:, None] - pw_im[:, :chunk, :, None] * bb_im[:, None]
    x_im = pw_re[:, :chunk, :, None] * bb_im[:, None] + pw_im[:, :chunk, :, None] * bb_re[:, None]
    conv = (jnp.einsum('gcp,gtpd->gtcd', c_re, x_re, precision=hp)
            - jnp.einsum('gcp,gtpd->gtcd', c_im, x_im, precision=hp))
    def slab_diag(x):
        r, w = x.shape[-2:]
        x = jnp.moveaxis(x, 1, -3)
        x = x.reshape(x.shape[:-3] + (gs * r, w))
        x = jnp.tile(x, (1,) * (x.ndim - 1) + (gs,))
        own = (jnp.arange(gs * r) // r)[:, None] == (jnp.arange(gs * w) // w)[None, :]
        return jnp.where(own, x, 0.0)

    conv_d = slab_diag(conv.transpose(0, 1, 3, 2).reshape(ns, gs, chunk, c, c))
    b_diag = lambda x: slab_diag(x.transpose(0, 2, 1).reshape(ns, gs, c, p))
    c_diag = lambda x: slab_diag(x.transpose(0, 2, 1).reshape(ns, gs, p, c))

    def slab_rows(x):
        n = x.shape[1]
        return x.reshape(ns, gs, n, p).transpose(0, 2, 1, 3).reshape(ns, n, SLAB_STATE)

    strip = conv_d.transpose(0, 2, 1, 3).reshape(ns, LANES, kl)
    toep = jnp.stack([jnp.pad(strip[:, :, :kl - s * LANES], ((0, 0), (0, 0), (s * LANES, 0)))
                      for s in range(chunk)], axis=1).reshape(ns, kl, kl)

    rev_re, rev_im = lam_bar_pow(chunk - 1 - jnp.arange(chunk))
    rev_re, rev_im = slab_rows(rev_re), slab_rows(rev_im)
    bd_re, bd_im = b_diag(bb_re), b_diag(bb_im)
    rev_a = jnp.concatenate([rev_re, rev_re], axis=-1)[:, :, None, :]
    rev_b = jnp.concatenate([-rev_im, rev_im], axis=-1)[:, :, None, :]
    bd_a = jnp.concatenate([bd_re, bd_im], axis=-1)[:, None]
    bd_b = jnp.concatenate([bd_im, bd_re], axis=-1)[:, None]
    w_in = (rev_a * bd_a + rev_b * bd_b).reshape(ns, kl, 2 * SLAB_STATE)

    col_re = slab_rows(pw_re[:, 1:]).transpose(0, 2, 1)
    col_im = slab_rows(pw_im[:, 1:]).transpose(0, 2, 1)
    cd_re, cd_im = c_diag(c_re), c_diag(c_im)
    col_a = jnp.concatenate([col_re, -col_re], axis=1)
    col_b = jnp.concatenate([-col_im, -col_im], axis=1)
    cd_a = jnp.concatenate([cd_re, cd_im], axis=1)
    cd_b = jnp.concatenate([cd_im, cd_re], axis=1)
    w_out = jnp.concatenate([col_a[:, :, t:t + 1] * cd_a + col_b[:, :, t:t + 1] * cd_b
                             for t in range(chunk)], axis=-1)

    lam_pows = jnp.stack([slab_rows(pw_re).transpose(1, 0, 2),
                          slab_rows(pw_im).transpose(1, 0, 2)], axis=2)
    skip = jnp.tile(d_skip.astype(F32).reshape(ns, LANES), (1, chunk))[:, None]
    return toep.astype(BF16), w_in.astype(BF16), w_out.astype(BF16), lam_pows, skip


def _s5_kernel(u_ref, toep_ref, win_ref, wout_ref, lam_ref, skip_ref, h0_ref, y_ref, hout_ref,
               a_scr, g_scr, hs_scr, y_scr, h_scr, *, batch, tk, chunk):
    n_chunks = tk // chunk
    ns = SLAB_STATE

    @pl.when(pl.program_id(1) == 0)
    def _():
        h_scr[...] = h0_ref[0]

    tile_swap = batch == SUBLANES and chunk == SUBLANES

    def gather(kk, carry):
        rows = pl.ds(pl.multiple_of(kk * batch, SUBLANES), batch)
        if tile_swap:
            tiles = [u_ref[pl.ds(pl.multiple_of(b * tk + kk * chunk, SUBLANES), chunk), :] for b in range(batch)]
            by_token = pltpu.einshape("btl->tbl", jnp.stack(tiles, axis=0))
            for t in range(chunk):
                a_scr[rows, t * LANES:(t + 1) * LANES] = by_token[t]
            return carry
        for t in range(chunk):
            a_scr[rows, t * LANES:(t + 1) * LANES] = u_ref[pl.ds(kk * chunk + t, batch, stride=tk), :]
        return carry

    lax.fori_loop(0, n_chunks, gather, 0)
    a = a_scr[...].astype(BF16)
    g_scr[...] = jnp.dot(a, win_ref[0], preferred_element_type=F32)
    ar = lam_ref[0, 0:1, :]
    ai = lam_ref[0, 1:2, :]

    def step(kk, carry):
        hr, hi = carry
        rows = pl.ds(pl.multiple_of(kk * batch, SUBLANES), batch)
        hs_scr[rows, 0:ns] = hr
        hs_scr[rows, ns:2 * ns] = hi
        return (ar * hr - ai * hi + g_scr[rows, 0:ns], ar * hi + ai * hr + g_scr[rows, ns:2 * ns])

    hr, hi = lax.fori_loop(0, n_chunks, step, (h_scr[:, 0:ns], h_scr[:, ns:2 * ns]))
    h_scr[:, 0:ns] = hr
    h_scr[:, ns:2 * ns] = hi
    hout_ref[0, :, 0:ns] = hr
    hout_ref[0, :, ns:2 * ns] = hi
    y = (jnp.dot(a, toep_ref[0], preferred_element_type=F32)
         + jnp.dot(hs_scr[...].astype(BF16), wout_ref[0], preferred_element_type=F32)
         + skip_ref[0] * a_scr[...])
    y_scr[...] = jax.nn.gelu(y)

    def scatter(kk, carry):
        rows = pl.ds(pl.multiple_of(kk * batch, SUBLANES), batch)
        if tile_swap:
            by_token = jnp.stack([y_scr[rows, t * LANES:(t + 1) * LANES] for t in range(chunk)], axis=0)
            by_seq = pltpu.einshape("tbl->btl", by_token)
            for b in range(batch):
                y_ref[pl.ds(pl.multiple_of(b * tk + kk * chunk, SUBLANES), chunk), :] = by_seq[b]
            return carry
        for t in range(chunk):
            y_ref[pl.ds(kk * chunk + t, batch, stride=tk), :] = y_scr[rows, t * LANES:(t + 1) * LANES]
        return carry

    lax.fori_loop(0, n_chunks, scatter, 0)


def _s5_mixer(u, h0_re, h0_im, tables, batch, seq, tk, chunk):
    toep, w_in, w_out, lam_pows, skip = tables
    kl = chunk * LANES
    table_chunk = toep.shape[-1] // LANES
    assert (table_chunk - chunk) % chunk == 0
    w_in_block = (table_chunk - chunk) // chunk
    lam_l = lam_pows[chunk]
    n_blocks = seq // tk
    rows = (tk // chunk) * batch
    to_slabs = lambda h: h.reshape(batch, SSM_SLABS, SLAB_STATE).transpose(1, 0, 2)
    h0 = jnp.concatenate([to_slabs(h0_re), to_slabs(h0_im)], axis=-1)
    slab = lambda j, k: (j, 0, 0)
    y, h = pl.pallas_call(
        functools.partial(_s5_kernel, batch=batch, tk=tk, chunk=chunk),
        grid=(SSM_SLABS, n_blocks),
        in_specs=[pl.BlockSpec((batch * tk, LANES), lambda j, k: (k, j)),
                  pl.BlockSpec((1, kl, kl), slab),
                  pl.BlockSpec((1, kl, 2 * SLAB_STATE), lambda j, k: (j, w_in_block, 0)),
                  pl.BlockSpec((1, 2 * SLAB_STATE, kl), slab),
                  pl.BlockSpec((1, 2, SLAB_STATE), slab),
                  pl.BlockSpec((1, 1, kl), slab),
                  pl.BlockSpec((1, batch, 2 * SLAB_STATE), slab)],
        out_specs=[pl.BlockSpec((batch * tk, LANES), lambda j, k: (k, j)),
                   pl.BlockSpec((1, batch, 2 * SLAB_STATE), slab)],
        out_shape=[jax.ShapeDtypeStruct(u.shape, F32),
                   jax.ShapeDtypeStruct((SSM_SLABS, batch, 2 * SLAB_STATE), F32)],
        scratch_shapes=[pltpu.VMEM((rows, kl), F32), pltpu.VMEM((rows, 2 * SLAB_STATE), F32),
                        pltpu.VMEM((rows, 2 * SLAB_STATE), F32), pltpu.VMEM((rows, kl), F32),
                        pltpu.VMEM((batch, 2 * SLAB_STATE), F32)],
        compiler_params=_params("parallel", "arbitrary"),
        name="s5_chunks",
    )(u, toep, w_in, w_out, lam_l, skip, h0)
    from_slabs = lambda x: x.transpose(1, 0, 2).reshape(batch, SSM_GROUPS, SSM_STATE)
    return y, from_slabs(h[:, :, :SLAB_STATE]), from_slabs(h[:, :, SLAB_STATE:])


def _head_rows(x, heads):
    t, w = x.shape
    row_head = lax.broadcasted_iota(jnp.int32, (heads * t, w), 0) // t
    lane_head = lax.broadcasted_iota(jnp.int32, (heads * t, w), 1) // HEAD_DIM
    mask = row_head == lane_head
    return jnp.where(mask, jnp.concatenate([x] * heads, axis=0), 0.0), mask


def _head_cols(x, mask, heads):
    t = x.shape[0] // heads
    x = jnp.where(mask, x, 0.0)
    out = x[0:t]
    for h in range(1, heads):
        out = out + x[h * t:(h + 1) * t]
    return out


def _mem_attn_kernel(q_ref, kv_ref, mg_ref, o_ref, *, block_batch):
    for bi in range(block_batch):
        q_bd, mask = _head_rows(q_ref[bi].astype(F32), MEM_HEADS)
        kt = jnp.concatenate([kv_ref[0, bi, 0, h] for h in range(MEM_HEADS)], axis=0).astype(BF16)
        vt = jnp.concatenate([kv_ref[0, bi, 1, h] for h in range(MEM_HEADS)], axis=0).astype(BF16)
        s = jnp.dot(q_bd.astype(BF16), kt, preferred_element_type=F32) * SCALE
        p = jnp.exp(s - jnp.max(s, axis=-1, keepdims=True))
        l = jnp.sum(p, axis=-1, keepdims=True)
        o = lax.dot_general(p.astype(BF16), vt, NT_DIMS, preferred_element_type=F32) / l
        o_ref[bi] = (_head_cols(o, mask, MEM_HEADS) * _silu(mg_ref[bi].astype(F32))).astype(o_ref.dtype)


def _mem_attn(q, kv_t, layer, mgate, block_batch, tq):
    b, t, w = q.shape
    tok = lambda i, j: (i, j, 0)
    return pl.pallas_call(
        functools.partial(_mem_attn_kernel, block_batch=block_batch),
        grid=(b // block_batch, t // tq),
        in_specs=[pl.BlockSpec((block_batch, tq, w), tok),
                  pl.BlockSpec((1, block_batch) + kv_t.shape[2:], lambda i, j: (layer, i, 0, 0, 0, 0)),
                  pl.BlockSpec((block_batch, tq, w), tok)],
        out_specs=pl.BlockSpec((block_batch, tq, w), tok),
        out_shape=jax.ShapeDtypeStruct((b, t, w), q.dtype),
        compiler_params=_params("parallel", "parallel"),
        name="mem_attn",
    )(q, kv_t, mgate)


def _mix_plain(tok_refs, fixed_refs):
    (y_ref,) = tok_refs
    return y_ref[...]


def _mix_glu(tok_refs, fixed_refs):
    y_ref, gate_ref = tok_refs
    w_ref, b_ref = fixed_refs
    y = y_ref[...]
    z = jnp.dot(y.astype(BF16), w_ref[...], preferred_element_type=F32) + b_ref[...]
    return y * jax.nn.sigmoid(z) * _silu(gate_ref[...].astype(F32))


def _mix_merge(tok_refs, fixed_refs):
    o0, o1, o2, l0, l1, l2, gate_ref = tok_refs
    ls = [l0[...], l1[...], l2[...]]
    mx = jnp.maximum(jnp.maximum(ls[0], ls[1]), ls[2])
    es = [jnp.exp(l - mx) for l in ls]
    den = es[0] + es[1] + es[2]
    o = jnp.concatenate([o_ref[...] * (e / den) for o_ref, e in zip((o0, o1, o2), es)], axis=1)
    return o * _silu(gate_ref[...].astype(F32))


def _layer_tail_kernel(*refs, n_tok, n_fixed, mix_fn):
    mix = mix_fn(refs[:n_tok], refs[n_tok:n_tok + n_fixed])
    m_ref, x_ref, wy_ref, wm_ref, g_ref, b_ref, o_ref = refs[n_tok + n_fixed:]
    out = (jnp.dot(mix.astype(BF16), wy_ref[...], preferred_element_type=F32)
           + jnp.dot(m_ref[...].astype(BF16), wm_ref[...], preferred_element_type=F32))
    z = DEEPNORM_ALPHA * x_ref[...] + out
    mu = jnp.mean(z, axis=-1, keepdims=True)
    zc = z - mu
    var = jnp.mean(zc * zc, axis=-1, keepdims=True)
    o_ref[...] = zc * lax.rsqrt(var + LN_EPS) * g_ref[...] + b_ref[...]


def _layer_tail(mix_fn, tok_arrays, fixed_arrays, m, x, w_out, ln_g, ln_b, tm, first_block=None):
    rows, d = x.shape
    tok = lambda i: (i, 0)
    fixed = lambda i: (0, 0)
    first = tok if first_block is None else (lambda i: (first_block(i), 0))
    tok_maps = [first] + [tok] * (len(tok_arrays) - 1)
    return pl.pallas_call(
        functools.partial(_layer_tail_kernel, n_tok=len(tok_arrays), n_fixed=len(fixed_arrays), mix_fn=mix_fn),
        grid=(rows // tm,),
        in_specs=[pl.BlockSpec((tm, a.shape[1]), tm_map) for a, tm_map in zip(tok_arrays, tok_maps)]
        + [pl.BlockSpec(a.shape, fixed) for a in fixed_arrays]
        + [pl.BlockSpec((tm, MEM_WIDTH), tok), pl.BlockSpec((tm, d), tok),
           pl.BlockSpec((MIX_WIDTH, d), fixed), pl.BlockSpec((MEM_WIDTH, d), fixed),
           pl.BlockSpec((1, d), fixed), pl.BlockSpec((1, d), fixed)],
        out_specs=pl.BlockSpec((tm, d), tok),
        out_shape=jax.ShapeDtypeStruct((rows, d), F32),
        compiler_params=_params(),
        name="layer_tail",
    )(*tok_arrays, *fixed_arrays, m, x, w_out[:MIX_WIDTH], w_out[MIX_WIDTH:], ln_g, ln_b)


def _dil_prompt_kernel(q_ref, k_ref, v_ref, o_ref, lse_ref, *, seq, dil, group):
    n = WINDOW_KEYS
    n_blocks = seq // dil // n
    heads = LANES // HEAD_DIM
    first_head = group * HEADS_PER_GROUP
    second_pair = pl.program_id(1) == 1
    slopes = [jnp.where(second_pair, _slope(first_head + heads + h), _slope(first_head + h))
              for h in range(heads)]
    ku = lax.broadcasted_iota(jnp.int32, (2 * n, n), 0)
    qi = lax.broadcasted_iota(jnp.int32, (2 * n, n), 1)
    delta2 = qi + n - ku
    valid2 = jnp.where(delta2 >= 0, delta2, n + 1) <= n
    delta1 = delta2[n:, :]
    valid1 = delta1 >= 0
    bias2 = [s * (delta2 * dil).astype(F32) for s in slopes]
    bias1 = [s * (delta1 * dil).astype(F32) for s in slopes]
    lane_head = lax.broadcasted_iota(jnp.int32, (1, LANES), 1) // HEAD_DIM
    head_lanes = [(lane_head == h).astype(F32) for h in range(heads)]
    tn_dims = (((0,), (0,)), ((), ()))

    def stream_rows(start):
        return pl.ds(start, n, stride=dil) if dil > 1 else pl.ds(start, n)

    def block(start, has_prev):
        cur = stream_rows(start)
        q = q_ref[0, cur, :]
        k = k_ref[0, cur, :]
        v = v_ref[0, cur, :]
        if has_prev:
            prev = stream_rows(start - n * dil)
            k = jnp.concatenate([k_ref[0, prev, :], k], axis=0)
            v = jnp.concatenate([v_ref[0, prev, :], v], axis=0)
            bias, valid = bias2, valid2
        else:
            bias, valid = bias1, valid1
        q = q * SCALE
        o = jnp.zeros((n, LANES), F32)
        lse_b = jnp.zeros((n, LANES), F32)
        k = k.astype(BF16)
        for h in range(heads):
            s = lax.dot_general(k, (q * head_lanes[h]).astype(BF16),
                                NT_DIMS, preferred_element_type=F32)
            s = jnp.where(valid, s - bias[h], NEG_INF)
            m = jnp.max(s, axis=0, keepdims=True)
            p = jnp.exp(s - m)
            l = jnp.sum(p, axis=0, keepdims=True)
            p = (p * (1.0 / l)).astype(BF16)
            o = o + lax.dot_general(p, (v * head_lanes[h]).astype(BF16), tn_dims, preferred_element_type=F32)
            lse = m + jnp.log(l)
            t1 = lse.astype(BF16)
            r1 = lse - t1.astype(F32)
            t2 = r1.astype(BF16)
            t3 = (r1 - t2.astype(F32)).astype(BF16)
            terms = jnp.concatenate([t1, t2, t3, jnp.zeros((SUBLANES - 3, n), BF16)], axis=0)
            spread = jnp.broadcast_to(head_lanes[h], (SUBLANES, LANES)).astype(BF16)
            lse_b = lse_b + lax.dot_general(terms, spread, tn_dims, preferred_element_type=F32)
        o_ref[0, cur, :] = o
        lse_ref[0, cur, :] = lse_b

    for r in range(dil):
        for jb in range(n_blocks):
            block(r + jb * n * dil, jb > 0)


def _dil_prompt(q, kv, group):
    b, seq, _ = q.shape
    _, dil = DIL_GROUPS[group]
    pairs = GROUP_WIDTH // LANES
    k_col = group * pairs
    v_col = MIX_WIDTH // LANES + k_col
    blk = (1, seq, LANES)
    return pl.pallas_call(
        functools.partial(_dil_prompt_kernel, seq=seq, dil=dil, group=group),
        grid=(b, pairs),
        in_specs=[pl.BlockSpec(blk, lambda i, j: (i, 0, k_col + j)),
                  pl.BlockSpec(blk, lambda i, j: (i, 0, k_col + j)),
                  pl.BlockSpec(blk, lambda i, j: (i, 0, v_col + j))],
        out_specs=[pl.BlockSpec(blk, lambda i, j: (i, 0, j)), pl.BlockSpec(blk, lambda i, j: (i, 0, j))],
        out_shape=[jax.ShapeDtypeStruct((b, seq, GROUP_WIDTH), F32)] * 2,
        compiler_params=_params("parallel", "parallel"),
        name=f"dil_prompt_g{group}",
    )(q, kv, kv)


def _dil_sample_kernel(q_ref, kvn_ref, gate_ref, c0_ref, c1_ref, c2_ref,
                       y_ref, n0_ref, n1_ref, n2_ref, *, steps):
    rows = q_ref.shape[1]
    lead = rows - steps
    q_all = q_ref[0]
    kvn = kvn_ref[0]
    hg = HEADS_PER_GROUP
    stacked = lax.broadcasted_iota(jnp.int32, (hg * rows, 1), 0)
    row_head = stacked // rows
    tok = stacked - row_head * rows - lead
    kvn_t = jnp.concatenate([jnp.zeros((LANES - rows, 2 * MIX_WIDTH), F32), kvn], axis=0).T
    tail_lane = lax.broadcasted_iota(jnp.int32, (HEAD_DIM, LANES), 1) >= LANES - steps
    outs, lses = [], []
    for g, ((win, dil), c_ref, n_ref) in enumerate(zip(DIL_GROUPS, (c0_ref, c1_ref, c2_ref),
                                                       (n0_ref, n1_ref, n2_ref))):
        cols = slice(g * GROUP_WIDTH, (g + 1) * GROUP_WIDTH)
        k_new = kvn[:, cols]
        v_new = kvn[:, MIX_WIDTH + g * GROUP_WIDTH:MIX_WIDTH + (g + 1) * GROUP_WIDTH]
        kv_heads = [[c_ref[0, kv_i, h] for h in range(hg)] for kv_i in range(2)]
        for kv_i in range(2):
            for h in range(hg):
                shifted = pltpu.roll(kv_heads[kv_i][h], win - steps, axis=1)
                lo = kv_i * MIX_WIDTH + g * GROUP_WIDTH + h * HEAD_DIM
                new_t = kvn_t[lo:lo + HEAD_DIM, :]
                if win > LANES:
                    n_ref[0, kv_i, h, :, 0:win - LANES] = shifted[:, 0:win - LANES]
                n_ref[0, kv_i, h, :, win - LANES:win] = jnp.where(tail_lane, new_t, shifted[:, win - LANES:win])
        kt = jnp.concatenate(kv_heads[0], axis=0).astype(BF16)
        vt = jnp.concatenate(kv_heads[1], axis=0).astype(BF16)
        q_bd, head_mask = _head_rows(q_all[:, cols], hg)
        slope = jnp.zeros((hg * rows, 1), F32)
        for h in range(hg):
            slope = jnp.where(row_head == h, _slope(g * hg + h), slope)
        pos = lax.broadcasted_iota(jnp.int32, (hg * rows, win), 1)
        dist_buf = ((win + tok) - pos).astype(F32)
        valid_buf = jnp.where(pos >= tok, (pos - tok) & (dil - 1), 1) == 0
        s_buf = jnp.dot(q_bd.astype(BF16), kt, preferred_element_type=F32) * SCALE - slope * dist_buf
        s_buf = jnp.where(valid_buf, s_buf, NEG_INF)
        s_new = []
        for t2 in range(steps):
            col = jnp.sum(q_bd * k_new[lead + t2:lead + t2 + 1, :], axis=-1, keepdims=True) * SCALE \
                - slope * (tok - t2).astype(F32)
            ok = jnp.where(tok >= t2, (tok - t2) & (dil - 1), 1) == 0
            s_new.append(jnp.where(ok, col, NEG_INF))
        m = jnp.max(s_buf, axis=-1, keepdims=True)
        for col in s_new:
            m = jnp.maximum(m, col)
        p_buf = jnp.exp(s_buf - m)
        l = jnp.sum(p_buf, axis=-1, keepdims=True)
        o = lax.dot_general(p_buf.astype(BF16), vt, NT_DIMS, preferred_element_type=F32)
        for t2, col in enumerate(s_new):
            p_col = jnp.exp(col - m)
            l = l + p_col
            o = o + p_col * v_new[lead + t2:lead + t2 + 1, :]
        outs.append(o / l)
        lses.append(m + jnp.log(l))
    mx = jnp.maximum(jnp.maximum(lses[0], lses[1]), lses[2])
    es = [jnp.exp(l - mx) for l in lses]
    den = es[0] + es[1] + es[2]
    mixed = [_head_cols(outs[g] * (es[g] / den), head_mask, hg) for g in range(len(DIL_GROUPS))]
    y_ref[0] = jnp.concatenate(mixed, axis=1) * _silu(gate_ref[0])


def _dil_sample_block_kernel(*refs, steps, block_batch):
    for bi in range(block_batch):
        _dil_sample_kernel(*[r.at[pl.ds(bi, 1)] for r in refs], steps=steps)


def _dil_sample(q, kv_new, gate, caches_t, block_batch=SAMPLE_DIL_BATCH):
    b, steps, _ = q.shape
    block_batch = math.gcd(b, block_batch)
    rows = _round_up(steps, SUBLANES)
    pad = ((0, 0), (rows - steps, 0), (0, 0))
    q, kv_new, gate = jnp.pad(q, pad), jnp.pad(kv_new, pad), jnp.pad(gate, pad)
    tok = lambda i: (i, 0, 0)
    cache_specs = [pl.BlockSpec((block_batch,) + c.shape[1:], lambda i: (i, 0, 0, 0, 0)) for c in caches_t]
    res = pl.pallas_call(
        functools.partial(_dil_sample_block_kernel, steps=steps, block_batch=block_batch),
        grid=(b // block_batch,),
        in_specs=[pl.BlockSpec((block_batch, rows, MIX_WIDTH), tok),
                  pl.BlockSpec((block_batch, rows, 2 * MIX_WIDTH), tok),
                  pl.BlockSpec((block_batch, rows, MIX_WIDTH), tok)] + cache_specs,
        out_specs=[pl.BlockSpec((block_batch, rows, MIX_WIDTH), tok)] + cache_specs,
        out_shape=[jax.ShapeDtypeStruct((b, rows, MIX_WIDTH), F32)]
        + [jax.ShapeDtypeStruct(c.shape, F32) for c in caches_t],
        compiler_params=_params(),
        name="dil_sample",
    )(q, kv_new, gate, *caches_t)
    return [res[0][:, rows - steps:]] + list(res[1:])


def _trunk(x, batch, seq, mem_kv_t, h0_re, h0_im, weights, tm, s5_tk, s5_chunk, mem_block, mem_tq,
           side_dtype, kv_proj, dil_attn):
    (w_in, w_out, ln_g, ln_b, tables, w_glu, b_glu) = weights
    splits = (MIX_WIDTH, MIX_WIDTH, MEM_WIDTH, MEM_WIDTH)
    seq_mem = _round_up(seq, SUBLANES)

    def mem_branch(mq, mgate, layer):
        mq = mq.reshape(batch, seq, MEM_WIDTH)
        mgate = mgate.reshape(batch, seq, MEM_WIDTH)
        if seq_mem != seq:
            pad = ((0, 0), (0, seq_mem - seq), (0, 0))
            mq, mgate = jnp.pad(mq, pad), jnp.pad(mgate, pad)
        m = _mem_attn(mq, mem_kv_t, layer, mgate, mem_block, min(mem_tq, seq_mem))
        return m[:, :seq].reshape(batch * seq, MEM_WIDTH)

    if s5_tk == seq:
        block_of = None
    else:
        assert s5_tk % tm == 0 and seq % s5_tk == 0
        per_seq, per_block = seq // tm, s5_tk // tm

        def block_of(i):
            b, t = i // per_seq, i % per_seq
            return ((t // per_block) * batch + b) * per_block + t % per_block
    u, gate, mq, mgate = _proj(x, w_in[0], splits, tm, first_out_block=block_of, rest_dtype=side_dtype)
    y, h_re, h_im = _s5_mixer(u, h0_re, h0_im, tables, batch, seq, s5_tk, s5_chunk)
    m = mem_branch(mq, mgate, 0)
    x = _layer_tail(_mix_glu, (y, gate), (w_glu, b_glu), m, x, w_out[0], ln_g[0:1], ln_b[0:1], tm,
                    first_block=block_of)
    kv, (u, gate, mq, mgate), kv_extras = kv_proj(x, w_in[1], splits, side_dtype)

    mix_fn, mix_inputs, attn_extras = dil_attn(u, kv, gate)
    m = mem_branch(mq, mgate, 1)
    x = _layer_tail(mix_fn, mix_inputs, (), m, x, w_out[1], ln_g[1:2], ln_b[1:2], tm)
    return x, h_re, h_im, kv, kv_extras, attn_extras


def kernel(x_prompt, x_sample, cache_mem_kv, state_ssm_re, state_ssm_im, cache_dil1_kv, cache_dil4_kv,
           cache_dil16_kv, mem_prompt, w_in, w_out, ln_g, ln_b, w_mem_kv, ssm_lambda_re, ssm_lambda_im,
           ssm_log_dt, ssm_b_re, ssm_b_im, ssm_c_re, ssm_c_im, ssm_d, w_glu, b_glu, w_kv_shared):
    bp, seq, d = x_prompt.shape
    bs, steps, _ = x_sample.shape
    prompt_chunk, prompt_tk = S5_CHUNK, PROMPT_TILE
    n_groups = len(DIL_GROUPS)

    w_in_b = w_in.astype(BF16)
    w_out_b = w_out.astype(BF16)
    w_glu_b = w_glu[0].astype(BF16)
    w_kv_b = w_kv_shared.astype(BF16)
    ssm = (ssm_lambda_re[0], ssm_lambda_im[0], ssm_log_dt[0], ssm_b_re[0], ssm_b_im[0],
           ssm_c_re[0], ssm_c_im[0], ssm_d[0])

    assert prompt_chunk % steps == 0
    weights = (w_in_b, w_out_b, ln_g, ln_b, _s5_tables(prompt_chunk, *ssm), w_glu_b, b_glu)

    mem_kv_p = _mem_kv_t(w_mem_kv.transpose(0, 2, 1).astype(BF16), mem_prompt)
    mem_kv_p = mem_kv_p.reshape(DEPTH, bp, 2, MEM_HEADS, HEAD_DIM, N_MEM)
    zeros = jnp.zeros((bp, SSM_GROUPS, SSM_STATE), F32)
    wide = n_groups - 1
    assert DIL_GROUPS[wide][0] >= seq and all(w <= prompt_tk for w, _ in DIL_GROUPS[:wide])

    def kv_prompt(x, w_in_next, splits, side_dtype):
        kv, kvt_all, kvt_tail, proj = _kv_proj(x, w_kv_b, w_in_next, splits, side_dtype, bp, seq, prompt_tk, wide)
        return kv, proj, (kvt_all, kvt_tail)

    def attn_prompt(u, kv, gate):
        q3 = u.reshape(bp, seq, MIX_WIDTH)
        kv3 = kv.reshape(bp, seq, 2 * MIX_WIDTH)
        res = [_dil_prompt(q3, kv3, g) for g in range(n_groups)]
        outs = [r[0].reshape(bp * seq, GROUP_WIDTH) for r in res]
        lses = [r[1].reshape(bp * seq, GROUP_WIDTH) for r in res]
        return _mix_merge, (*outs, *lses, gate), None

    y_p, hre_p, him_p, _, (kvt_all, kvt_tail), _ = _trunk(
        x_prompt.reshape(bp * seq, d), bp, seq, mem_kv_p, zeros, zeros, weights,
        prompt_tk, S5_BLOCK, prompt_chunk, 1, MEM_QUERY_TILE, BF16, kv_prompt, attn_prompt)
    kvt_tail = kvt_tail.reshape(bp, wide, 2, HEADS_PER_GROUP, HEAD_DIM, prompt_tk)
    win_p = [kvt_tail[:, g, :, :, :, prompt_tk - win:].transpose(0, 4, 1, 2, 3)
             for g, (win, _) in enumerate(DIL_GROUPS[:wide])]
    win_p.append(kvt_all.reshape(bp, 2, HEADS_PER_GROUP, HEAD_DIM, seq).transpose(0, 4, 1, 2, 3))

    caches = (cache_dil1_kv, cache_dil4_kv, cache_dil16_kv)
    caches_t = [c.transpose(0, 2, 3, 4, 1) for c in caches]
    mem_kv_s = cache_mem_kv.transpose(0, 1, 3, 4, 5, 2)

    def kv_sample(x, w_in_next, splits, side_dtype):
        (kv,) = _proj(x, w_kv_b, (2 * MIX_WIDTH,), SAMPLE_TILE)
        return kv, _proj(x, w_in_next, splits, SAMPLE_TILE, rest_dtype=side_dtype), None

    def attn_sample(u, kv, gate):
        res = _dil_sample(u.reshape(bs, steps, MIX_WIDTH), kv.reshape(bs, steps, 2 * MIX_WIDTH),
                          gate.reshape(bs, steps, MIX_WIDTH), caches_t)
        return _mix_plain, (res[0].reshape(bs * steps, MIX_WIDTH),), res[1:]

    y_s, hre_s, him_s, _, _, rolled = _trunk(
        x_sample.reshape(bs * steps, d), bs, steps, mem_kv_s, state_ssm_re[0], state_ssm_im[0],
        weights, SAMPLE_TILE, steps, steps, SAMPLE_MEM_BATCH, SUBLANES, F32, kv_sample, attn_sample)
    win_s = [c.transpose(0, 4, 1, 2, 3) for c in rolled]

    mem_kv_out = mem_kv_p.transpose(0, 1, 5, 2, 3, 4)
    return (y_p.reshape(bp, seq, d), y_s.reshape(bs, steps, d), mem_kv_out,
            hre_p[None], him_p[None], win_p[0], win_p[1], win_p[2],
            hre_s[None], him_s[None], win_s[0], win_s[1], win_s[2])
```
